```python
import jax
import jax.numpy as jnp
from jax import lax
import numpy as np

D_MODEL = 1024
BATCH = 8
SEQ = 8192
DEPTH = 4

GRID_W = 64
CTX_LEN = 256
BLOCK = 128

RNN_WIDTH = D_MODEL
RNN_HEADS = RNN_WIDTH // 128
RNN_HEAD_DIM = RNN_WIDTH // RNN_HEADS
CONV_W = 4
CONV_LEFT = CONV_W // 2
LRU_C = 8.0
CMLP_WIDTH = D_MODEL // 2
CMLP_GROUPS = 4
CHUNK = 128
HEAD_DIM = 64
C_Q_HEADS = D_MODEL // (2 * HEAD_DIM)
C_KV_HEADS = C_Q_HEADS // 4
D_Q_HEADS = D_MODEL // (2 * HEAD_DIM)
D_KV_HEADS = D_Q_HEADS // 4
WINDOW = 128
ROPE_THETA = 10000.0
NEG_INF = -1e30
FFN_HIDDEN = -(-8 * D_MODEL // (3 * 256)) * 256

EVEN_SPLITS = (RNN_WIDTH, RNN_WIDTH, CMLP_WIDTH, CMLP_WIDTH)
EVEN_IN = sum(EVEN_SPLITS)
EVEN_MIX = RNN_WIDTH + CMLP_WIDTH
Q_SPLITS = (C_Q_HEADS * HEAD_DIM, D_Q_HEADS * HEAD_DIM)
KV_SPLITS = (C_KV_HEADS * HEAD_DIM, C_KV_HEADS * HEAD_DIM, D_KV_HEADS * HEAD_DIM, D_KV_HEADS * HEAD_DIM)
Q_COLS = sum(Q_SPLITS)
ODD_SPLITS = Q_SPLITS + KV_SPLITS
ODD_IN = sum(ODD_SPLITS)
ODD_MIX = Q_COLS

kernel_name = 'hybrid_rglru_gmlp_gqa_swa_prefix_trunk'


def _split(z, sizes):
    cuts = [int(s) for s in np.cumsum(sizes)[:-1]]
    return jnp.split(z, cuts, axis=-1)


def _heads(z, n):
    return z.reshape(z.shape[:-1] + (n, HEAD_DIM))


def _group(q, n_kv):
    return q.reshape(q.shape[:2] + (n_kv, q.shape[2] // n_kv, HEAD_DIM))


def _normalise(x, eps=1e-6):
    xf = x.astype(jnp.float32)
    xc = xf - jnp.mean(xf, axis=-1, keepdims=True)
    return (xc * lax.rsqrt(jnp.mean(xc * xc, axis=-1, keepdims=True) + eps)).astype(x.dtype)


def layer_norm(x, g, b):
    return _normalise(x) * g + b


def rms_norm(x, g, eps=1e-6):
    xf = x.astype(jnp.float32)
    return (xf * lax.rsqrt(jnp.mean(xf * xf, axis=-1, keepdims=True) + eps)).astype(x.dtype) * g


def modulate(x, shift, scale):
    return x * (1.0 + scale) + shift


def swiglu(h, w_in, w_out):
    gate, up = jnp.split(h @ w_in, 2, axis=-1)
    return (jax.nn.silu(gate) * up) @ w_out


def rope_tables(n_tokens, dtype):
    rows = n_tokens // GRID_W
    row = jnp.repeat(jnp.arange(rows, dtype=jnp.float32), GRID_W)
    col = jnp.tile(jnp.arange(GRID_W, dtype=jnp.float32), rows)
    nf = HEAD_DIM // 4
    inv_freq = ROPE_THETA ** (-jnp.arange(nf, dtype=jnp.float32) / nf)
    ang = jnp.stack([row[:, None] * inv_freq, col[:, None] * inv_freq], axis=1)
    return jnp.cos(ang).astype(dtype), jnp.sin(ang).astype(dtype)


def apply_rope_2d(x, cos, sin):
    nf = HEAD_DIM // 4
    xs = x.reshape(x.shape[:-1] + (2, 2, nf))
    x1, x2 = xs[..., 0, :], xs[..., 1, :]
    c = cos[None, :, None]
    s = sin[None, :, None]
    out = jnp.stack([x1 * c - x2 * s, x2 * c + x1 * s], axis=-2)
    return out.reshape(x.shape)


def centred_conv(x, w, b):
    T = x.shape[1]
    xp = jnp.pad(x, ((0, 0), (CONV_LEFT, CONV_W - 1 - CONV_LEFT), (0, 0)))
    out = xp[:, 0:T] * w[0]
    for k in range(1, CONV_W):
        out = out + xp[:, k:k + T] * w[k]
    return out + b


def rglru_coeffs(x, gate_w, gate_b, lam):
    B, T, _ = x.shape
    xh = x.reshape(B, T, RNN_HEADS, RNN_HEAD_DIM)
    gates = jnp.einsum('bthi,khij->kbthj', xh, gate_w.astype(jnp.float32)).reshape(2, B, T, RNN_WIDTH)
    gates = gates + gate_b.astype(jnp.float32)[:, None, None, :]
    r = jax.nn.sigmoid(gates[0])
    i = jax.nn.sigmoid(gates[1])
    log_a = -LRU_C * r * jax.nn.softplus(-lam.astype(jnp.float32))
    a = jnp.exp(log_a)
    b = jnp.sqrt(-jnp.expm1(2.0 * log_a)) * (i * x)
    return a, b


def linear_scan(a, b, reverse, h0=None):
    def combine(e1, e2):
        return e1[0] * e2[0], e2[0] * e1[1] + e2[1]
    acc_a, h = lax.associative_scan(combine, (a, b), reverse=reverse, axis=1)
    if h0 is None:
        return h
    return h + acc_a * h0[:, None, :]


def rg_lru_bidir(x_lat, x_ctx, gate_w, gate_b, lam, with_ctx_out):
    dtype = x_lat.dtype
    x_lat = x_lat.astype(jnp.float32)
    x_ctx = x_ctx.astype(jnp.float32)
    lat_out, ctx_out = [], []
    for d, reverse in enumerate((False, True)):
        a_c, b_c = rglru_coeffs(x_ctx, gate_w[d], gate_b[d], lam[d])
        h_c = linear_scan(a_c, b_c, reverse)
        h0 = h_c[:, 0] if reverse else h_c[:, -1]
        a_l, b_l = rglru_coeffs(x_lat, gate_w[d], gate_b[d], lam[d])
        lat_out.append(linear_scan(a_l, b_l, reverse, h0))
        ctx_out.append(h_c)
    y_lat = (lat_out[0] + lat_out[1]).astype(dtype)
    y_ctx = (ctx_out[0] + ctx_out[1]).astype(dtype) if with_ctx_out else None
    return y_lat, y_ctx


def chunk_gmlp(u, v, ws, bs):
    B, T, _ = v.shape
    vh = _normalise(v).reshape(B, T // CHUNK, CHUNK, CMLP_GROUPS, CMLP_WIDTH // CMLP_GROUPS)
    mixed = jnp.einsum('gpq,bnqgd->bnpgd', ws, vh) + bs.T[:, :, None]
    return u * mixed.reshape(B, T, CMLP_WIDTH)


def even_mixer(h_lat, h_ctx, w_in, w_out, conv_w, conv_b, gate_w, gate_b, lam, ws, bs, with_ctx_out):
    gate, xr, u, v = _split(h_lat @ w_in, EVEN_SPLITS)
    if with_ctx_out:
        gate_c, xr_c, u_c, v_c = _split(h_ctx @ w_in, EVEN_SPLITS)
    else:
        xr_c = h_ctx @ w_in[:, RNN_WIDTH:2 * RNN_WIDTH]
    xr = centred_conv(xr, conv_w, conv_b)
    xr_c = centred_conv(xr_c, conv_w, conv_b)
    rec, rec_c = rg_lru_bidir(xr, xr_c, gate_w, gate_b, lam, with_ctx_out)
    y_lat = jnp.concatenate([jax.nn.gelu(gate) * rec,
                             chunk_gmlp(jax.nn.gelu(u), jax.nn.gelu(v), ws, bs)], axis=-1) @ w_out
    if not with_ctx_out:
        return y_lat, None
    y_ctx = jnp.concatenate([jax.nn.gelu(gate_c) * rec_c,
                             chunk_gmlp(jax.nn.gelu(u_c), jax.nn.gelu(v_c), ws, bs)], axis=-1) @ w_out
    return y_lat, y_ctx


def attend(q, k, v, mask=None, sink=None):
    s = jnp.einsum('bqhgd,bkhd->bhgqk', q, k).astype(jnp.float32) * (HEAD_DIM ** -0.5)
    if mask is not None:
        s = jnp.where(mask, s, NEG_INF)
    if sink is not None:
        col = jnp.broadcast_to(sink.astype(jnp.float32)[None, :, :, None, None], s.shape[:-1] + (1,))
        p = jax.nn.softmax(jnp.concatenate([s, col], axis=-1), axis=-1)[..., :-1]
    else:
        p = jax.nn.softmax(s, axis=-1)
    return jnp.einsum('bhgqk,bkhd->bqhgd', p.astype(v.dtype), v)


def global_attn(q, k_lat, v_lat, k_ctx, v_ctx):
    B, T = q.shape[:2]
    k_all = jnp.concatenate([k_ctx, k_lat], axis=1)
    v_all = jnp.concatenate([v_ctx, v_lat], axis=1)
    qb = jnp.swapaxes(q.reshape((B, T // BLOCK, BLOCK) + q.shape[2:]), 0, 1)
    out = lax.map(lambda qi: attend(qi, k_all, v_all), qb)
    return jnp.swapaxes(out, 0, 1).reshape(B, T, -1)


def window_attn(q, k_lat, v_lat, k_ctx, v_ctx, sink):
    B, T = q.shape[:2]
    nb = T // BLOCK
    L = k_ctx.shape[1]
    pad = ((0, 0), (BLOCK, BLOCK), (0, 0), (0, 0))
    kp = jnp.pad(k_lat, pad)
    vp = jnp.pad(v_lat, pad)
    qb = jnp.swapaxes(q.reshape((B, nb, BLOCK) + q.shape[2:]), 0, 1)
    qoff = jnp.arange(BLOCK)[:, None]
    kidx = jnp.arange(3 * BLOCK)[None, :]
    ctx_mask = jnp.ones((BLOCK, L), dtype=bool)

    def one(args):
        qi, i = args
        start = i * BLOCK
        ki = lax.dynamic_slice_in_dim(kp, start, 3 * BLOCK, axis=1)
        vi = lax.dynamic_slice_in_dim(vp, start, 3 * BLOCK, axis=1)
        kpos = start - BLOCK + kidx
        band = (jnp.abs(kpos - (start + qoff)) <= WINDOW) & (kpos >= 0) & (kpos < T)
        mask = jnp.concatenate([ctx_mask, band], axis=-1)
        return attend(qi, jnp.concatenate([k_ctx, ki], axis=1), jnp.concatenate([v_ctx, vi], axis=1), mask, sink)

    out = lax.map(one, (qb, jnp.arange(nb)))
    return jnp.swapaxes(out, 0, 1).reshape(B, T, -1)


def odd_mixer(h_lat, h_ctx, w_in, w_out, qn_g, kn_g, sink, cos, sin, with_ctx_out):
    B, T, _ = h_lat.shape
    L = h_ctx.shape[1]
    cq, dq, ck, cv, dk, dv = _split(h_lat @ w_in, ODD_SPLITS)
    cq = apply_rope_2d(rms_norm(_heads(cq, C_Q_HEADS), qn_g), cos, sin)
    ck = apply_rope_2d(rms_norm(_heads(ck, C_KV_HEADS), kn_g), cos, sin)
    cv = _heads(cv, C_KV_HEADS)
    dq = apply_rope_2d(_heads(dq, D_Q_HEADS), cos, sin)
    dk = apply_rope_2d(_heads(dk, D_KV_HEADS), cos, sin)
    dv = _heads(dv, D_KV_HEADS)
    ck_c, cv_c, dk_c, dv_c = _split(h_ctx @ w_in[:, Q_COLS:], KV_SPLITS)
    ck_c = rms_norm(_heads(ck_c, C_KV_HEADS), kn_g)
    cv_c = _heads(cv_c, C_KV_HEADS)
    dk_c = _heads(dk_c, D_KV_HEADS)
    dv_c = _heads(dv_c, D_KV_HEADS)
    sink_g = sink.reshape(D_KV_HEADS, D_Q_HEADS // D_KV_HEADS)
    y_c = global_attn(_group(cq, C_KV_HEADS), ck, cv, ck_c, cv_c)
    y_d = window_attn(_group(dq, D_KV_HEADS), dk, dv, dk_c, dv_c, sink_g)
    y_lat = jnp.concatenate([y_c, y_d], axis=-1) @ w_out
    if not with_ctx_out:
        return y_lat, None
    cq_c, dq_c = _split(h_ctx @ w_in[:, :Q_COLS], Q_SPLITS)
    cq_c = rms_norm(_heads(cq_c, C_Q_HEADS), qn_g)
    yc_c = attend(_group(cq_c, C_KV_HEADS), ck_c, cv_c).reshape(B, L, -1)
    yd_c = attend(_group(_heads(dq_c, D_Q_HEADS), D_KV_HEADS), dk_c, dv_c, None, sink_g).reshape(B, L, -1)
    y_ctx = jnp.concatenate([yc_c, yd_c], axis=-1) @ w_out
    return y_lat, y_ctx


def setup_inputs(seed: int = 0) -> dict:
    key = jax.random.key(seed)
    ks = iter(jax.random.split(key, 40))

    def nrm(shape, scale):
        return jax.random.normal(next(ks), shape, jnp.float32) * scale

    n_even = (DEPTH + 1) // 2
    n_odd = DEPTH // 2
    beta = (8.0 * DEPTH) ** -0.25
    D = D_MODEL
    a_target = jax.random.uniform(next(ks), (n_even, 2, RNN_WIDTH), jnp.float32, 0.9, 0.999)
    a_base = a_target ** (1.0 / LRU_C)
    rg_lambda = jnp.log(a_base) - jnp.log1p(-a_base)
    return {
        'x': nrm((BATCH, SEQ, D), 1.0),
        'c': nrm((BATCH, D), 1.0),
        'ctx': nrm((BATCH, CTX_LEN, D), 1.0),
        'c_ctx': nrm((D,), 1.0),
        'ada_w': nrm((DEPTH, D, 6 * D), 0.5 * D ** -0.5),
        'ada_b': nrm((DEPTH, 6 * D), 0.02),
        'ln1_g': 1.0 + nrm((DEPTH, D), 0.02),
        'ln1_b': nrm((DEPTH, D), 0.02),
        'ln2_g': 1.0 + nrm((DEPTH, D), 0.02),
        'ln2_b': nrm((DEPTH, D), 0.02),
        'ffn_w_in': nrm((DEPTH, D, 2 * FFN_HIDDEN), D ** -0.5),
        'ffn_w_out': nrm((DEPTH, FFN_HIDDEN, D), beta * FFN_HIDDEN ** -0.5),
        'ev_w_in': nrm((n_even, D, EVEN_IN), D ** -0.5),
        'ev_w_out': nrm((n_even, EVEN_MIX, D), beta * EVEN_MIX ** -0.5),
        'rg_conv_w': nrm((n_even, CONV_W, RNN_WIDTH), CONV_W ** -0.5),
        'rg_conv_b': nrm((n_even, RNN_WIDTH), 0.02),
        'rg_gate_w': nrm((n_even, 2, 2, RNN_HEADS, RNN_HEAD_DIM, RNN_HEAD_DIM), RNN_HEAD_DIM ** -0.5),
        'rg_gate_b': nrm((n_even, 2, 2, RNN_WIDTH), 0.02),
        'rg_lambda': rg_lambda,
        'cm_w_s': nrm((n_even, CMLP_GROUPS, CHUNK, CHUNK), CHUNK ** -0.5),
        'cm_b_s': 1.0 + nrm((n_even, CMLP_GROUPS, CHUNK), 0.02),
        'od_w_in': nrm((n_odd, D, ODD_IN), D ** -0.5),
        'od_w_out': nrm((n_odd, ODD_MIX, D), beta * ODD_MIX ** -0.5),
        'qn_g': 1.0 + nrm((n_odd, HEAD_DIM), 0.02),
        'kn_g': 1.0 + nrm((n_odd, HEAD_DIM), 0.02),
        'sink': nrm((n_odd, D_Q_HEADS), 0.5),
    }


def reference(x, c, ctx, c_ctx, ada_w, ada_b, ln1_g, ln1_b, ln2_g, ln2_b, ffn_w_in, ffn_w_out,
              ev_w_in, ev_w_out, rg_conv_w, rg_conv_b, rg_gate_w, rg_gate_b, rg_lambda, cm_w_s, cm_b_s,
              od_w_in, od_w_out, qn_g, kn_g, sink):
    alpha = (2.0 * DEPTH) ** 0.25
    T = x.shape[1]
    cos, sin = rope_tables(T, x.dtype)
    s_lat = jax.nn.silu(c)
    s_ctx = jax.nn.silu(c_ctx)
    h, hc = x, ctx
    for l in range(DEPTH):
        last = l == DEPTH - 1
        sh1, sc1, g1, sh2, sc2, g2 = jnp.split((s_lat @ ada_w[l] + ada_b[l])[:, None, :], 6, axis=-1)
        csh1, csc1, cg1, csh2, csc2, cg2 = jnp.split(s_ctx @ ada_w[l] + ada_b[l], 6, axis=-1)
        a_lat = modulate(h, sh1, sc1)
        a_ctx = modulate(hc, csh1, csc1)
        j = l // 2
        if l % 2 == 0:
            y, yc = even_mixer(a_lat, a_ctx, ev_w_in[j], ev_w_out[j], rg_conv_w[j], rg_conv_b[j],
                               rg_gate_w[j], rg_gate_b[j], rg_lambda[j], cm_w_s[j], cm_b_s[j], not last)
        else:
            y, yc = odd_mixer(a_lat, a_ctx, od_w_in[j], od_w_out[j], qn_g[j], kn_g[j], sink[j],
                              cos, sin, not last)
        h = layer_norm(alpha * h + g1 * y, ln1_g[l], ln1_b[l])
        h = layer_norm(alpha * h + g2 * swiglu(modulate(h, sh2, sc2), ffn_w_in[l], ffn_w_out[l]),
                       ln2_g[l], ln2_b[l])
        if not last:
            hc = layer_norm(alpha * hc + cg1 * yc, ln1_g[l], ln1_b[l])
            hc = layer_norm(alpha * hc + cg2 * swiglu(modulate(hc, csh2, csc2), ffn_w_in[l], ffn_w_out[l]),
                            ln2_g[l], ln2_b[l])
    return h
```

```python
import functools

import numpy as np
import jax
import jax.numpy as jnp
from jax import lax
from jax.experimental import pallas as pl
from jax.experimental.pallas import tpu as pltpu

F32 = jnp.float32
BF16 = jnp.bfloat16

D_MODEL = 1024
DEPTH = 4
GRID_W = 64
RNN_WIDTH = D_MODEL
RNN_HEADS = RNN_WIDTH // 128
CONV_W = 4
LRU_C = 8.0
CMLP_WIDTH = D_MODEL // 2
CMLP_GROUPS = 4
CHUNK = 128
HEAD_DIM = 64
Q_HEADS = 8
KV_HEADS = 2
GROUP = Q_HEADS // KV_HEADS
WINDOW = 128
ROPE_THETA = 10000.0
NEG_INF = -1e30
FFN_HIDDEN = 2816
FFN_CHUNK = 256
ALPHA = (2.0 * DEPTH) ** 0.25
EPS = 1e-6

LANES = 128
ROW_TILE = 256
SCAN_BLOCK = 128
ATT_TQ = 256
ATT_TK = 256
VMEM_LIMIT = 56 * 1024 * 1024


def _cparams(sem):
    return pltpu.CompilerParams(dimension_semantics=sem, vmem_limit_bytes=VMEM_LIMIT)


def _const_spec(shape):
    nd = len(shape)
    return pl.BlockSpec(shape, lambda *_: (0,) * nd, pipeline_mode=pl.Buffered(1))


def _gelu(x):
    return 0.5 * x * (1.0 + jnp.tanh(0.7978845608028654 * (x + 0.044715 * (x * x * x))))


def _normalise(x):
    mu = jnp.mean(x, axis=-1, keepdims=True)
    xc = x - mu
    var = jnp.mean(xc * xc, axis=-1, keepdims=True)
    return xc * lax.rsqrt(var + EPS)


def _mod_rows(m_ref, is_ctx, lat_row):
    return jnp.where(is_ctx, m_ref[lat_row + 8:lat_row + 9, :], m_ref[lat_row:lat_row + 1, :])


def _mod_kernel(c_ref, w_ref, b_ref, o_ref):
    c = c_ref[...]
    s = c * jax.nn.sigmoid(c)
    o_ref[...] = jnp.dot(s, w_ref[...], preferred_element_type=F32,
                         precision=lax.Precision.HIGHEST) + b_ref[...]


def _ada_vectors(c_rows, ada_w, ada_b):
    depth, d, n = ada_w.shape
    rows = c_rows.shape[0]
    nb = 1536
    return pl.pallas_call(
        _mod_kernel,
        grid=(depth, n // nb),
        in_specs=[pl.BlockSpec((rows, d), lambda l, j: (0, 0)),
                  pl.BlockSpec((None, d, nb), lambda l, j: (l, 0, j)),
                  pl.BlockSpec((None, 1, nb), lambda l, j: (l, 0, j))],
        out_specs=pl.BlockSpec((None, rows, nb), lambda l, j: (l, 0, j)),
        out_shape=jax.ShapeDtypeStruct((depth, rows, n), F32),
        compiler_params=_cparams(("parallel", "parallel")),
        name="ada_vectors",
    )(c_rows, ada_w, ada_b.reshape(depth, 1, n))


def _in_even_kernel(h_ref, m_ref, w_ref, gg_ref, xr_ref, gu_ref, vn_ref, *, ctx_len):
    tm = h_ref.shape[0]
    is_ctx = pl.program_id(1) * tm < ctx_len
    sh = _mod_rows(m_ref, is_ctx, 0)
    sc = _mod_rows(m_ref, is_ctx, 1)
    a = (h_ref[...] * (1.0 + sc) + sh).astype(BF16)
    w = RNN_WIDTH
    gate = jnp.dot(a, w_ref[:, 0:w], preferred_element_type=F32)
    gg_ref[...] = _gelu(gate).astype(BF16)
    xr_ref[...] = jnp.dot(a, w_ref[:, w:2 * w], preferred_element_type=F32)
    u = jnp.dot(a, w_ref[:, 2 * w:2 * w + CMLP_WIDTH], preferred_element_type=F32)
    gu_ref[...] = _gelu(u).astype(BF16)
    v = jnp.dot(a, w_ref[:, 2 * w + CMLP_WIDTH:], preferred_element_type=F32)
    vn_ref[...] = _normalise(_gelu(v)).astype(BF16)


def _in_even(h, modv, w_in, ctx_len):
    b, s, d = h.shape
    tm = ROW_TILE
    n_in = w_in.shape[1]
    row = lambda width: pl.BlockSpec((None, tm, width), lambda bi, i: (bi, i, 0))
    return pl.pallas_call(
        functools.partial(_in_even_kernel, ctx_len=ctx_len),
        grid=(b, s // tm),
        in_specs=[row(d),
                  pl.BlockSpec((None, 16, d), lambda bi, i: (bi, 0, 0)),
                  _const_spec((d, n_in))],
        out_specs=[row(RNN_WIDTH), row(RNN_WIDTH), row(CMLP_WIDTH), row(CMLP_WIDTH)],
        out_shape=[jax.ShapeDtypeStruct((b, s, RNN_WIDTH), BF16),
                   jax.ShapeDtypeStruct((b, s, RNN_WIDTH), F32),
                   jax.ShapeDtypeStruct((b, s, CMLP_WIDTH), BF16),
                   jax.ShapeDtypeStruct((b, s, CMLP_WIDTH), BF16)],
        compiler_params=_cparams(("parallel", "parallel")),
        name="even_in_proj",
    )(h, modv, w_in)


def _block_scan(a, b, reverse):
    tb = a.shape[0]
    row = lax.broadcasted_iota(jnp.int32, a.shape, 0)
    d = 1
    while d < tb:
        if reverse:
            a_s = pltpu.roll(a, tb - d, 0)
            b_s = pltpu.roll(b, tb - d, 0)
            ok = row < tb - d
        else:
            a_s = pltpu.roll(a, d, 0)
            b_s = pltpu.roll(b, d, 0)
            ok = row >= d
        b = jnp.where(ok, b + a * b_s, b)
        a = jnp.where(ok, a * a_s, a)
        d *= 2
    return a, b


def _scan_kernel(xr_ref, gg_ref, cw_ref, cb_ref, gw_ref, gb_ref, lam_ref, out_ref, rf_ref, rr_ref,
                 *, ctx_len, tb):
    s = xr_ref.shape[0]
    nblk = s // tb
    ncb = ctx_len // tb
    cw = cw_ref[...]
    cb = cb_ref[...]
    row = lax.broadcasted_iota(jnp.int32, (tb, LANES), 0)

    def coeffs(blk, d):
        t0 = pl.multiple_of(blk * tb, tb)
        x = xr_ref[pl.ds(t0, tb), :]
        prev = xr_ref[pl.ds(pl.multiple_of(jnp.maximum(t0 - 8, 0), 8), 8), :]
        nxt = xr_ref[pl.ds(pl.multiple_of(jnp.minimum(t0 + tb, s - 8), 8), 8), :]
        pf = jnp.where((blk == 0) | (blk == ncb), 0.0, 1.0)
        nf = jnp.where((blk == ncb - 1) | (blk == nblk - 1), 0.0, 1.0)
        p6 = prev[6:7, :] * pf
        p7 = prev[7:8, :] * pf
        n0 = nxt[0:1, :] * nf
        xm1 = jnp.where(row == 0, p7, pltpu.roll(x, 1, 0))
        xm2 = jnp.where(row == 0, p6, jnp.where(row == 1, p7, pltpu.roll(x, 2, 0)))
        xp1 = jnp.where(row == tb - 1, n0, pltpu.roll(x, tb - 1, 0))
        xc = xm2 * cw[0:1, :] + xm1 * cw[1:2, :] + x * cw[2:3, :] + xp1 * cw[3:4, :] + cb
        g = jnp.dot(xc.astype(BF16), gw_ref[d], preferred_element_type=F32) + gb_ref[d]
        r = jax.nn.sigmoid(g[:, :LANES])
        ig = jax.nn.sigmoid(g[:, LANES:])
        z = -lam_ref[d:d + 1, :]
        softplus = jnp.maximum(z, 0.0) + jnp.log(1.0 + jnp.exp(-jnp.abs(z)))
        log_a = (-LRU_C) * r * softplus
        a = jnp.exp(log_a)
        bb = jnp.sqrt(1.0 - jnp.exp(2.0 * log_a)) * (ig * xc)
        return t0, a, bb

    def step(j, carry):
        cf, cr = carry
        t0, a, bb = coeffs(j, 0)
        acum, hloc = _block_scan(a, bb, False)
        hf = hloc + acum * cf
        rf_ref[pl.ds(t0, tb), :] = hf
        cf = hf[tb - 1:tb, :]
        blk = jnp.where(j < ncb, ncb - 1 - j, nblk - 1 - (j - ncb))
        t0, a, bb = coeffs(blk, 1)
        acum, hloc = _block_scan(a, bb, True)
        hr = hloc + acum * cr
        rr_ref[pl.ds(t0, tb), :] = hr
        cr = hr[0:1, :]
        return cf, cr

    zero = jnp.zeros((1, LANES), F32)
    lax.fori_loop(0, nblk, step, (zero, zero))

    def combine(j, _):
        t0 = pl.multiple_of(j * tb, tb)
        rec = rf_ref[pl.ds(t0, tb), :] + rr_ref[pl.ds(t0, tb), :]
        out_ref[pl.ds(t0, tb), :] = (gg_ref[pl.ds(t0, tb), :].astype(F32) * rec).astype(BF16)
        return 0

    lax.fori_loop(0, nblk, combine, 0)


def _rglru(xr, gg, conv_w, conv_b, gate_w, gate_b, lam, ctx_len):
    b, s, w = xr.shape
    nh = w // LANES
    col = lambda dt: pl.BlockSpec((None, s, LANES), lambda bi, hd: (bi, 0, hd))
    return pl.pallas_call(
        functools.partial(_scan_kernel, ctx_len=ctx_len, tb=SCAN_BLOCK),
        grid=(b, nh),
        in_specs=[col(F32), col(BF16),
                  pl.BlockSpec((CONV_W, LANES), lambda bi, hd: (0, hd)),
                  pl.BlockSpec((1, LANES), lambda bi, hd: (0, hd)),
                  pl.BlockSpec((None, 2, LANES, 2 * LANES), lambda bi, hd: (hd, 0, 0, 0)),
                  pl.BlockSpec((None, 2, 1, 2 * LANES), lambda bi, hd: (hd, 0, 0, 0)),
                  pl.BlockSpec((2, LANES), lambda bi, hd: (0, hd))],
        out_specs=col(BF16),
        out_shape=jax.ShapeDtypeStruct((b, s, w), BF16),
        scratch_shapes=[pltpu.VMEM((s, LANES), F32), pltpu.VMEM((s, LANES), F32)],
        compiler_params=_cparams(("parallel", "parallel")),
        name="rglru_scan",
    )(xr, gg, conv_w, conv_b, gate_w, gate_b, lam)


def _residual_ln(h, y, gate, g, b):
    return _normalise(ALPHA * h + gate * y) * g + b


def _post_even_kernel(h_ref, m_ref, mr_ref, gu_ref, vn_ref, ws_ref, bs_ref, wo_ref, lg_ref, lb_ref,
                      o_ref, gm_ref, *, ctx_len):
    tm = h_ref.shape[0]
    is_ctx = pl.program_id(1) * tm < ctx_len
    gw = CMLP_WIDTH // CMLP_GROUPS
    for c in range(tm // CHUNK):
        rows = slice(c * CHUNK, (c + 1) * CHUNK)
        for g in range(CMLP_GROUPS):
            cols = slice(g * gw, (g + 1) * gw)
            mixed = jnp.dot(ws_ref[g], vn_ref[rows, cols], preferred_element_type=F32) + bs_ref[g]
            gm_ref[rows, cols] = (gu_ref[rows, cols].astype(F32) * mixed).astype(BF16)
    y = jnp.dot(mr_ref[...], wo_ref[0:RNN_WIDTH, :], preferred_element_type=F32)
    y = y + jnp.dot(gm_ref[...], wo_ref[RNN_WIDTH:, :], preferred_element_type=F32)
    o_ref[...] = _residual_ln(h_ref[...], y, _mod_rows(m_ref, is_ctx, 2), lg_ref[...], lb_ref[...])


def _post_even(h, modv, mr, gu, vn, ws, bsb, w_out, ln_g, ln_b, ctx_len):
    b, s, d = h.shape
    tm = ROW_TILE
    row = lambda width: pl.BlockSpec((None, tm, width), lambda bi, i: (bi, i, 0))
    return pl.pallas_call(
        functools.partial(_post_even_kernel, ctx_len=ctx_len),
        grid=(b, s // tm),
        in_specs=[row(d),
                  pl.BlockSpec((None, 16, d), lambda bi, i: (bi, 0, 0)),
                  row(RNN_WIDTH), row(CMLP_WIDTH), row(CMLP_WIDTH),
                  _const_spec(ws.shape), _const_spec(bsb.shape), _const_spec(w_out.shape),
                  _const_spec((1, d)), _const_spec((1, d))],
        out_specs=row(d),
        out_shape=jax.ShapeDtypeStruct((b, s, d), F32),
        scratch_shapes=[pltpu.VMEM((tm, CMLP_WIDTH), BF16)],
        compiler_params=_cparams(("parallel", "parallel")),
        name="even_out_proj",
    )(h, modv, mr, gu, vn, ws, bsb, w_out, ln_g, ln_b)


def _post_odd_kernel(h_ref, m_ref, yc_ref, yd_ref, wo_ref, lg_ref, lb_ref, o_ref, *, ctx_len):
    tm = h_ref.shape[0]
    is_ctx = pl.program_id(1) * tm < ctx_len
    half = yc_ref.shape[1]
    y = jnp.dot(yc_ref[...], wo_ref[0:half, :], preferred_element_type=F32)
    y = y + jnp.dot(yd_ref[...], wo_ref[half:, :], preferred_element_type=F32)
    o_ref[...] = _residual_ln(h_ref[...], y, _mod_rows(m_ref, is_ctx, 2), lg_ref[...], lb_ref[...])


def _post_odd(h, modv, yc, yd, w_out, ln_g, ln_b, ctx_len):
    b, s, d = h.shape
    tm = ROW_TILE
    row = lambda width: pl.BlockSpec((None, tm, width), lambda bi, i: (bi, i, 0))
    return pl.pallas_call(
        functools.partial(_post_odd_kernel, ctx_len=ctx_len),
        grid=(b, s // tm),
        in_specs=[row(d),
                  pl.BlockSpec((None, 16, d), lambda bi, i: (bi, 0, 0)),
                  row(yc.shape[2]), row(yd.shape[2]),
                  _const_spec(w_out.shape), _const_spec((1, d)), _const_spec((1, d))],
        out_specs=row(d),
        out_shape=jax.ShapeDtypeStruct((b, s, d), F32),
        compiler_params=_cparams(("parallel", "parallel")),
        name="odd_out_proj",
    )(h, modv, yc, yd, w_out, ln_g, ln_b)


def _ffn_kernel(h_ref, m_ref, wg_ref, wu_ref, wo_ref, lg_ref, lb_ref, o_ref, acc_ref, *, ctx_len):
    tm = h_ref.shape[0]
    is_ctx = pl.program_id(1) * tm < ctx_len
    h = h_ref[...]
    a = (h * (1.0 + _mod_rows(m_ref, is_ctx, 4)) + _mod_rows(m_ref, is_ctx, 3)).astype(BF16)
    acc_ref[...] = jnp.zeros_like(acc_ref)

    def chunk(j, _):
        zg = jnp.dot(a, wg_ref[j], preferred_element_type=F32)
        zu = jnp.dot(a, wu_ref[j], preferred_element_type=F32)
        hm = (zg * jax.nn.sigmoid(zg) * zu).astype(BF16)
        acc_ref[...] += jnp.dot(hm, wo_ref[j], preferred_element_type=F32)
        return 0

    lax.fori_loop(0, wg_ref.shape[0], chunk, 0)
    o_ref[...] = _residual_ln(h, acc_ref[...], _mod_rows(m_ref, is_ctx, 5), lg_ref[...], lb_ref[...])


def _ffn(h, modv, wg, wu, wo, ln_g, ln_b, ctx_len):
    b, s, d = h.shape
    tm = ROW_TILE
    row = pl.BlockSpec((None, tm, d), lambda bi, i: (bi, i, 0))
    return pl.pallas_call(
        functools.partial(_ffn_kernel, ctx_len=ctx_len),
        grid=(b, s // tm),
        in_specs=[row,
                  pl.BlockSpec((None, 16, d), lambda bi, i: (bi, 0, 0)),
                  _const_spec(wg.shape), _const_spec(wu.shape), _const_spec(wo.shape),
                  _const_spec((1, d)), _const_spec((1, d))],
        out_specs=row,
        out_shape=jax.ShapeDtypeStruct((b, s, d), F32),
        scratch_shapes=[pltpu.VMEM((tm, d), F32)],
        compiler_params=_cparams(("parallel", "parallel")),
        name="swiglu_ffn",
    )(h, modv, wg, wu, wo, ln_g, ln_b)


def _in_odd_kernel(h_ref, m_ref, w_ref, bd_ref, gq_ref, gk_ref, cos_ref, sin_ref,
                   qc_ref, qd_ref, kc_ref, vc_ref, kd_ref, vd_ref, *, ctx_len):
    tm = h_ref.shape[0]
    is_ctx = pl.program_id(1) * tm < ctx_len
    sh = _mod_rows(m_ref, is_ctx, 0)
    sc = _mod_rows(m_ref, is_ctx, 1)
    a = (h_ref[...] * (1.0 + sc) + sh).astype(BF16)
    cos = cos_ref[...]
    sin = sin_ref[...]
    lane = lax.broadcasted_iota(jnp.int32, (tm, LANES), 1)
    first = (lane & 31) < 16
    scale = HEAD_DIM ** -0.5

    def rope(x):
        partner = jnp.where(first, pltpu.roll(x, LANES - 16, 1), pltpu.roll(x, 16, 1))
        return x * cos + partner * sin

    def rms(x, gain, width):
        ms = jnp.dot((x * x).astype(BF16), bd_ref[0:width, 0:width], preferred_element_type=F32)
        return x * lax.rsqrt(ms + EPS) * gain

    nq = Q_HEADS * HEAD_DIM
    cq = rms(jnp.dot(a, w_ref[:, 0:nq], preferred_element_type=F32), gq_ref[...], nq)
    for m in range(nq // LANES):
        qc_ref[:, m * LANES:(m + 1) * LANES] = (rope(cq[:, m * LANES:(m + 1) * LANES]) * scale).astype(BF16)
    dq = jnp.dot(a, w_ref[:, nq:2 * nq], preferred_element_type=F32)
    for m in range(nq // LANES):
        qd_ref[:, m * LANES:(m + 1) * LANES] = (rope(dq[:, m * LANES:(m + 1) * LANES]) * scale).astype(BF16)
    kv = jnp.dot(a, w_ref[:, 2 * nq:], preferred_element_type=F32)
    kc_ref[...] = rope(rms(kv[:, 0:LANES], gk_ref[...], LANES)).astype(BF16)
    vc_ref[...] = kv[:, LANES:2 * LANES].astype(BF16)
    kd_ref[...] = rope(kv[:, 2 * LANES:3 * LANES]).astype(BF16)
    vd_ref[...] = kv[:, 3 * LANES:4 * LANES].astype(BF16)


def _in_odd(h, modv, w_in, bd, gq, gk, cos_t, sin_t, ctx_len):
    b, s, d = h.shape
    tm = ROW_TILE
    nq = Q_HEADS * HEAD_DIM
    row = lambda width: pl.BlockSpec((None, tm, width), lambda bi, i: (bi, i, 0))
    tab = pl.BlockSpec((tm, LANES), lambda bi, i: (i, 0))
    sds = lambda width: jax.ShapeDtypeStruct((b, s, width), BF16)
    return pl.pallas_call(
        functools.partial(_in_odd_kernel, ctx_len=ctx_len),
        grid=(b, s // tm),
        in_specs=[row(d),
                  pl.BlockSpec((None, 16, d), lambda bi, i: (bi, 0, 0)),
                  _const_spec(w_in.shape), _const_spec(bd.shape),
                  _const_spec(gq.shape), _const_spec(gk.shape), tab, tab],
        out_specs=[row(nq), row(nq), row(LANES), row(LANES), row(LANES), row(LANES)],
        out_shape=[sds(nq), sds(nq), sds(LANES), sds(LANES), sds(LANES), sds(LANES)],
        compiler_params=_cparams(("parallel", "parallel")),
        name="odd_in_proj",
    )(h, modv, w_in, bd, gq, gk, cos_t, sin_t)


def _stack_queries(q_ref, qs_ref, j, tq):
    lane = lax.broadcasted_iota(jnp.int32, (tq, LANES), 1)
    mine = (lane >= j * HEAD_DIM) & (lane < (j + 1) * HEAD_DIM)
    for hh in range(GROUP):
        x = q_ref[:, (hh // 2) * LANES:(hh // 2 + 1) * LANES].astype(F32)
        y = jnp.where(j == hh % 2, x, pltpu.roll(x, HEAD_DIM, 1))
        qs_ref[hh * tq:(hh + 1) * tq, :] = jnp.where(mine, y, 0.0).astype(BF16)


def _unstack_heads(o, o_ref, j, tq):
    lane = lax.broadcasted_iota(jnp.int32, (tq, LANES), 1)
    for m in range(GROUP // 2):
        lo = o[(2 * m) * tq:(2 * m + 1) * tq, :]
        hi = o[(2 * m + 1) * tq:(2 * m + 2) * tq, :]
        lo = jnp.where(j == 0, lo, pltpu.roll(lo, HEAD_DIM, 1))
        hi = jnp.where(j == 0, pltpu.roll(hi, HEAD_DIM, 1), hi)
        o_ref[:, m * LANES:(m + 1) * LANES] = jnp.where(lane < HEAD_DIM, lo, hi).astype(o_ref.dtype)


def _lane_tile(x, n):
    return x if n == 1 else jnp.concatenate([x] * n, axis=1)


def _gattn_kernel(q_ref, k_ref, v_ref, o_ref, qs_ref, m_ref, l_ref, acc_ref, *, ctx_len, tk):
    tq = q_ref.shape[0]
    s = k_ref.shape[0]
    qi = pl.program_id(1)
    j = pl.program_id(2)
    _stack_queries(q_ref, qs_ref, j, tq)
    m_ref[...] = jnp.full(m_ref.shape, NEG_INF, F32)
    l_ref[...] = jnp.zeros(l_ref.shape, F32)
    acc_ref[...] = jnp.zeros(acc_ref.shape, F32)
    nkv = jnp.where(qi * tq < ctx_len, ctx_len // tk, s // tk)

    def body(t, _):
        start = pl.multiple_of(t * tk, tk)
        k = k_ref[pl.ds(start, tk), :]
        v = v_ref[pl.ds(start, tk), :]
        sc = lax.dot_general(qs_ref[...], k, (((1,), (1,)), ((), ())), preferred_element_type=F32)
        m_prev = m_ref[...]
        m_new = jnp.maximum(m_prev, jnp.max(sc, axis=1, keepdims=True))
        alpha = jnp.exp(m_prev - m_new)
        p = jnp.exp(sc - _lane_tile(m_new, tk // LANES))
        l_ref[...] = alpha * l_ref[...] + jnp.sum(p, axis=1, keepdims=True)
        acc_ref[...] = alpha * acc_ref[...] + jnp.dot(p.astype(BF16), v, preferred_element_type=F32)
        m_ref[...] = m_new
        return 0

    lax.fori_loop(0, nkv, body, 0)
    _unstack_heads(acc_ref[...] / l_ref[...], o_ref, j, tq)


def _global_attention(q, k, v, ctx_len):
    b, s, nq = q.shape
    tq = ATT_TQ
    gw = GROUP * HEAD_DIM
    qspec = pl.BlockSpec((None, tq, gw), lambda bi, i, j: (bi, i, j))
    kvspec = pl.BlockSpec((None, s, LANES), lambda bi, i, j: (bi, 0, 0))
    return pl.pallas_call(
        functools.partial(_gattn_kernel, ctx_len=ctx_len, tk=ATT_TK),
        grid=(b, s // tq, KV_HEADS),
        in_specs=[qspec, kvspec, kvspec],
        out_specs=qspec,
        out_shape=jax.ShapeDtypeStruct((b, s, nq), BF16),
        scratch_shapes=[pltpu.VMEM((GROUP * tq, LANES), BF16),
                        pltpu.VMEM((GROUP * tq, LANES), F32),
                        pltpu.VMEM((GROUP * tq, LANES), F32),
                        pltpu.VMEM((GROUP * tq, LANES), F32)],
        compiler_params=_cparams(("parallel", "parallel", "arbitrary")),
        name="global_attention",
    )(q, k, v)


def _wattn_kernel(q_ref, k_ref, v_ref, sink_ref, o_ref, qs_ref, *, ctx_len, span):
    tq = q_ref.shape[0]
    s = k_ref.shape[0]
    qi = pl.program_id(1)
    j = pl.program_id(2)
    _stack_queries(q_ref, qs_ref, j, tq)
    qs = qs_ref[...]
    sink = sink_ref[...]
    nt = (((1,), (1,)), ((), ()))
    kc = k_ref[0:ctx_len, :]
    vc = v_ref[0:ctx_len, :]
    s_ctx = lax.dot_general(qs, kc, nt, preferred_element_type=F32)
    is_ctx = qi * tq < ctx_len

    @pl.when(is_ctx)
    def _():
        m = jnp.maximum(jnp.max(s_ctx, axis=1, keepdims=True), sink)
        p = jnp.exp(s_ctx - _lane_tile(m, ctx_len // LANES))
        l = jnp.sum(p, axis=1, keepdims=True) + jnp.exp(sink - m)
        o = jnp.dot(p.astype(BF16), vc, preferred_element_type=F32)
        _unstack_heads(o / l, o_ref, j, tq)

    @pl.when(jnp.logical_not(is_ctx))
    def _():
        q0 = qi * tq
        start = pl.multiple_of(jnp.clip(q0 - WINDOW, ctx_len, s - span), LANES)
        kw = k_ref[pl.ds(start, span), :]
        vw = v_ref[pl.ds(start, span), :]
        s_win = lax.dot_general(qs, kw, nt, preferred_element_type=F32)
        qpos = q0 + (lax.broadcasted_iota(jnp.int32, (GROUP * tq, span), 0) & (tq - 1))
        kpos = start + lax.broadcasted_iota(jnp.int32, (GROUP * tq, span), 1)
        s_win = jnp.where(jnp.abs(kpos - qpos) <= WINDOW, s_win, NEG_INF)
        m = jnp.maximum(jnp.maximum(jnp.max(s_ctx, axis=1, keepdims=True),
                                    jnp.max(s_win, axis=1, keepdims=True)), sink)
        p_ctx = jnp.exp(s_ctx - _lane_tile(m, ctx_len // LANES))
        p_win = jnp.exp(s_win - _lane_tile(m, span // LANES))
        l = (jnp.sum(p_ctx, axis=1, keepdims=True) + jnp.sum(p_win, axis=1, keepdims=True)
             + jnp.exp(sink - m))
        o = jnp.dot(p_ctx.astype(BF16), vc, preferred_element_type=F32)
        o = o + jnp.dot(p_win.astype(BF16), vw, preferred_element_type=F32)
        _unstack_heads(o / l, o_ref, j, tq)


def _window_attention(q, k, v, sink_rows, ctx_len):
    b, s, nq = q.shape
    tq = ATT_TQ
    gw = GROUP * HEAD_DIM
    span = tq + 2 * WINDOW
    qspec = pl.BlockSpec((None, tq, gw), lambda bi, i, j: (bi, i, j))
    kvspec = pl.BlockSpec((None, s, LANES), lambda bi, i, j: (bi, 0, 0))
    return pl.pallas_call(
        functools.partial(_wattn_kernel, ctx_len=ctx_len, span=span),
        grid=(b, s // tq, KV_HEADS),
        in_specs=[qspec, kvspec, kvspec,
                  pl.BlockSpec((None, GROUP * tq, LANES), lambda bi, i, j: (j, 0, 0))],
        out_specs=qspec,
        out_shape=jax.ShapeDtypeStruct((b, s, nq), BF16),
        scratch_shapes=[pltpu.VMEM((GROUP * tq, LANES), BF16)],
        compiler_params=_cparams(("parallel", "parallel", "arbitrary")),
        name="window_attention",
    )(q, k, v, sink_rows)


def _rope_tables(n_lat, ctx_len):
    rows = n_lat // GRID_W
    row = jnp.repeat(jnp.arange(rows, dtype=F32), GRID_W)
    col = jnp.tile(jnp.arange(GRID_W, dtype=F32), rows)
    nf = HEAD_DIM // 4
    inv_freq = ROPE_THETA ** (-jnp.arange(nf, dtype=F32) / nf)
    ang_r = row[:, None] * inv_freq
    ang_c = col[:, None] * inv_freq
    cos = jnp.concatenate([jnp.cos(ang_r)] * 2 + [jnp.cos(ang_c)] * 2, axis=1)
    sin = jnp.concatenate([-jnp.sin(ang_r), jnp.sin(ang_r), -jnp.sin(ang_c), jnp.sin(ang_c)], axis=1)
    cos = jnp.concatenate([jnp.ones((ctx_len, HEAD_DIM), F32), cos], axis=0)
    sin = jnp.concatenate([jnp.zeros((ctx_len, HEAD_DIM), F32), sin], axis=0)
    return jnp.tile(cos, (1, LANES // HEAD_DIM)), jnp.tile(sin, (1, LANES // HEAD_DIM))


def kernel(x, c, ctx, c_ctx, ada_w, ada_b, ln1_g, ln1_b, ln2_g, ln2_b, ffn_w_in, ffn_w_out,
           ev_w_in, ev_w_out, rg_conv_w, rg_conv_b, rg_gate_w, rg_gate_b, rg_lambda, cm_w_s, cm_b_s,
           od_w_in, od_w_out, qn_g, kn_g, sink):
    b, t, d = x.shape
    ctx_len = ctx.shape[1]
    s = ctx_len + t
    depth = ada_w.shape[0]
    assert d == D_MODEL and depth == DEPTH and b + 1 <= 16
    assert ctx_len % ROW_TILE == 0 and t % ROW_TILE == 0 and t % GRID_W == 0

    c_rows = jnp.zeros((16, d), F32).at[:b].set(c).at[b].set(c_ctx)
    mods = _ada_vectors(c_rows, ada_w, ada_b).reshape(depth, 16, 6, d)
    modv = jnp.zeros((depth, b, 16, d), F32)
    modv = modv.at[:, :, 0:6].set(mods[:, :b])
    modv = modv.at[:, :, 8:14].set(jnp.broadcast_to(mods[:, b][:, None], (depth, b, 6, d)))

    cos_t, sin_t = _rope_tables(t, ctx_len)
    nq = Q_HEADS * HEAD_DIM
    bd = jnp.kron(jnp.eye(nq // HEAD_DIM, dtype=F32), jnp.full((HEAD_DIM, HEAD_DIM), 1.0 / HEAD_DIM, F32)).astype(BF16)
    nchunk = FFN_HIDDEN // FFN_CHUNK

    h = jnp.concatenate([ctx, x], axis=1)
    for l in range(depth):
        j = l // 2
        mv = modv[l]
        if l % 2 == 0:
            gg, xr, gu, vn = _in_even(h, mv, ev_w_in[j].astype(BF16), ctx_len)
            gw = jnp.transpose(rg_gate_w[j], (2, 0, 3, 1, 4)).reshape(RNN_HEADS, 2, LANES, 2 * LANES).astype(BF16)
            gb = jnp.transpose(rg_gate_b[j].reshape(2, 2, RNN_HEADS, LANES), (2, 0, 1, 3)).reshape(RNN_HEADS, 2, 1, 2 * LANES)
            mr = _rglru(xr, gg, rg_conv_w[j], rg_conv_b[j].reshape(1, -1), gw, gb, rg_lambda[j], ctx_len)
            bsb = jnp.broadcast_to(cm_b_s[j][:, :, None], (CMLP_GROUPS, CHUNK, CMLP_WIDTH // CMLP_GROUPS))
            h = _post_even(h, mv, mr, gu, vn, cm_w_s[j].astype(BF16), bsb, ev_w_out[j].astype(BF16),
                           ln1_g[l].reshape(1, d), ln1_b[l].reshape(1, d), ctx_len)
        else:
            gq = jnp.tile(qn_g[j], Q_HEADS).reshape(1, nq)
            gk = jnp.tile(kn_g[j], LANES // HEAD_DIM).reshape(1, LANES)
            qc, qd, kc, vc, kd, vd = _in_odd(h, mv, od_w_in[j].astype(BF16), bd, gq, gk, cos_t, sin_t, ctx_len)
            yc = _global_attention(qc, kc, vc, ctx_len)
            sink_rows = jnp.broadcast_to(sink[j].reshape(KV_HEADS, GROUP, 1, 1), (KV_HEADS, GROUP, ATT_TQ, LANES))
            yd = _window_attention(qd, kd, vd, sink_rows.reshape(KV_HEADS, GROUP * ATT_TQ, LANES), ctx_len)
            h = _post_odd(h, mv, yc, yd, od_w_out[j].astype(BF16),
                          ln1_g[l].reshape(1, d), ln1_b[l].reshape(1, d), ctx_len)
        wg = jnp.transpose(ffn_w_in[l][:, :FFN_HIDDEN].reshape(d, nchunk, FFN_CHUNK), (1, 0, 2)).astype(BF16)
        wu = jnp.transpose(ffn_w_in[l][:, FFN_HIDDEN:].reshape(d, nchunk, FFN_CHUNK), (1, 0, 2)).astype(BF16)
        wo = ffn_w_out[l].reshape(nchunk, FFN_CHUNK, d).astype(BF16)
        h = _ffn(h, mv, wg, wu, wo, ln2_g[l].reshape(1, d), ln2_b[l].reshape(1, d), ctx_len)
    return h[:, ctx_len:, :]
```

```python
import functools

import numpy as np
import jax
import jax.numpy as jnp
from jax import lax
from jax.experimental import pallas as pl
from jax.experimental.pallas import tpu as pltpu

F32 = jnp.float32
BF16 = jnp.bfloat16

D_MODEL = 1024
DEPTH = 4
GRID_W = 64
RNN_WIDTH = D_MODEL
RNN_HEADS = RNN_WIDTH // 128
CONV_W = 4
LRU_C = 8.0
CMLP_WIDTH = D_MODEL // 2
CMLP_GROUPS = 4
CHUNK = 128
HEAD_DIM = 64
Q_HEADS = 8
KV_HEADS = 2
GROUP = Q_HEADS // KV_HEADS
WINDOW = 128
ROPE_THETA = 10000.0
NEG_INF = -1e30
FFN_HIDDEN = 2816
FFN_CHUNK = 256
ALPHA = (2.0 * DEPTH) ** 0.25
LOG2E = 1.4426950408889634
EPS = 1e-6

LANES = 128
ROW_TILE = 256
SCAN_BLOCK = 128
ATT_TQ = 256
ATT_TK = 256
VMEM_LIMIT = 56 * 1024 * 1024


def _cparams(sem):
    return pltpu.CompilerParams(dimension_semantics=sem, vmem_limit_bytes=VMEM_LIMIT)


def _const_spec(shape):
    nd = len(shape)
    return pl.BlockSpec(shape, lambda *_: (0,) * nd, pipeline_mode=pl.Buffered(1))


def _gelu(x):
    return 0.5 * x * (1.0 + jnp.tanh(0.7978845608028654 * (x + 0.044715 * (x * x * x))))


def _normalise(x):
    mu = jnp.mean(x, axis=-1, keepdims=True)
    xc = x - mu
    var = jnp.mean(xc * xc, axis=-1, keepdims=True)
    return xc * lax.rsqrt(var + EPS)


def _mod_rows(m_ref, is_ctx, lat_row):
    return jnp.where(is_ctx, m_ref[lat_row + 8:lat_row + 9, :], m_ref[lat_row:lat_row + 1, :])


def _mod_kernel(c_ref, w_ref, b_ref, o_ref):
    c = c_ref[...]
    s = c * jax.nn.sigmoid(c)
    o_ref[...] = jnp.dot(s, w_ref[...], preferred_element_type=F32,
                         precision=lax.Precision.HIGHEST) + b_ref[...]


def _ada_vectors(c_rows, ada_w, ada_b):
    depth, d, n = ada_w.shape
    rows = c_rows.shape[0]
    nb = 1536
    return pl.pallas_call(
        _mod_kernel,
        grid=(depth, n // nb),
        in_specs=[pl.BlockSpec((rows, d), lambda l, j: (0, 0)),
                  pl.BlockSpec((None, d, nb), lambda l, j: (l, 0, j)),
                  pl.BlockSpec((None, 1, nb), lambda l, j: (l, 0, j))],
        out_specs=pl.BlockSpec((None, rows, nb), lambda l, j: (l, 0, j)),
        out_shape=jax.ShapeDtypeStruct((depth, rows, n), F32),
        compiler_params=_cparams(("parallel", "parallel")),
        name="ada_vectors",
    )(c_rows, ada_w, ada_b.reshape(depth, 1, n))


def _in_even_kernel(h_ref, m_ref, w_ref, gg_ref, xr_ref, gu_ref, vn_ref, *, ctx_len):
    tm = h_ref.shape[0]
    is_ctx = pl.program_id(1) * tm < ctx_len
    sh = _mod_rows(m_ref, is_ctx, 0)
    sc = _mod_rows(m_ref, is_ctx, 1)
    a = (h_ref[...] * (1.0 + sc) + sh).astype(BF16)
    w = RNN_WIDTH
    gate = jnp.dot(a, w_ref[:, 0:w], preferred_element_type=F32)
    gg_ref[...] = _gelu(gate).astype(BF16)
    xr_ref[...] = jnp.dot(a, w_ref[:, w:2 * w], preferred_element_type=F32)
    u = jnp.dot(a, w_ref[:, 2 * w:2 * w + CMLP_WIDTH], preferred_element_type=F32)
    gu_ref[...] = _gelu(u).astype(BF16)
    v = jnp.dot(a, w_ref[:, 2 * w + CMLP_WIDTH:], preferred_element_type=F32)
    vn_ref[...] = _normalise(_gelu(v)).astype(BF16)


def _in_even(h, modv, w_in, ctx_len):
    b, s, d = h.shape
    tm = ROW_TILE
    n_in = w_in.shape[1]
    row = lambda width: pl.BlockSpec((None, tm, width), lambda bi, i: (bi, i, 0))
    return pl.pallas_call(
        functools.partial(_in_even_kernel, ctx_len=ctx_len),
        grid=(b, s // tm),
        in_specs=[row(d),
                  pl.BlockSpec((None, 16, d), lambda bi, i: (bi, 0, 0)),
                  _const_spec((d, n_in))],
        out_specs=[row(RNN_WIDTH), row(RNN_WIDTH), row(CMLP_WIDTH), row(CMLP_WIDTH)],
        out_shape=[jax.ShapeDtypeStruct((b, s, RNN_WIDTH), BF16),
                   jax.ShapeDtypeStruct((b, s, RNN_WIDTH), F32),
                   jax.ShapeDtypeStruct((b, s, CMLP_WIDTH), BF16),
                   jax.ShapeDtypeStruct((b, s, CMLP_WIDTH), BF16)],
        compiler_params=_cparams(("parallel", "parallel")),
        name="even_in_proj",
    )(h, modv, w_in)


def _block_scan(a, b, reverse):
    tb = a.shape[0]
    row = lax.broadcasted_iota(jnp.int32, a.shape, 0)
    d = 1
    while d < tb:
        if reverse:
            a_s = pltpu.roll(a, tb - d, 0)
            b_s = pltpu.roll(b, tb - d, 0)
            ok = row < tb - d
        else:
            a_s = pltpu.roll(a, d, 0)
            b_s = pltpu.roll(b, d, 0)
            ok = row >= d
        b = jnp.where(ok, b + a * b_s, b)
        a = jnp.where(ok, a * a_s, a)
        d *= 2
    return a, b


def _scan_kernel(xr_ref, gg_ref, cw_ref, cb_ref, gw_ref, gb_ref, lam_ref, out_ref, rf_ref, rr_ref,
                 *, ctx_len, tb):
    s = xr_ref.shape[0]
    nblk = s // tb
    ncb = ctx_len // tb
    cw = cw_ref[...]
    cb = cb_ref[...]
    row = lax.broadcasted_iota(jnp.int32, (tb, LANES), 0)

    def coeffs(blk, d):
        t0 = pl.multiple_of(blk * tb, tb)
        x = xr_ref[pl.ds(t0, tb), :]
        prev = xr_ref[pl.ds(pl.multiple_of(jnp.maximum(t0 - 8, 0), 8), 8), :]
        nxt = xr_ref[pl.ds(pl.multiple_of(jnp.minimum(t0 + tb, s - 8), 8), 8), :]
        pf = jnp.where((blk == 0) | (blk == ncb), 0.0, 1.0)
        nf = jnp.where((blk == ncb - 1) | (blk == nblk - 1), 0.0, 1.0)
        p6 = prev[6:7, :] * pf
        p7 = prev[7:8, :] * pf
        n0 = nxt[0:1, :] * nf
        xm1 = jnp.where(row == 0, p7, pltpu.roll(x, 1, 0))
        xm2 = jnp.where(row == 0, p6, jnp.where(row == 1, p7, pltpu.roll(x, 2, 0)))
        xp1 = jnp.where(row == tb - 1, n0, pltpu.roll(x, tb - 1, 0))
        xc = xm2 * cw[0:1, :] + xm1 * cw[1:2, :] + x * cw[2:3, :] + xp1 * cw[3:4, :] + cb
        g = jnp.dot(xc.astype(BF16), gw_ref[d], preferred_element_type=F32) + gb_ref[d]
        r = jax.nn.sigmoid(g[:, :LANES])
        ig = jax.nn.sigmoid(g[:, LANES:])
        z = -lam_ref[d:d + 1, :]
        softplus = jnp.maximum(z, 0.0) + jnp.log(1.0 + jnp.exp(-jnp.abs(z)))
        log_a = (-LRU_C) * r * softplus
        a = jnp.exp(log_a)
        bb = jnp.sqrt(1.0 - jnp.exp(2.0 * log_a)) * (ig * xc)
        return t0, a, bb

    def step(j, carry):
        cf, cr = carry
        t0, a, bb = coeffs(j, 0)
        acum, hloc = _block_scan(a, bb, False)
        hf = hloc + acum * cf
        rf_ref[pl.ds(t0, tb), :] = hf
        cf = hf[tb - 1:tb, :]
        blk = jnp.where(j < ncb, ncb - 1 - j, nblk - 1 - (j - ncb))
        t0, a, bb = coeffs(blk, 1)
        acum, hloc = _block_scan(a, bb, True)
        hr = hloc + acum * cr
        rr_ref[pl.ds(t0, tb), :] = hr
        cr = hr[0:1, :]
        return cf, cr

    zero = jnp.zeros((1, LANES), F32)
    lax.fori_loop(0, nblk, step, (zero, zero))

    def combine(j, _):
        t0 = pl.multiple_of(j * tb, tb)
        rec = rf_ref[pl.ds(t0, tb), :] + rr_ref[pl.ds(t0, tb), :]
        out_ref[pl.ds(t0, tb), :] = (gg_ref[pl.ds(t0, tb), :].astype(F32) * rec).astype(BF16)
        return 0

    lax.fori_loop(0, nblk, combine, 0)


def _rglru(xr, gg, conv_w, conv_b, gate_w, gate_b, lam, ctx_len):
    b, s, w = xr.shape
    nh = w // LANES
    col = lambda dt: pl.BlockSpec((None, s, LANES), lambda bi, hd: (bi, 0, hd))
    return pl.pallas_call(
        functools.partial(_scan_kernel, ctx_len=ctx_len, tb=SCAN_BLOCK),
        grid=(b, nh),
        in_specs=[col(F32), col(BF16),
                  pl.BlockSpec((CONV_W, LANES), lambda bi, hd: (0, hd)),
                  pl.BlockSpec((1, LANES), lambda bi, hd: (0, hd)),
                  pl.BlockSpec((None, 2, LANES, 2 * LANES), lambda bi, hd: (hd, 0, 0, 0)),
                  pl.BlockSpec((None, 2, 1, 2 * LANES), lambda bi, hd: (hd, 0, 0, 0)),
                  pl.BlockSpec((2, LANES), lambda bi, hd: (0, hd))],
        out_specs=col(BF16),
        out_shape=jax.ShapeDtypeStruct((b, s, w), BF16),
        scratch_shapes=[pltpu.VMEM((s, LANES), F32), pltpu.VMEM((s, LANES), F32)],
        compiler_params=_cparams(("parallel", "parallel")),
        name="rglru_scan",
    )(xr, gg, conv_w, conv_b, gate_w, gate_b, lam)


def _residual_ln(h, y, gate, g, b):
    return _normalise(ALPHA * h + gate * y) * g + b


def _post_even_kernel(h_ref, m_ref, mr_ref, gu_ref, vn_ref, ws_ref, bs_ref, wo_ref, lg_ref, lb_ref,
                      o_ref, gm_ref, *, ctx_len):
    tm = h_ref.shape[0]
    is_ctx = pl.program_id(1) * tm < ctx_len
    gw = CMLP_WIDTH // CMLP_GROUPS
    for c in range(tm // CHUNK):
        rows = slice(c * CHUNK, (c + 1) * CHUNK)
        for g in range(CMLP_GROUPS):
            cols = slice(g * gw, (g + 1) * gw)
            mixed = jnp.dot(ws_ref[g], vn_ref[rows, cols], preferred_element_type=F32) + bs_ref[g]
            gm_ref[rows, cols] = (gu_ref[rows, cols].astype(F32) * mixed).astype(BF16)
    y = jnp.dot(mr_ref[...], wo_ref[0:RNN_WIDTH, :], preferred_element_type=F32)
    y = y + jnp.dot(gm_ref[...], wo_ref[RNN_WIDTH:, :], preferred_element_type=F32)
    o_ref[...] = _residual_ln(h_ref[...], y, _mod_rows(m_ref, is_ctx, 2), lg_ref[...], lb_ref[...])


def _post_even(h, modv, mr, gu, vn, ws, bsb, w_out, ln_g, ln_b, ctx_len):
    b, s, d = h.shape
    tm = ROW_TILE
    row = lambda width: pl.BlockSpec((None, tm, width), lambda bi, i: (bi, i, 0))
    return pl.pallas_call(
        functools.partial(_post_even_kernel, ctx_len=ctx_len),
        grid=(b, s // tm),
        in_specs=[row(d),
                  pl.BlockSpec((None, 16, d), lambda bi, i: (bi, 0, 0)),
                  row(RNN_WIDTH), row(CMLP_WIDTH), row(CMLP_WIDTH),
                  _const_spec(ws.shape), _const_spec(bsb.shape), _const_spec(w_out.shape),
                  _const_spec((1, d)), _const_spec((1, d))],
        out_specs=row(d),
        out_shape=jax.ShapeDtypeStruct((b, s, d), F32),
        scratch_shapes=[pltpu.VMEM((tm, CMLP_WIDTH), BF16)],
        compiler_params=_cparams(("parallel", "parallel")),
        name="even_out_proj",
    )(h, modv, mr, gu, vn, ws, bsb, w_out, ln_g, ln_b)


def _post_odd_kernel(h_ref, m_ref, yc_ref, yd_ref, wo_ref, lg_ref, lb_ref, o_ref, *, ctx_len):
    tm = h_ref.shape[0]
    is_ctx = pl.program_id(1) * tm < ctx_len
    half = yc_ref.shape[1]
    y = jnp.dot(yc_ref[...], wo_ref[0:half, :], preferred_element_type=F32)
    y = y + jnp.dot(yd_ref[...], wo_ref[half:, :], preferred_element_type=F32)
    o_ref[...] = _residual_ln(h_ref[...], y, _mod_rows(m_ref, is_ctx, 2), lg_ref[...], lb_ref[...])


def _post_odd(h, modv, yc, yd, w_out, ln_g, ln_b, ctx_len):
    b, s, d = h.shape
    tm = ROW_TILE
    row = lambda width: pl.BlockSpec((None, tm, width), lambda bi, i: (bi, i, 0))
    return pl.pallas_call(
        functools.partial(_post_odd_kernel, ctx_len=ctx_len),
        grid=(b, s // tm),
        in_specs=[row(d),
                  pl.BlockSpec((None, 16, d), lambda bi, i: (bi, 0, 0)),
                  row(yc.shape[2]), row(yd.shape[2]),
                  _const_spec(w_out.shape), _const_spec((1, d)), _const_spec((1, d))],
        out_specs=row(d),
        out_shape=jax.ShapeDtypeStruct((b, s, d), F32),
        compiler_params=_cparams(("parallel", "parallel")),
        name="odd_out_proj",
    )(h, modv, yc, yd, w_out, ln_g, ln_b)


def _ffn_kernel(h_ref, m_ref, wg_ref, wu_ref, wo_ref, lg_ref, lb_ref, o_ref, acc_ref, *, ctx_len):
    tm = h_ref.shape[0]
    is_ctx = pl.program_id(1) * tm < ctx_len
    h = h_ref[...]
    a = (h * (1.0 + _mod_rows(m_ref, is_ctx, 4)) + _mod_rows(m_ref, is_ctx, 3)).astype(BF16)
    acc_ref[...] = jnp.zeros_like(acc_ref)

    def chunk(j, _):
        zg = jnp.dot(a, wg_ref[j], preferred_element_type=F32)
        zu = jnp.dot(a, wu_ref[j], preferred_element_type=F32)
        hm = (zg * jax.nn.sigmoid(zg) * zu).astype(BF16)
        acc_ref[...] += jnp.dot(hm, wo_ref[j], preferred_element_type=F32)
        return 0

    lax.fori_loop(0, wg_ref.shape[0], chunk, 0, unroll=True)
    o_ref[...] = _residual_ln(h, acc_ref[...], _mod_rows(m_ref, is_ctx, 5), lg_ref[...], lb_ref[...])


def _ffn(h, modv, wg, wu, wo, ln_g, ln_b, ctx_len):
    b, s, d = h.shape
    tm = ROW_TILE
    row = pl.BlockSpec((None, tm, d), lambda bi, i: (bi, i, 0))
    return pl.pallas_call(
        functools.partial(_ffn_kernel, ctx_len=ctx_len),
        grid=(b, s // tm),
        in_specs=[row,
                  pl.BlockSpec((None, 16, d), lambda bi, i: (bi, 0, 0)),
                  _const_spec(wg.shape), _const_spec(wu.shape), _const_spec(wo.shape),
                  _const_spec((1, d)), _const_spec((1, d))],
        out_specs=row,
        out_shape=jax.ShapeDtypeStruct((b, s, d), F32),
        scratch_shapes=[pltpu.VMEM((tm, d), F32)],
        compiler_params=_cparams(("parallel", "parallel")),
        name="swiglu_ffn",
    )(h, modv, wg, wu, wo, ln_g, ln_b)


def _in_odd_kernel(h_ref, m_ref, w_ref, bd_ref, gq_ref, gk_ref, cos_ref, sin_ref,
                   qc_ref, qd_ref, kc_ref, vc_ref, kd_ref, vd_ref, *, ctx_len):
    tm = h_ref.shape[0]
    is_ctx = pl.program_id(1) * tm < ctx_len
    sh = _mod_rows(m_ref, is_ctx, 0)
    sc = _mod_rows(m_ref, is_ctx, 1)
    a = (h_ref[...] * (1.0 + sc) + sh).astype(BF16)
    cos = cos_ref[...]
    sin = sin_ref[...]
    lane = lax.broadcasted_iota(jnp.int32, (tm, LANES), 1)
    first = (lane & 31) < 16
    scale = HEAD_DIM ** -0.5

    def rope(x):
        partner = jnp.where(first, pltpu.roll(x, LANES - 16, 1), pltpu.roll(x, 16, 1))
        return x * cos + partner * sin

    def rms(x, gain, width):
        ms = jnp.dot((x * x).astype(BF16), bd_ref[0:width, 0:width], preferred_element_type=F32)
        return x * lax.rsqrt(ms + EPS) * gain

    nq = Q_HEADS * HEAD_DIM
    cq = rms(jnp.dot(a, w_ref[:, 0:nq], preferred_element_type=F32), gq_ref[...], nq)
    for m in range(nq // LANES):
        qc_ref[m * LANES:(m + 1) * LANES, :] = (rope(cq[:, m * LANES:(m + 1) * LANES]) * (scale * LOG2E)).T.astype(BF16)
    dq =jnp.dot(a, w_ref[:, nq:2 * nq], preferred_element_type=F32)
    for m in range(nq // LANES):
        qd_ref[:, m * LANES:(m + 1) * LANES] = (rope(dq[:, m * LANES:(m + 1) * LANES]) * scale).astype(BF16)
    kv = jnp.dot(a, w_ref[:, 2 * nq:], preferred_element_type=F32)
    kc_ref[...] = rope(rms(kv[:, 0:LANES], gk_ref[...], LANES)).astype(BF16)
    vc_ref[...] = kv[:, LANES:2 * LANES].T.astype(BF16)
    kd_ref[...] = rope(kv[:, 2 * LANES:3 * LANES]).astype(BF16)
    vd_ref[...] = kv[:, 3 * LANES:4 * LANES].astype(BF16)


def _in_odd(h, modv, w_in, bd, gq, gk, cos_t, sin_t, ctx_len):
    b, s, d = h.shape
    tm = ROW_TILE
    nq = Q_HEADS * HEAD_DIM
    row = lambda width: pl.BlockSpec((None, tm, width), lambda bi, i: (bi, i, 0))
    tab = pl.BlockSpec((tm, LANES), lambda bi, i: (i, 0))
    sds = lambda width: jax.ShapeDtypeStruct((b, s, width), BF16)
    return pl.pallas_call(
        functools.partial(_in_odd_kernel, ctx_len=ctx_len),
        grid=(b, s // tm),
        in_specs=[row(d),
                  pl.BlockSpec((None, 16, d), lambda bi, i: (bi, 0, 0)),
                  _const_spec(w_in.shape), _const_spec(bd.shape),
                  _const_spec(gq.shape), _const_spec(gk.shape), tab, tab],
        out_specs=[pl.BlockSpec((None, nq, tm), lambda bi, i: (bi, 0, i)), row(nq), row(LANES),
                   pl.BlockSpec((None, None, LANES, tm), lambda bi, i: (bi, i, 0, 0)), row(LANES), row(LANES)],
        out_shape=[jax.ShapeDtypeStruct((b, nq, s), BF16), sds(nq), sds(LANES),
                   jax.ShapeDtypeStruct((b, s // tm, LANES, tm), BF16), sds(LANES), sds(LANES)],
        compiler_params=_cparams(("parallel", "parallel")),
        name="odd_in_proj",
    )(h, modv, w_in, bd, gq, gk, cos_t, sin_t)


def _stack_queries(q_ref, qs_ref, j, tq):
    lane = lax.broadcasted_iota(jnp.int32, (tq, LANES), 1)
    mine = (lane >= j * HEAD_DIM) & (lane < (j + 1) * HEAD_DIM)
    for hh in range(GROUP):
        x = q_ref[:, (hh // 2) * LANES:(hh // 2 + 1) * LANES].astype(F32)
        y = jnp.where(j == hh % 2, x, pltpu.roll(x, HEAD_DIM, 1))
        qs_ref[hh * tq:(hh + 1) * tq, :] = jnp.where(mine, y, 0.0).astype(BF16)


def _unstack_heads(o, o_ref, j, tq):
    lane = lax.broadcasted_iota(jnp.int32, (tq, LANES), 1)
    for m in range(GROUP // 2):
        lo = o[(2 * m) * tq:(2 * m + 1) * tq, :]
        hi = o[(2 * m + 1) * tq:(2 * m + 2) * tq, :]
        lo = jnp.where(j == 0, lo, pltpu.roll(lo, HEAD_DIM, 1))
        hi = jnp.where(j == 0, pltpu.roll(hi, HEAD_DIM, 1), hi)
        o_ref[:, m * LANES:(m + 1) * LANES] = jnp.where(lane < HEAD_DIM, lo, hi).astype(o_ref.dtype)


def _lane_tile(x, n):
    return x if n == 1 else jnp.concatenate([x] * n, axis=1)


def _gattn_kernel(q_ref, k_ref, v_ref, o_ref, qs_ref, m_ref, l_ref, acc_ref, sa_ref, sb_ref, *, ctx_len):
    tq = q_ref.shape[1]
    ntile, _, tk = v_ref.shape
    qi = pl.program_id(1)
    j = pl.program_id(2)
    for hh in range(GROUP):
        qh = q_ref[hh * HEAD_DIM:(hh + 1) * HEAD_DIM, :]
        zero = jnp.zeros_like(qh)
        qs_ref[0:HEAD_DIM, hh * tq:(hh + 1) * tq] = jnp.where(j == 0, qh, zero)
        qs_ref[HEAD_DIM:, hh * tq:(hh + 1) * tq] = jnp.where(j == 0, zero, qh)
    m_ref[...] = jnp.full(m_ref.shape, NEG_INF, F32)
    l_ref[...] = jnp.zeros(l_ref.shape, F32)
    acc_ref[...] = jnp.zeros(acc_ref.shape, F32)
    nkv = jnp.where(qi * tq < ctx_len, ctx_len // tk, ntile)

    def scores(t, s_ref):
        start = pl.multiple_of(t * tk, tk)
        s_ref[...] = jnp.dot(k_ref[pl.ds(start, tk), :], qs_ref[...], preferred_element_type=F32)

    def update(t, s_ref):
        sc = s_ref[...]
        m_prev = m_ref[...]
        m_new = jnp.maximum(m_prev, jnp.max(sc, axis=0, keepdims=True))
        alpha = jnp.exp2(m_prev - m_new)
        p = jnp.exp2(sc - m_new)
        l_ref[...] = alpha * l_ref[...] + jnp.sum(p, axis=0, keepdims=True)
        acc_ref[...] = alpha * acc_ref[...] + jnp.dot(v_ref[t], p.astype(BF16), preferred_element_type=F32)
        m_ref[...] = m_new

    scores(0, sa_ref)

    def pair(u, _):
        scores(2 * u + 1, sb_ref)
        update(2 * u, sa_ref)
        scores(2 * u + 2, sa_ref)
        update(2 * u + 1, sb_ref)
        return 0

    lax.fori_loop(0, nkv // 2, pair, 0)
    update(nkv - 1, sa_ref)
    o = acc_ref[...] / l_ref[...]
    o = jnp.where(j == 0, o[0:HEAD_DIM, :], o[HEAD_DIM:, :])
    for m in range(GROUP // 2):
        pair = jnp.concatenate([o[:, (2 * m) * tq:(2 * m + 1) * tq],
                                o[:, (2 * m + 1) * tq:(2 * m + 2) * tq]], axis=0)
        o_ref[:, m * LANES:(m + 1) * LANES] = pair.T.astype(o_ref.dtype)


def _global_attention(qt, k, vt, ctx_len):
    b, nq, s = qt.shape
    tq = ATT_TQ
    gw = GROUP * HEAD_DIM
    assert vt.shape[3] == ATT_TK
    assert (s // ATT_TK) % 2 == 1 and (ctx_len // ATT_TK) % 2 == 1
    return pl.pallas_call(
        functools.partial(_gattn_kernel, ctx_len=ctx_len),
        grid=(b, s // tq, KV_HEADS),
        in_specs=[pl.BlockSpec((None, gw, tq), lambda bi, i, j: (bi, j, i)),
                  pl.BlockSpec((None, s, LANES), lambda bi, i, j: (bi, 0, 0)),
                  pl.BlockSpec((None,) + vt.shape[1:], lambda bi, i, j: (bi, 0, 0, 0))],
        out_specs=pl.BlockSpec((None, tq, gw), lambda bi, i, j: (bi, i, j)),
        out_shape=jax.ShapeDtypeStruct((b, s, nq), BF16),
        scratch_shapes=[pltpu.VMEM((LANES, GROUP * tq), BF16),
                        pltpu.VMEM((1, GROUP * tq), F32),
                        pltpu.VMEM((1, GROUP * tq), F32),
                        pltpu.VMEM((LANES, GROUP * tq), F32),
                        pltpu.VMEM((ATT_TK, GROUP * tq), F32),
                        pltpu.VMEM((ATT_TK, GROUP * tq), F32)],
        compiler_params=_cparams(("parallel", "parallel", "arbitrary")),
        name="global_attention",
    )(qt, k, vt)


def _wattn_kernel(q_ref, k_ref, v_ref, sink_ref, o_ref, qs_ref, *, ctx_len, span):
    tq = q_ref.shape[0]
    s = k_ref.shape[0]
    qi = pl.program_id(1)
    j = pl.program_id(2)
    _stack_queries(q_ref, qs_ref, j, tq)
    qs = qs_ref[...]
    sink = sink_ref[...]
    nt = (((1,), (1,)), ((), ()))
    kc = k_ref[0:ctx_len, :]
    vc = v_ref[0:ctx_len, :]
    s_ctx = lax.dot_general(qs, kc, nt, preferred_element_type=F32)
    is_ctx = qi * tq < ctx_len

    @pl.when(is_ctx)
    def _():
        m = jnp.maximum(jnp.max(s_ctx, axis=1, keepdims=True), sink)
        p = jnp.exp(s_ctx - _lane_tile(m, ctx_len // LANES))
        l = jnp.sum(p, axis=1, keepdims=True) + jnp.exp(sink - m)
        o = jnp.dot(p.astype(BF16), vc, preferred_element_type=F32)
        _unstack_heads(o / l, o_ref, j, tq)

    @pl.when(jnp.logical_not(is_ctx))
    def _():
        q0 = qi * tq
        start = pl.multiple_of(jnp.clip(q0 - WINDOW, ctx_len, s - span), LANES)
        kw = k_ref[pl.ds(start, span), :]
        vw = v_ref[pl.ds(start, span), :]
        s_win = lax.dot_general(qs, kw, nt, preferred_element_type=F32)
        qpos = q0 + (lax.broadcasted_iota(jnp.int32, (GROUP * tq, span), 0) & (tq - 1))
        kpos = start + lax.broadcasted_iota(jnp.int32, (GROUP * tq, span), 1)
        s_win = jnp.where(jnp.abs(kpos - qpos) <= WINDOW, s_win, NEG_INF)
        m = jnp.maximum(jnp.maximum(jnp.max(s_ctx, axis=1, keepdims=True),
                                    jnp.max(s_win, axis=1, keepdims=True)), sink)
        p_ctx = jnp.exp(s_ctx - _lane_tile(m, ctx_len // LANES))
        p_win = jnp.exp(s_win - _lane_tile(m, span // LANES))
        l = (jnp.sum(p_ctx, axis=1, keepdims=True) + jnp.sum(p_win, axis=1, keepdims=True)
             + jnp.exp(sink - m))
        o = jnp.dot(p_ctx.astype(BF16), vc, preferred_element_type=F32)
        o = o + jnp.dot(p_win.astype(BF16), vw, preferred_element_type=F32)
        _unstack_heads(o / l, o_ref, j, tq)


def _window_attention(q, k, v, sink_rows, ctx_len):
    b, s, nq = q.shape
    tq = ATT_TQ
    gw = GROUP * HEAD_DIM
    span = tq + 2 * WINDOW
    qspec = pl.BlockSpec((None, tq, gw), lambda bi, i, j: (bi, i, j))
    kvspec = pl.BlockSpec((None, s, LANES), lambda bi, i, j: (bi, 0, 0))
    return pl.pallas_call(
        functools.partial(_wattn_kernel, ctx_len=ctx_len, span=span),
        grid=(b, s // tq, KV_HEADS),
        in_specs=[qspec, kvspec, kvspec,
                  pl.BlockSpec((None, GROUP * tq, LANES), lambda bi, i, j: (j, 0, 0))],
        out_specs=qspec,
        out_shape=jax.ShapeDtypeStruct((b, s, nq), BF16),
        scratch_shapes=[pltpu.VMEM((GROUP * tq, LANES), BF16)],
        compiler_params=_cparams(("parallel", "parallel", "arbitrary")),
        name="window_attention",
    )(q, k, v, sink_rows)


def _rope_tables(n_lat, ctx_len):
    rows = n_lat // GRID_W
    row = jnp.repeat(jnp.arange(rows, dtype=F32), GRID_W)
    col = jnp.tile(jnp.arange(GRID_W, dtype=F32), rows)
    nf = HEAD_DIM // 4
    inv_freq = ROPE_THETA ** (-jnp.arange(nf, dtype=F32) / nf)
    ang_r = row[:, None] * inv_freq
    ang_c = col[:, None] * inv_freq
    cos = jnp.concatenate([jnp.cos(ang_r)] * 2 + [jnp.cos(ang_c)] * 2, axis=1)
    sin = jnp.concatenate([-jnp.sin(ang_r), jnp.sin(ang_r), -jnp.sin(ang_c), jnp.sin(ang_c)], axis=1)
    cos = jnp.concatenate([jnp.ones((ctx_len, HEAD_DIM), F32), cos], axis=0)
    sin = jnp.concatenate([jnp.zeros((ctx_len, HEAD_DIM), F32), sin], axis=0)
    return jnp.tile(cos, (1, LANES // HEAD_DIM)), jnp.tile(sin, (1, LANES // HEAD_DIM))


def kernel(x, c, ctx, c_ctx, ada_w, ada_b, ln1_g, ln1_b, ln2_g, ln2_b, ffn_w_in, ffn_w_out,
           ev_w_in, ev_w_out, rg_conv_w, rg_conv_b, rg_gate_w, rg_gate_b, rg_lambda, cm_w_s, cm_b_s,
           od_w_in, od_w_out, qn_g, kn_g, sink):
    b, t, d = x.shape
    ctx_len = ctx.shape[1]
    s = ctx_len + t
    depth = ada_w.shape[0]
    assert d == D_MODEL and depth == DEPTH and b + 1 <= 16
    assert ctx_len % ROW_TILE == 0 and t % ROW_TILE == 0 and t % GRID_W == 0

    c_rows = jnp.zeros((16, d), F32).at[:b].set(c).at[b].set(c_ctx)
    mods = _ada_vectors(c_rows, ada_w, ada_b).reshape(depth, 16, 6, d)
    modv = jnp.zeros((depth, b, 16, d), F32)
    modv = modv.at[:, :, 0:6].set(mods[:, :b])
    modv = modv.at[:, :, 8:14].set(jnp.broadcast_to(mods[:, b][:, None], (depth, b, 6, d)))

    cos_t, sin_t = _rope_tables(t, ctx_len)
    nq = Q_HEADS * HEAD_DIM
    bd = jnp.kron(jnp.eye(nq // HEAD_DIM, dtype=F32), jnp.full((HEAD_DIM, HEAD_DIM), 1.0 / HEAD_DIM, F32)).astype(BF16)
    nchunk = FFN_HIDDEN // FFN_CHUNK

    h = jnp.concatenate([ctx, x], axis=1)
    for l in range(depth):
        j = l // 2
        mv = modv[l]
        if l % 2 == 0:
            gg, xr, gu, vn = _in_even(h, mv, ev_w_in[j].astype(BF16), ctx_len)
            gw = jnp.transpose(rg_gate_w[j], (2, 0, 3, 1, 4)).reshape(RNN_HEADS, 2, LANES, 2 * LANES).astype(BF16)
            gb = jnp.transpose(rg_gate_b[j].reshape(2, 2, RNN_HEADS, LANES), (2, 0, 1, 3)).reshape(RNN_HEADS, 2, 1, 2 * LANES)
            mr = _rglru(xr, gg, rg_conv_w[j], rg_conv_b[j].reshape(1, -1), gw, gb, rg_lambda[j], ctx_len)
            bsb = jnp.broadcast_to(cm_b_s[j][:, :, None], (CMLP_GROUPS, CHUNK, CMLP_WIDTH // CMLP_GROUPS))
            h = _post_even(h, mv, mr, gu, vn, cm_w_s[j].astype(BF16), bsb, ev_w_out[j].astype(BF16),
                           ln1_g[l].reshape(1, d), ln1_b[l].reshape(1, d), ctx_len)
        else:
            gq = jnp.tile(qn_g[j], Q_HEADS).reshape(1, nq)
            gk = jnp.tile(kn_g[j], LANES // HEAD_DIM).reshape(1, LANES)
            qc, qd, kc, vc, kd, vd = _in_odd(h, mv, od_w_in[j].astype(BF16), bd, gq, gk, cos_t, sin_t, ctx_len)
            yc = _global_attention(qc, kc, vc, ctx_len)
            sink_rows = jnp.broadcast_to(sink[j].reshape(KV_HEADS, GROUP, 1, 1), (KV_HEADS, GROUP, ATT_TQ, LANES))
            yd = _window_attention(qd, kd, vd, sink_rows.reshape(KV_HEADS, GROUP * ATT_TQ, LANES), ctx_len)
            h = _post_odd(h, mv, yc, yd, od_w_out[j].astype(BF16),
                          ln1_g[l].reshape(1, d), ln1_b[l].reshape(1, d), ctx_len)
        wg = jnp.transpose(ffn_w_in[l][:, :FFN_HIDDEN].reshape(d, nchunk, FFN_CHUNK), (1, 0, 2)).astype(BF16)
        wu = jnp.transpose(ffn_w_in[l][:, FFN_HIDDEN:].reshape(d, nchunk, FFN_CHUNK), (1, 0, 2)).astype(BF16)
        wo = ffn_w_out[l].reshape(nchunk, FFN_CHUNK, d).astype(BF16)
        h = _ffn(h, mv, wg, wu, wo, ln2_g[l].reshape(1, d), ln2_b[l].reshape(1, d), ctx_len)
    return h[:, ctx_len:, :]
```

```python
import functools

import numpy as np
import jax
import jax.numpy as jnp
from jax import lax
from jax.experimental import pallas as pl
from jax.experimental.pallas import tpu as pltpu

F32 = jnp.float32
BF16 = jnp.bfloat16

D_MODEL = 1024
DEPTH = 4
GRID_W = 64
RNN_WIDTH = D_MODEL
RNN_HEADS = RNN_WIDTH // 128
CONV_W = 4
LRU_C = 8.0
CMLP_WIDTH = D_MODEL // 2
CMLP_GROUPS = 4
CHUNK = 128
HEAD_DIM = 64
Q_HEADS = 8
KV_HEADS = 2
GROUP = Q_HEADS // KV_HEADS
WINDOW = 128
ROPE_THETA = 10000.0
NEG_INF = -1e30
FFN_HIDDEN = 2816
FFN_CHUNK = 256
ALPHA = (2.0 * DEPTH) ** 0.25
LOG2E = 1.4426950408889634
EPS = 1e-6

LANES = 128
ROW_TILE = 256
SCAN_BLOCK = 128
ATT_TQ = 256
ATT_TK = 256
VMEM_LIMIT = 56 * 1024 * 1024


def _cparams(sem):
    return pltpu.CompilerParams(dimension_semantics=sem, vmem_limit_bytes=VMEM_LIMIT)


def _const_spec(shape):
    nd = len(shape)
    return pl.BlockSpec(shape, lambda *_: (0,) * nd, pipeline_mode=pl.Buffered(1))


def _gelu(x):
    return 0.5 * x * (1.0 + jnp.tanh(0.7978845608028654 * (x + 0.044715 * (x * x * x))))


def _normalise(x):
    mu = jnp.mean(x, axis=-1, keepdims=True)
    xc = x - mu
    var = jnp.mean(xc * xc, axis=-1, keepdims=True)
    return xc * lax.rsqrt(var + EPS)


def _mod_rows(m_ref, is_ctx, lat_row):
    return jnp.where(is_ctx, m_ref[lat_row + 8:lat_row + 9, :], m_ref[lat_row:lat_row + 1, :])


def _mod_kernel(c_ref, w_ref, b_ref, o_ref):
    c = c_ref[...]
    s = c * jax.nn.sigmoid(c)
    o_ref[...] = jnp.dot(s, w_ref[...], preferred_element_type=F32,
                         precision=lax.Precision.HIGHEST) + b_ref[...]


def _ada_vectors(c_rows, ada_w, ada_b):
    depth, d, n = ada_w.shape
    rows = c_rows.shape[0]
    nb = 1536
    return pl.pallas_call(
        _mod_kernel,
        grid=(depth, n // nb),
        in_specs=[pl.BlockSpec((rows, d), lambda l, j: (0, 0)),
                  pl.BlockSpec((None, d, nb), lambda l, j: (l, 0, j)),
                  pl.BlockSpec((None, 1, nb), lambda l, j: (l, 0, j))],
        out_specs=pl.BlockSpec((None, rows, nb), lambda l, j: (l, 0, j)),
        out_shape=jax.ShapeDtypeStruct((depth, rows, n), F32),
        compiler_params=_cparams(("parallel", "parallel")),
        name="ada_vectors",
    )(c_rows, ada_w, ada_b.reshape(depth, 1, n))


def _in_even_kernel(h_ref, m_ref, w_ref, gg_ref, xr_ref, gu_ref, vn_ref, *, ctx_len):
    tm = h_ref.shape[0]
    is_ctx = pl.program_id(1) * tm < ctx_len
    sh = _mod_rows(m_ref, is_ctx, 0)
    sc = _mod_rows(m_ref, is_ctx, 1)
    a = (h_ref[...] * (1.0 + sc) + sh).astype(BF16)
    w = RNN_WIDTH
    gate = jnp.dot(a, w_ref[:, 0:w], preferred_element_type=F32)
    gg_ref[...] = _gelu(gate).astype(BF16)
    xr_ref[...] = jnp.dot(a, w_ref[:, w:2 * w], preferred_element_type=F32)
    u = jnp.dot(a, w_ref[:, 2 * w:2 * w + CMLP_WIDTH], preferred_element_type=F32)
    gu_ref[...] = _gelu(u).astype(BF16)
    v = jnp.dot(a, w_ref[:, 2 * w + CMLP_WIDTH:], preferred_element_type=F32)
    vn_ref[...] = _normalise(_gelu(v)).astype(BF16)


def _in_even(h, modv, w_in, ctx_len):
    b, s, d = h.shape
    tm = ROW_TILE
    n_in = w_in.shape[1]
    row = lambda width: pl.BlockSpec((None, tm, width), lambda bi, i: (bi, i, 0))
    return pl.pallas_call(
        functools.partial(_in_even_kernel, ctx_len=ctx_len),
        grid=(b, s // tm),
        in_specs=[row(d),
                  pl.BlockSpec((None, 16, d), lambda bi, i: (bi, 0, 0)),
                  _const_spec((d, n_in))],
        out_specs=[row(RNN_WIDTH), row(RNN_WIDTH), row(CMLP_WIDTH), row(CMLP_WIDTH)],
        out_shape=[jax.ShapeDtypeStruct((b, s, RNN_WIDTH), BF16),
                   jax.ShapeDtypeStruct((b, s, RNN_WIDTH), F32),
                   jax.ShapeDtypeStruct((b, s, CMLP_WIDTH), BF16),
                   jax.ShapeDtypeStruct((b, s, CMLP_WIDTH), BF16)],
        compiler_params=_cparams(("parallel", "parallel")),
        name="even_in_proj",
    )(h, modv, w_in)


def _block_scan(a_ref, b_ref, carry, reverse):
    ntile = a_ref.shape[0] // 8
    sub = lax.broadcasted_iota(jnp.int32, (8, a_ref.shape[1]), 0)
    out = [None] * ntile
    for k in (range(ntile - 1, -1, -1) if reverse else range(ntile)):
        ak = a_ref[8 * k:8 * k + 8, :]
        bk = b_ref[8 * k:8 * k + 8, :]
        for d in (1, 2, 4):
            if reverse:
                a_s = pltpu.roll(ak, 8 - d, 0)
                b_s = pltpu.roll(bk, 8 - d, 0)
                ok = sub < 8 - d
            else:
                a_s = pltpu.roll(ak, d, 0)
                b_s = pltpu.roll(bk, d, 0)
                ok = sub >= d
            bk = jnp.where(ok, bk + ak * b_s, bk)
            ak = jnp.where(ok, ak * a_s, ak)
        out[k] = bk + ak * carry
        last = 0 if reverse else 7
        carry = (jnp.broadcast_to(bk[last:last + 1, :], bk.shape)
                 + jnp.broadcast_to(ak[last:last + 1, :], ak.shape) * carry)
    return jnp.concatenate(out, axis=0), carry


def _scan_kernel(xr_ref, gg_ref, cw_ref, cb_ref, gw_ref, gb_ref, lam_ref, out_ref, xc_ref, rf_ref, rr_ref,
                 ab0_ref, ab1_ref, g0_ref, g1_ref, *, ctx_len, tb):
    s = xr_ref.shape[0]
    nblk = s // tb
    ncb = ctx_len // tb
    cw = cw_ref[...]
    cb = cb_ref[...]
    row = lax.broadcasted_iota(jnp.int32, (tb, LANES), 0)

    def conv(blk, _):
        t0 = pl.multiple_of(blk * tb, tb)
        x = xr_ref[pl.ds(t0, tb), :]
        prev = xr_ref[pl.ds(pl.multiple_of(jnp.maximum(t0 - 8, 0), 8), 8), :]
        nxt = xr_ref[pl.ds(pl.multiple_of(jnp.minimum(t0 + tb, s - 8), 8), 8), :]
        pf = jnp.where((blk == 0) | (blk == ncb), 0.0, 1.0)
        nf = jnp.where((blk == ncb - 1) | (blk == nblk - 1), 0.0, 1.0)
        p6 = prev[6:7, :] * pf
        p7 = prev[7:8, :] * pf
        n0 = nxt[0:1, :] * nf
        xm1 = jnp.where(row == 0, p7, pltpu.roll(x, 1, 0))
        xm2 = jnp.where(row == 0, p6, jnp.where(row == 1, p7, pltpu.roll(x, 2, 0)))
        xp1 = jnp.where(row == tb - 1, n0, pltpu.roll(x, tb - 1, 0))
        xc_ref[pl.ds(t0, tb), :] = (xm2 * cw[0:1, :] + xm1 * cw[1:2, :] + x * cw[2:3, :] + xp1 * cw[3:4, :]
                                    + cb)
        return 0

    lax.fori_loop(0, nblk, conv, 0)

    def block_start(step, d):
        blk = step if d == 0 else jnp.where(step < ncb, ncb - 1 - step, nblk - 1 - (step - ncb))
        return pl.multiple_of(blk * tb, tb)

    def gate_matmuls(step, g_ref):
        step = jnp.minimum(step, nblk - 1)
        for d in range(2):
            xc = xc_ref[pl.ds(block_start(step, d), tb), :]
            g_ref[d] = jnp.dot(xc.astype(BF16), gw_ref[d], preferred_element_type=F32)

    def stash_coeffs(step, g_ref, ab_ref):
        step = jnp.minimum(step, nblk - 1)
        for d in range(2):
            xc = xc_ref[pl.ds(block_start(step, d), tb), :]
            g = g_ref[d] + gb_ref[d]
            r = jax.nn.sigmoid(g[:, :LANES])
            ig = jax.nn.sigmoid(g[:, LANES:])
            z = -lam_ref[d:d + 1, :]
            softplus = jnp.maximum(z, 0.0) + jnp.log(1.0 + jnp.exp(-jnp.abs(z)))
            a = jnp.exp((-LRU_C) * r * softplus)
            ab_ref[2 * d] = a
            ab_ref[2 * d + 1] = jnp.sqrt(1.0 - a * a) * (ig * xc)

    def scan(step, ab_ref, cf, cr):
        hf, cf = _block_scan(ab_ref.at[0], ab_ref.at[1], cf, False)
        rf_ref[pl.ds(block_start(step, 0), tb), :] = hf
        hr, cr = _block_scan(ab_ref.at[2], ab_ref.at[3], cr, True)
        rr_ref[pl.ds(block_start(step, 1), tb), :] = hr
        return cf, cr

    gate_matmuls(0, g0_ref)
    stash_coeffs(0, g0_ref, ab0_ref)
    gate_matmuls(1, g1_ref)
    gate_matmuls(2, g0_ref)

    def two_steps(u, carry):
        cf, cr = carry
        cf, cr = scan(2 * u, ab0_ref, cf, cr)
        stash_coeffs(2 * u + 1, g1_ref, ab1_ref)
        cf, cr = scan(2 * u + 1, ab1_ref, cf, cr)
        stash_coeffs(2 * u + 2, g0_ref, ab0_ref)
        gate_matmuls(2 * u + 3, g1_ref)
        gate_matmuls(2 * u + 4, g0_ref)
        return cf, cr

    zero = jnp.zeros((8, LANES), F32)
    lax.fori_loop(0, nblk // 2, two_steps, (zero, zero))

    def combine(j, _):
        t0 = pl.multiple_of(j * tb, tb)
        rec = rf_ref[pl.ds(t0, tb), :] + rr_ref[pl.ds(t0, tb), :]
        out_ref[pl.ds(t0, tb), :] = (gg_ref[pl.ds(t0, tb), :].astype(F32) * rec).astype(BF16)
        return 0

    lax.fori_loop(0, nblk, combine, 0)


def _rglru(xr, gg, conv_w, conv_b, gate_w, gate_b, lam, ctx_len):
    b, s, w = xr.shape
    nh = w // LANES
    assert (s // SCAN_BLOCK) % 2 == 0 and ctx_len % SCAN_BLOCK == 0
    col = lambda dt: pl.BlockSpec((None, s, LANES), lambda bi, hd: (bi, 0, hd))
    return pl.pallas_call(
        functools.partial(_scan_kernel, ctx_len=ctx_len, tb=SCAN_BLOCK),
        grid=(b, nh),
        in_specs=[col(F32), col(BF16),
                  pl.BlockSpec((CONV_W, LANES), lambda bi, hd: (0, hd)),
                  pl.BlockSpec((1, LANES), lambda bi, hd: (0, hd)),
                  pl.BlockSpec((None, 2, LANES, 2 * LANES), lambda bi, hd: (hd, 0, 0, 0)),
                  pl.BlockSpec((None, 2, 1, 2 * LANES), lambda bi, hd: (hd, 0, 0, 0)),
                  pl.BlockSpec((2, LANES), lambda bi, hd: (0, hd))],
        out_specs=col(BF16),
        out_shape=jax.ShapeDtypeStruct((b, s, w), BF16),
        scratch_shapes=[pltpu.VMEM((s, LANES), F32), pltpu.VMEM((s, LANES), F32), pltpu.VMEM((s, LANES), F32),
                        pltpu.VMEM((4, SCAN_BLOCK, LANES), F32), pltpu.VMEM((4, SCAN_BLOCK, LANES), F32),
                        pltpu.VMEM((2, SCAN_BLOCK, 2 * LANES), F32), pltpu.VMEM((2, SCAN_BLOCK, 2 * LANES), F32)],
        compiler_params=_cparams(("parallel", "parallel")),
        name="rglru_scan",
    )(xr, gg, conv_w, conv_b, gate_w, gate_b, lam)


def _residual_ln(h, y, gate, g, b):
    return _normalise(ALPHA * h + gate * y) * g + b


def _post_even_kernel(h_ref, m_ref, mr_ref, gu_ref, vn_ref, ws_ref, bs_ref, wo_ref, lg_ref, lb_ref,
                      o_ref, gm_ref, *, ctx_len):
    tm = h_ref.shape[0]
    is_ctx = pl.program_id(1) * tm < ctx_len
    gw = CMLP_WIDTH // CMLP_GROUPS
    for c in range(tm // CHUNK):
        rows = slice(c * CHUNK, (c + 1) * CHUNK)
        for g in range(CMLP_GROUPS):
            cols = slice(g * gw, (g + 1) * gw)
            mixed = jnp.dot(ws_ref[g], vn_ref[rows, cols], preferred_element_type=F32) + bs_ref[g]
            gm_ref[rows, cols] = (gu_ref[rows, cols].astype(F32) * mixed).astype(BF16)
    y = jnp.dot(mr_ref[...], wo_ref[0:RNN_WIDTH, :], preferred_element_type=F32)
    y = y + jnp.dot(gm_ref[...], wo_ref[RNN_WIDTH:, :], preferred_element_type=F32)
    o_ref[...] = _residual_ln(h_ref[...], y, _mod_rows(m_ref, is_ctx, 2), lg_ref[...], lb_ref[...])


def _post_even(h, modv, mr, gu, vn, ws, bsb, w_out, ln_g, ln_b, ctx_len):
    b, s, d = h.shape
    tm = ROW_TILE
    row = lambda width: pl.BlockSpec((None, tm, width), lambda bi, i: (bi, i, 0))
    return pl.pallas_call(
        functools.partial(_post_even_kernel, ctx_len=ctx_len),
        grid=(b, s // tm),
        in_specs=[row(d),
                  pl.BlockSpec((None, 16, d), lambda bi, i: (bi, 0, 0)),
                  row(RNN_WIDTH), row(CMLP_WIDTH), row(CMLP_WIDTH),
                  _const_spec(ws.shape), _const_spec(bsb.shape), _const_spec(w_out.shape),
                  _const_spec((1, d)), _const_spec((1, d))],
        out_specs=row(d),
        out_shape=jax.ShapeDtypeStruct((b, s, d), F32),
        scratch_shapes=[pltpu.VMEM((tm, CMLP_WIDTH), BF16)],
        compiler_params=_cparams(("parallel", "parallel")),
        name="even_out_proj",
    )(h, modv, mr, gu, vn, ws, bsb, w_out, ln_g, ln_b)


def _post_odd_kernel(h_ref, m_ref, yc_ref, yd_ref, wo_ref, lg_ref, lb_ref, o_ref, *, ctx_len, first_tile):
    tm = h_ref.shape[0]
    is_ctx = (pl.program_id(1) + first_tile) * tm < ctx_len
    half = yc_ref.shape[1]
    y = jnp.dot(yc_ref[...], wo_ref[0:half, :], preferred_element_type=F32)
    y = y + jnp.dot(yd_ref[...], wo_ref[half:, :], preferred_element_type=F32)
    o_ref[...] = _residual_ln(h_ref[...], y, _mod_rows(m_ref, is_ctx, 2), lg_ref[...], lb_ref[...])


def _post_odd(h, modv, yc, yd, w_out, ln_g, ln_b, ctx_len, first_tile):
    b, s, d = h.shape
    tm = ROW_TILE
    n = s // tm - first_tile
    assert yc.shape[1] == n * tm and yd.shape[1] == n * tm
    row = lambda width: pl.BlockSpec((None, tm, width), lambda bi, i: (bi, i, 0))
    return pl.pallas_call(
        functools.partial(_post_odd_kernel, ctx_len=ctx_len, first_tile=first_tile),
        grid=(b, n),
        in_specs=[pl.BlockSpec((None, tm, d), lambda bi, i: (bi, i + first_tile, 0)),
                  pl.BlockSpec((None, 16, d), lambda bi, i: (bi, 0, 0)),
                  row(yc.shape[2]), row(yd.shape[2]),
                  _const_spec(w_out.shape), _const_spec((1, d)), _const_spec((1, d))],
        out_specs=row(d),
        out_shape=jax.ShapeDtypeStruct((b, n * tm, d), F32),
        compiler_params=_cparams(("parallel", "parallel")),
        name="odd_out_proj",
    )(h, modv, yc, yd, w_out, ln_g, ln_b)


def _ffn_kernel(h_ref, m_ref, wg_ref, wu_ref, wo_ref, lg_ref, lb_ref, o_ref, acc_ref, *, ctx_len):
    tm = h_ref.shape[0]
    is_ctx = pl.program_id(1) * tm < ctx_len
    h = h_ref[...]
    a = (h * (1.0 + _mod_rows(m_ref, is_ctx, 4)) + _mod_rows(m_ref, is_ctx, 3)).astype(BF16)
    acc_ref[...] = jnp.zeros_like(acc_ref)

    def chunk(j, _):
        zg = jnp.dot(a, wg_ref[j], preferred_element_type=F32)
        zu = jnp.dot(a, wu_ref[j], preferred_element_type=F32)
        hm = (zg * jax.nn.sigmoid(zg) * zu).astype(BF16)
        acc_ref[...] += jnp.dot(hm, wo_ref[j], preferred_element_type=F32)
        return 0

    lax.fori_loop(0, wg_ref.shape[0], chunk, 0, unroll=True)
    o_ref[...] = _residual_ln(h, acc_ref[...], _mod_rows(m_ref, is_ctx, 5), lg_ref[...], lb_ref[...])


def _ffn(h, modv, wg, wu, wo, ln_g, ln_b, ctx_len):
    b, s, d = h.shape
    tm = ROW_TILE
    row = pl.BlockSpec((None, tm, d), lambda bi, i: (bi, i, 0))
    return pl.pallas_call(
        functools.partial(_ffn_kernel, ctx_len=ctx_len),
        grid=(b, s // tm),
        in_specs=[row,
                  pl.BlockSpec((None, 16, d), lambda bi, i: (bi, 0, 0)),
                  _const_spec(wg.shape), _const_spec(wu.shape), _const_spec(wo.shape),
                  _const_spec((1, d)), _const_spec((1, d))],
        out_specs=row,
        out_shape=jax.ShapeDtypeStruct((b, s, d), F32),
        scratch_shapes=[pltpu.VMEM((tm, d), F32)],
        compiler_params=_cparams(("parallel", "parallel")),
        name="swiglu_ffn",
    )(h, modv, wg, wu, wo, ln_g, ln_b)


def _in_odd_kernel(h_ref, m_ref, w_ref, bd_ref, gq_ref, gk_ref, cos_ref, sin_ref,
                   qc_ref, qd_ref, kc_ref, vc_ref, kd_ref, vd_ref, *, ctx_len):
    tm = h_ref.shape[0]
    is_ctx = pl.program_id(1) * tm < ctx_len
    sh = _mod_rows(m_ref, is_ctx, 0)
    sc = _mod_rows(m_ref, is_ctx, 1)
    a = (h_ref[...] * (1.0 + sc) + sh).astype(BF16)
    cos = cos_ref[...]
    sin = sin_ref[...]
    lane = lax.broadcasted_iota(jnp.int32, (tm, LANES), 1)
    first = (lane & 31) < 16
    scale = HEAD_DIM ** -0.5

    def rope(x):
        partner = jnp.where(first, pltpu.roll(x, LANES - 16, 1), pltpu.roll(x, 16, 1))
        return x * cos + partner * sin

    def rms(x, gain, width):
        ms = jnp.dot((x * x).astype(BF16), bd_ref[0:width, 0:width], preferred_element_type=F32)
        return x * lax.rsqrt(ms + EPS) * gain

    nq = Q_HEADS * HEAD_DIM
    cq = rms(jnp.dot(a, w_ref[:, 0:nq], preferred_element_type=F32), gq_ref[...], nq)
    for m in range(nq // LANES):
        qc_ref[m * LANES:(m + 1) * LANES, :] = (rope(cq[:, m * LANES:(m + 1) * LANES]) * (scale * LOG2E)).T.astype(BF16)
    dq = jnp.dot(a, w_ref[:, nq:2 * nq], preferred_element_type=F32)
    for m in range(nq // LANES):
        qd_ref[m * LANES:(m + 1) * LANES, :] = (rope(dq[:, m * LANES:(m + 1) * LANES]) * (scale * LOG2E)).T.astype(BF16)
    kv = jnp.dot(a, w_ref[:, 2 * nq:], preferred_element_type=F32)
    kc_ref[...] = rope(rms(kv[:, 0:LANES], gk_ref[...], LANES)).astype(BF16)
    kd_ref[...] = rope(kv[:, 2 * LANES:3 * LANES]).astype(BF16)
    rowi = lax.broadcasted_iota(jnp.int32, (LANES, tm), 0)
    pad = jnp.where(rowi == HEAD_DIM, 1.0, 0.0)

    def value_rows(v):
        vt = v.T
        return (jnp.where(rowi < HEAD_DIM, vt, pad).astype(BF16),
                jnp.where(rowi < HEAD_DIM, jnp.concatenate([vt[HEAD_DIM:], vt[:HEAD_DIM]], axis=0), pad).astype(BF16))

    vc_ref[0], vc_ref[1] = value_rows(kv[:, LANES:2 * LANES])
    vd = value_rows(kv[:, 3 * LANES:4 * LANES])
    for jj in range(KV_HEADS):
        for c in range(tm // LANES):
            vd_ref[jj, c] = vd[jj][:, c * LANES:(c + 1) * LANES]


def _in_odd(h, modv, w_in, bd, gq, gk, cos_t, sin_t, ctx_len):
    b, s, d = h.shape
    tm = ROW_TILE
    nq = Q_HEADS * HEAD_DIM
    row = lambda width: pl.BlockSpec((None, tm, width), lambda bi, i: (bi, i, 0))
    tab = pl.BlockSpec((tm, LANES), lambda bi, i: (i, 0))
    qt = pl.BlockSpec((None, nq, tm), lambda bi, i: (bi, 0, i))
    sds = lambda width: jax.ShapeDtypeStruct((b, s, width), BF16)
    return pl.pallas_call(
        functools.partial(_in_odd_kernel, ctx_len=ctx_len),
        grid=(b, s // tm),
        in_specs=[row(d),
                  pl.BlockSpec((None, 16, d), lambda bi, i: (bi, 0, 0)),
                  _const_spec(w_in.shape), _const_spec(bd.shape),
                  _const_spec(gq.shape), _const_spec(gk.shape), tab, tab],
        out_specs=[qt, qt, row(LANES),
                   pl.BlockSpec((None, KV_HEADS, None, LANES, tm), lambda bi, i: (bi, 0, i, 0, 0)),
                   row(LANES),
                   pl.BlockSpec((None, KV_HEADS, tm // LANES, LANES, LANES), lambda bi, i: (bi, 0, i, 0, 0))],
        out_shape=[jax.ShapeDtypeStruct((b, nq, s), BF16), jax.ShapeDtypeStruct((b, nq, s), BF16), sds(LANES),
                   jax.ShapeDtypeStruct((b, KV_HEADS, s // tm, LANES, tm), BF16), sds(LANES),
                   jax.ShapeDtypeStruct((b, KV_HEADS, s // LANES, LANES, LANES), BF16)],
        compiler_params=_cparams(("parallel", "parallel")),
        name="odd_in_proj",
    )(h, modv, w_in, bd, gq, gk, cos_t, sin_t)


def _stack_queries(q_ref, qs_ref, j, tq):
    for hh in range(GROUP):
        qh = q_ref[hh * HEAD_DIM:(hh + 1) * HEAD_DIM, :]
        zero = jnp.zeros_like(qh)
        qs_ref[0:HEAD_DIM, hh * tq:(hh + 1) * tq] = jnp.where(j == 0, qh, zero)
        qs_ref[HEAD_DIM:, hh * tq:(hh + 1) * tq] = jnp.where(j == 0, zero, qh)


def _store_heads(o, o_ref, tq):
    for m in range(GROUP // 2):
        pair = jnp.concatenate([o[:, (2 * m) * tq:(2 * m + 1) * tq],
                                o[:, (2 * m + 1) * tq:(2 * m + 2) * tq]], axis=0)
        o_ref[:, m * LANES:(m + 1) * LANES] = pair.T.astype(o_ref.dtype)


def _gattn_kernel(q_ref, k_ref, v_ref, o_ref, qs_ref, m_ref, acc_ref, sa_ref, sb_ref, *, ctx_len, first_tile):
    tq = q_ref.shape[1]
    ntile, _, tk = v_ref.shape
    j = pl.program_id(1)
    qi = pl.program_id(2) + first_tile
    _stack_queries(q_ref, qs_ref, j, tq)
    m_ref[...] = jnp.full(m_ref.shape, NEG_INF, F32)
    acc_ref[...] = jnp.zeros(acc_ref.shape, F32)
    nkv = jnp.where(qi * tq < ctx_len, ctx_len // tk, ntile)

    def scores(t, s_ref):
        start = pl.multiple_of(t * tk, tk)
        s_ref[...] = jnp.dot(k_ref[pl.ds(start, tk), :], qs_ref[...], preferred_element_type=F32)

    def update(t, s_ref):
        sc = s_ref[...]
        m_prev = m_ref[...]
        m_new = jnp.maximum(m_prev, jnp.max(sc, axis=0, keepdims=True))
        alpha = jnp.exp2(m_prev - m_new)
        p = jnp.exp2(sc - m_new)
        acc_ref[...] = alpha * acc_ref[...] + jnp.dot(v_ref[t], p.astype(BF16), preferred_element_type=F32)
        m_ref[...] = m_new

    scores(0, sa_ref)

    def quad(u, _):
        t = 4 * u
        scores(t + 1, sb_ref)
        update(t, sa_ref)
        scores(t + 2, sa_ref)
        update(t + 1, sb_ref)
        scores(t + 3, sb_ref)
        update(t + 2, sa_ref)
        scores(t + 4, sa_ref)
        update(t + 3, sb_ref)
        return 0

    lax.fori_loop(0, nkv // 4, quad, 0)
    update(nkv - 1, sa_ref)
    _store_heads(acc_ref[0:HEAD_DIM, :] / acc_ref[HEAD_DIM:HEAD_DIM + 1, :], o_ref, tq)


def _global_attention(qt, k, vt, ctx_len, first_tile):
    b, nq, s = qt.shape
    tq = ATT_TQ
    gw = GROUP * HEAD_DIM
    assert vt.shape[4] == ATT_TK
    assert (s // ATT_TK) % 4 == 1 and (ctx_len // ATT_TK) % 4 == 1
    return pl.pallas_call(
        functools.partial(_gattn_kernel, ctx_len=ctx_len, first_tile=first_tile),
        grid=(b, KV_HEADS, s // tq - first_tile),
        in_specs=[pl.BlockSpec((None, gw, tq), lambda bi, j, i: (bi, j, i + first_tile)),
                  pl.BlockSpec((None, s, LANES), lambda bi, j, i: (bi, 0, 0)),
                  pl.BlockSpec((None, None) + vt.shape[2:], lambda bi, j, i: (bi, j, 0, 0, 0))],
        out_specs=pl.BlockSpec((None, tq, gw), lambda bi, j, i: (bi, i, j)),
        out_shape=jax.ShapeDtypeStruct((b, s - first_tile * tq, nq), BF16),
        scratch_shapes=[pltpu.VMEM((LANES, GROUP * tq), BF16),
                        pltpu.VMEM((1, GROUP * tq), F32),
                        pltpu.VMEM((LANES, GROUP * tq), F32),
                        pltpu.VMEM((ATT_TK, GROUP * tq), F32),
                        pltpu.VMEM((ATT_TK, GROUP * tq), F32)],
        compiler_params=_cparams(("parallel", "parallel", "arbitrary")),
        name="global_attention",
    )(qt, k, vt)


def _wattn_kernel(q_ref, k_ref, v_ref, sink_ref, bias_ref, o_ref, qs_ref, *, ctx_len, span, first_tile):
    tq = q_ref.shape[1]
    s = k_ref.shape[0]
    j = pl.program_id(1)
    qi = pl.program_id(2) + first_tile
    _stack_queries(q_ref, qs_ref, j, tq)
    qs = qs_ref[...]
    sink = sink_ref[...] * LOG2E
    s_ctx = jnp.dot(k_ref[0:ctx_len, :], qs, preferred_element_type=F32)
    v_ctx = jnp.concatenate([v_ref[t] for t in range(ctx_len // LANES)], axis=1)
    is_ctx = qi * tq < ctx_len

    def finish(acc, m):
        den = acc[HEAD_DIM:HEAD_DIM + 1, :] + jnp.exp2(sink - m)
        _store_heads(acc[0:HEAD_DIM, :] / den, o_ref, tq)

    @pl.when(is_ctx)
    def _():
        m = jnp.maximum(jnp.max(s_ctx, axis=0, keepdims=True), sink)
        p = jnp.exp2(s_ctx - m)
        finish(jnp.dot(v_ctx, p.astype(BF16), preferred_element_type=F32), m)

    @pl.when(jnp.logical_not(is_ctx))
    def _():
        start = pl.multiple_of(jnp.clip(qi * tq - WINDOW, ctx_len, s - span), LANES)
        s_win = jnp.dot(k_ref[pl.ds(start, span), :], qs, preferred_element_type=F32) + bias_ref[...]
        m = jnp.maximum(jnp.maximum(jnp.max(s_ctx, axis=0, keepdims=True),
                                    jnp.max(s_win, axis=0, keepdims=True)), sink)
        p_ctx = jnp.exp2(s_ctx - m)
        p_win = jnp.exp2(s_win - m)
        t0 = start // LANES
        v_win = jnp.concatenate([v_ref[t0 + i] for i in range(span // LANES)], axis=1)
        acc = jnp.dot(v_ctx, p_ctx.astype(BF16), preferred_element_type=F32)
        finish(acc + jnp.dot(v_win, p_win.astype(BF16), preferred_element_type=F32), m)


def _window_attention(qt, k, vt, sink, ctx_len, first_tile):
    b, nq, s = qt.shape
    tq = ATT_TQ
    gw = GROUP * HEAD_DIM
    span = tq + 2 * WINDOW
    assert tq >= WINDOW and (s - ctx_len) // tq >= 2
    sink_row = jnp.repeat(sink.reshape(KV_HEADS, 1, GROUP), tq, axis=2)
    r = jnp.arange(span)[:, None]
    qcol = (jnp.arange(GROUP * tq) % tq)[None, :]
    bias = jnp.stack([jnp.where(jnp.abs(off + r - qcol) <= WINDOW, 0.0, NEG_INF)
                      for off in (0, -WINDOW, tq - span)]).astype(F32)

    def placement(i):
        lo = (i + first_tile) * tq - WINDOW
        return jnp.where(lo < ctx_len, 0, jnp.where(lo > s - span, 2, 1))

    return pl.pallas_call(
        functools.partial(_wattn_kernel, ctx_len=ctx_len, span=span, first_tile=first_tile),
        grid=(b, KV_HEADS, s // tq - first_tile),
        in_specs=[pl.BlockSpec((None, gw, tq), lambda bi, j, i: (bi, j, i + first_tile)),
                  pl.BlockSpec((None, s, LANES), lambda bi, j, i: (bi, 0, 0)),
                  pl.BlockSpec((None, None) + vt.shape[2:], lambda bi, j, i: (bi, j, 0, 0, 0)),
                  pl.BlockSpec((None, 1, GROUP * tq), lambda bi, j, i: (j, 0, 0)),
                  pl.BlockSpec((None, span, GROUP * tq), lambda bi, j, i: (placement(i), 0, 0))],
        out_specs=pl.BlockSpec((None, tq, gw), lambda bi, j, i: (bi, i, j)),
        out_shape=jax.ShapeDtypeStruct((b, s - first_tile * tq, nq), BF16),
        scratch_shapes=[pltpu.VMEM((LANES, GROUP * tq), BF16)],
        compiler_params=_cparams(("parallel", "parallel", "arbitrary")),
        name="window_attention",
    )(qt, k, vt, sink_row, bias)


def _rope_tables(n_lat, ctx_len):
    rows = n_lat // GRID_W
    row = jnp.repeat(jnp.arange(rows, dtype=F32), GRID_W)
    col = jnp.tile(jnp.arange(GRID_W, dtype=F32), rows)
    nf = HEAD_DIM // 4
    inv_freq = ROPE_THETA ** (-jnp.arange(nf, dtype=F32) / nf)
    ang_r = row[:, None] * inv_freq
    ang_c = col[:, None] * inv_freq
    cos = jnp.concatenate([jnp.cos(ang_r)] * 2 + [jnp.cos(ang_c)] * 2, axis=1)
    sin = jnp.concatenate([-jnp.sin(ang_r), jnp.sin(ang_r), -jnp.sin(ang_c), jnp.sin(ang_c)], axis=1)
    cos = jnp.concatenate([jnp.ones((ctx_len, HEAD_DIM), F32), cos], axis=0)
    sin = jnp.concatenate([jnp.zeros((ctx_len, HEAD_DIM), F32), sin], axis=0)
    return jnp.tile(cos, (1, LANES // HEAD_DIM)), jnp.tile(sin, (1, LANES // HEAD_DIM))


def kernel(x, c, ctx, c_ctx, ada_w, ada_b, ln1_g, ln1_b, ln2_g, ln2_b, ffn_w_in, ffn_w_out,
           ev_w_in, ev_w_out, rg_conv_w, rg_conv_b, rg_gate_w, rg_gate_b, rg_lambda, cm_w_s, cm_b_s,
           od_w_in, od_w_out, qn_g, kn_g, sink):
    b, t, d = x.shape
    ctx_len = ctx.shape[1]
    s = ctx_len + t
    depth = ada_w.shape[0]
    assert d == D_MODEL and depth == DEPTH and b + 1 <= 16
    assert ctx_len % ROW_TILE == 0 and t % ROW_TILE == 0 and t % GRID_W == 0 and ATT_TQ == ROW_TILE

    c_rows = jnp.zeros((16, d), F32).at[:b].set(c).at[b].set(c_ctx)
    mods = _ada_vectors(c_rows, ada_w, ada_b).reshape(depth, 16, 6, d)
    modv = jnp.zeros((depth, b, 16, d), F32)
    modv = modv.at[:, :, 0:6].set(mods[:, :b])
    modv = modv.at[:, :, 8:14].set(jnp.broadcast_to(mods[:, b][:, None], (depth, b, 6, d)))

    cos_t, sin_t = _rope_tables(t, ctx_len)
    nq = Q_HEADS * HEAD_DIM
    bd = jnp.kron(jnp.eye(nq // HEAD_DIM, dtype=F32), jnp.full((HEAD_DIM, HEAD_DIM), 1.0 / HEAD_DIM, F32)).astype(BF16)
    nchunk = FFN_HIDDEN // FFN_CHUNK

    h = jnp.concatenate([ctx, x], axis=1)
    rows_ctx = ctx_len
    for l in range(depth):
        j = l // 2
        mv = modv[l]
        if l % 2 == 0:
            gg, xr, gu, vn = _in_even(h, mv, ev_w_in[j].astype(BF16), ctx_len)
            gw = jnp.transpose(rg_gate_w[j], (2, 0, 3, 1, 4)).reshape(RNN_HEADS, 2, LANES, 2 * LANES).astype(BF16)
            gb = jnp.transpose(rg_gate_b[j].reshape(2, 2, RNN_HEADS, LANES), (2, 0, 1, 3)).reshape(RNN_HEADS, 2, 1, 2 * LANES)
            mr = _rglru(xr, gg, rg_conv_w[j], rg_conv_b[j].reshape(1, -1), gw, gb, rg_lambda[j], ctx_len)
            bsb = jnp.broadcast_to(cm_b_s[j][:, :, None], (CMLP_GROUPS, CHUNK, CMLP_WIDTH // CMLP_GROUPS))
            h = _post_even(h, mv, mr, gu, vn, cm_w_s[j].astype(BF16), bsb, ev_w_out[j].astype(BF16),
                           ln1_g[l].reshape(1, d), ln1_b[l].reshape(1, d), ctx_len)
        else:
            gq = jnp.tile(qn_g[j], Q_HEADS).reshape(1, nq)
            gk = jnp.tile(kn_g[j], LANES // HEAD_DIM).reshape(1, LANES)
            qc, qd, kc, vc, kd, vd = _in_odd(h, mv, od_w_in[j].astype(BF16), bd, gq, gk, cos_t, sin_t, ctx_len)
            skip = ctx_len // ATT_TQ if l == depth - 1 else 0
            yc = _global_attention(qc, kc, vc, ctx_len, skip)
            yd = _window_attention(qd, kd, vd, sink[j], ctx_len, skip)
            h = _post_odd(h, mv, yc, yd, od_w_out[j].astype(BF16),
                          ln1_g[l].reshape(1, d), ln1_b[l].reshape(1, d), ctx_len, skip)
            if skip:
                rows_ctx = 0
        wg = jnp.transpose(ffn_w_in[l][:, :FFN_HIDDEN].reshape(d, nchunk, FFN_CHUNK), (1, 0, 2)).astype(BF16)
        wu = jnp.transpose(ffn_w_in[l][:, FFN_HIDDEN:].reshape(d, nchunk, FFN_CHUNK), (1, 0, 2)).astype(BF16)
        wo = ffn_w_out[l].reshape(nchunk, FFN_CHUNK, d).astype(BF16)
        h = _ffn(h, mv, wg, wu, wo, ln2_g[l].reshape(1, d), ln2_b[l].reshape(1, d), rows_ctx)
    return h[:, rows_ctx:, :]
```

```python
import functools

import numpy as np
import jax
import jax.numpy as jnp
from jax import lax
from jax.experimental import pallas as pl
from jax.experimental.pallas import tpu as pltpu

F32 = jnp.float32
BF16 = jnp.bfloat16

D_MODEL = 1024
DEPTH = 4
GRID_W = 64
RNN_WIDTH = D_MODEL
RNN_HEADS = RNN_WIDTH // 128
CONV_W = 4
LRU_C = 8.0
CMLP_WIDTH = D_MODEL // 2
CMLP_GROUPS = 4
CHUNK = 128
HEAD_DIM = 64
Q_HEADS = 8
KV_HEADS = 2
GROUP = Q_HEADS // KV_HEADS
WINDOW = 128
ROPE_THETA = 10000.0
NEG_INF = -1e30
FFN_HIDDEN = 2816
FFN_CHUNKS = (1024, 1024, 768)
ALPHA = (2.0 * DEPTH) ** 0.25
LOG2E = 1.4426950408889634
EPS = 1e-6

LANES = 128
ROW_TILE = 256
SCAN_BLOCK = 128
ATT_TQ = 256
ATT_TK = 256
VMEM_LIMIT = 56 * 1024 * 1024


def _cparams(sem):
    return pltpu.CompilerParams(dimension_semantics=sem, vmem_limit_bytes=VMEM_LIMIT)


def _const_spec(shape):
    nd = len(shape)
    return pl.BlockSpec(shape, lambda *_: (0,) * nd, pipeline_mode=pl.Buffered(1))


def _gelu(x):
    return 0.5 * x * (1.0 + jnp.tanh(0.7978845608028654 * (x + 0.044715 * (x * x * x))))


def _normalise(x):
    mu = jnp.mean(x, axis=-1, keepdims=True)
    xc = x - mu
    var = jnp.mean(xc * xc, axis=-1, keepdims=True)
    return xc * lax.rsqrt(var + EPS)


def _mod_rows(m_ref, is_ctx, lat_row):
    return jnp.where(is_ctx, m_ref[lat_row + 8:lat_row + 9, :], m_ref[lat_row:lat_row + 1, :])


def _mod_kernel(c_ref, w_ref, b_ref, o_ref):
    c = c_ref[...]
    s = c * jax.nn.sigmoid(c)
    o_ref[...] = jnp.dot(s, w_ref[...], preferred_element_type=F32,
                         precision=lax.Precision.HIGHEST) + b_ref[...]


def _ada_vectors(c_rows, ada_w, ada_b):
    depth, d, n = ada_w.shape
    rows = c_rows.shape[0]
    nb = 1536
    return pl.pallas_call(
        _mod_kernel,
        grid=(depth, n // nb),
        in_specs=[pl.BlockSpec((rows, d), lambda l, j: (0, 0)),
                  pl.BlockSpec((None, d, nb), lambda l, j: (l, 0, j)),
                  pl.BlockSpec((None, 1, nb), lambda l, j: (l, 0, j))],
        out_specs=pl.BlockSpec((None, rows, nb), lambda l, j: (l, 0, j)),
        out_shape=jax.ShapeDtypeStruct((depth, rows, n), F32),
        compiler_params=_cparams(("parallel", "parallel")),
        name="ada_vectors",
    )(c_rows, ada_w, ada_b.reshape(depth, 1, n))


def _in_even_kernel(h_ref, m_ref, w_ref, gg_ref, xr_ref, gu_ref, vn_ref, *, ctx_len):
    tm = h_ref.shape[0]
    is_ctx = pl.program_id(1) * tm < ctx_len
    sh = _mod_rows(m_ref, is_ctx, 0)
    sc = _mod_rows(m_ref, is_ctx, 1)
    a = (h_ref[...] * (1.0 + sc) + sh).astype(BF16)
    w = RNN_WIDTH
    gate = jnp.dot(a, w_ref[:, 0:w], preferred_element_type=F32)
    gg_ref[...] = _gelu(gate).astype(BF16)
    xr_ref[...] = jnp.dot(a, w_ref[:, w:2 * w], preferred_element_type=F32)
    u = jnp.dot(a, w_ref[:, 2 * w:2 * w + CMLP_WIDTH], preferred_element_type=F32)
    gu_ref[...] = _gelu(u).astype(BF16)
    v = jnp.dot(a, w_ref[:, 2 * w + CMLP_WIDTH:], preferred_element_type=F32)
    vn_ref[...] = _normalise(_gelu(v)).astype(BF16)


def _in_even(h, modv, w_in, ctx_len):
    b, s, d = h.shape
    tm = ROW_TILE
    n_in = w_in.shape[1]
    row = lambda width: pl.BlockSpec((None, tm, width), lambda bi, i: (bi, i, 0))
    return pl.pallas_call(
        functools.partial(_in_even_kernel, ctx_len=ctx_len),
        grid=(b, s // tm),
        in_specs=[row(d),
                  pl.BlockSpec((None, 16, d), lambda bi, i: (bi, 0, 0)),
                  _const_spec((d, n_in))],
        out_specs=[row(RNN_WIDTH), row(RNN_WIDTH), row(CMLP_WIDTH), row(CMLP_WIDTH)],
        out_shape=[jax.ShapeDtypeStruct((b, s, RNN_WIDTH), BF16),
                   jax.ShapeDtypeStruct((b, s, RNN_WIDTH), F32),
                   jax.ShapeDtypeStruct((b, s, CMLP_WIDTH), BF16),
                   jax.ShapeDtypeStruct((b, s, CMLP_WIDTH), BF16)],
        compiler_params=_cparams(("parallel", "parallel")),
        name="even_in_proj",
    )(h, modv, w_in)


def _block_scan(a_ref, b_ref, carry, reverse):
    ntile = a_ref.shape[0] // 8
    sub = lax.broadcasted_iota(jnp.int32, (8, a_ref.shape[1]), 0)
    out = [None] * ntile
    for k in (range(ntile - 1, -1, -1) if reverse else range(ntile)):
        ak = a_ref[8 * k:8 * k + 8, :]
        bk = b_ref[8 * k:8 * k + 8, :]
        for d in (1, 2, 4):
            if reverse:
                a_s = pltpu.roll(ak, 8 - d, 0)
                b_s = pltpu.roll(bk, 8 - d, 0)
                ok = sub < 8 - d
            else:
                a_s = pltpu.roll(ak, d, 0)
                b_s = pltpu.roll(bk, d, 0)
                ok = sub >= d
            bk = jnp.where(ok, bk + ak * b_s, bk)
            ak = jnp.where(ok, ak * a_s, ak)
        out[k] = bk + ak * carry
        last = 0 if reverse else 7
        carry = (jnp.broadcast_to(bk[last:last + 1, :], bk.shape)
                 + jnp.broadcast_to(ak[last:last + 1, :], ak.shape) * carry)
    return jnp.concatenate(out, axis=0), carry


def _scan_kernel(xr_ref, gg_ref, cw_ref, cb_ref, gw_ref, gb_ref, lam_ref, out_ref, xc_ref, rf_ref, rr_ref,
                 ab0_ref, ab1_ref, g0_ref, g1_ref, *, ctx_len, tb):
    s = xr_ref.shape[0]
    nblk = s // tb
    ncb = ctx_len // tb
    cw = cw_ref[...]
    cb = cb_ref[...]
    row = lax.broadcasted_iota(jnp.int32, (tb, LANES), 0)

    def conv(blk, _):
        t0 = pl.multiple_of(blk * tb, tb)
        x = xr_ref[pl.ds(t0, tb), :]
        prev = xr_ref[pl.ds(pl.multiple_of(jnp.maximum(t0 - 8, 0), 8), 8), :]
        nxt = xr_ref[pl.ds(pl.multiple_of(jnp.minimum(t0 + tb, s - 8), 8), 8), :]
        pf = jnp.where((blk == 0) | (blk == ncb), 0.0, 1.0)
        nf = jnp.where((blk == ncb - 1) | (blk == nblk - 1), 0.0, 1.0)
        p6 = prev[6:7, :] * pf
        p7 = prev[7:8, :] * pf
        n0 = nxt[0:1, :] * nf
        xm1 = jnp.where(row == 0, p7, pltpu.roll(x, 1, 0))
        xm2 = jnp.where(row == 0, p6, jnp.where(row == 1, p7, pltpu.roll(x, 2, 0)))
        xp1 = jnp.where(row == tb - 1, n0, pltpu.roll(x, tb - 1, 0))
        xc_ref[pl.ds(t0, tb), :] = (xm2 * cw[0:1, :] + xm1 * cw[1:2, :] + x * cw[2:3, :] + xp1 * cw[3:4, :]
                                    + cb)
        return 0

    lax.fori_loop(0, nblk, conv, 0)

    def block_start(step, d):
        blk = step if d == 0 else jnp.where(step < ncb, ncb - 1 - step, nblk - 1 - (step - ncb))
        return pl.multiple_of(blk * tb, tb)

    def gate_matmuls(step, g_ref):
        step = jnp.minimum(step, nblk - 1)
        for d in range(2):
            xc = xc_ref[pl.ds(block_start(step, d), tb), :]
            g_ref[d] = jnp.dot(xc.astype(BF16), gw_ref[d], preferred_element_type=F32)

    def stash_coeffs(step, g_ref, ab_ref):
        step = jnp.minimum(step, nblk - 1)
        for d in range(2):
            xc = xc_ref[pl.ds(block_start(step, d), tb), :]
            g = g_ref[d] + gb_ref[d]
            r = jax.nn.sigmoid(g[:, :LANES])
            ig = jax.nn.sigmoid(g[:, LANES:])
            z = -lam_ref[d:d + 1, :]
            softplus = jnp.maximum(z, 0.0) + jnp.log(1.0 + jnp.exp(-jnp.abs(z)))
            a = jnp.exp((-LRU_C) * r * softplus)
            ab_ref[2 * d] = a
            ab_ref[2 * d + 1] = jnp.sqrt(1.0 - a * a) * (ig * xc)

    def scan(step, ab_ref, cf, cr):
        hf, cf = _block_scan(ab_ref.at[0], ab_ref.at[1], cf, False)
        rf_ref[pl.ds(block_start(step, 0), tb), :] = hf
        hr, cr = _block_scan(ab_ref.at[2], ab_ref.at[3], cr, True)
        rr_ref[pl.ds(block_start(step, 1), tb), :] = hr
        return cf, cr

    gate_matmuls(0, g0_ref)
    stash_coeffs(0, g0_ref, ab0_ref)
    gate_matmuls(1, g1_ref)
    gate_matmuls(2, g0_ref)

    def two_steps(u, carry):
        cf, cr = carry
        cf, cr = scan(2 * u, ab0_ref, cf, cr)
        stash_coeffs(2 * u + 1, g1_ref, ab1_ref)
        cf, cr = scan(2 * u + 1, ab1_ref, cf, cr)
        stash_coeffs(2 * u + 2, g0_ref, ab0_ref)
        gate_matmuls(2 * u + 3, g1_ref)
        gate_matmuls(2 * u + 4, g0_ref)
        return cf, cr

    zero = jnp.zeros((8, LANES), F32)
    lax.fori_loop(0, nblk // 2, two_steps, (zero, zero))

    def combine(j, _):
        t0 = pl.multiple_of(j * tb, tb)
        rec = rf_ref[pl.ds(t0, tb), :] + rr_ref[pl.ds(t0, tb), :]
        out_ref[pl.ds(t0, tb), :] = (gg_ref[pl.ds(t0, tb), :].astype(F32) * rec).astype(BF16)
        return 0

    lax.fori_loop(0, nblk, combine, 0)


def _rglru(xr, gg, conv_w, conv_b, gate_w, gate_b, lam, ctx_len):
    b, s, w = xr.shape
    nh = w // LANES
    assert (s // SCAN_BLOCK) % 2 == 0 and ctx_len % SCAN_BLOCK == 0
    col = lambda dt: pl.BlockSpec((None, s, LANES), lambda bi, hd: (bi, 0, hd))
    return pl.pallas_call(
        functools.partial(_scan_kernel, ctx_len=ctx_len, tb=SCAN_BLOCK),
        grid=(b, nh),
        in_specs=[col(F32), col(BF16),
                  pl.BlockSpec((CONV_W, LANES), lambda bi, hd: (0, hd)),
                  pl.BlockSpec((1, LANES), lambda bi, hd: (0, hd)),
                  pl.BlockSpec((None, 2, LANES, 2 * LANES), lambda bi, hd: (hd, 0, 0, 0)),
                  pl.BlockSpec((None, 2, 1, 2 * LANES), lambda bi, hd: (hd, 0, 0, 0)),
                  pl.BlockSpec((2, LANES), lambda bi, hd: (0, hd))],
        out_specs=col(BF16),
        out_shape=jax.ShapeDtypeStruct((b, s, w), BF16),
        scratch_shapes=[pltpu.VMEM((s, LANES), F32), pltpu.VMEM((s, LANES), F32), pltpu.VMEM((s, LANES), F32),
                        pltpu.VMEM((4, SCAN_BLOCK, LANES), F32), pltpu.VMEM((4, SCAN_BLOCK, LANES), F32),
                        pltpu.VMEM((2, SCAN_BLOCK, 2 * LANES), F32), pltpu.VMEM((2, SCAN_BLOCK, 2 * LANES), F32)],
        compiler_params=_cparams(("parallel", "parallel")),
        name="rglru_scan",
    )(xr, gg, conv_w, conv_b, gate_w, gate_b, lam)


def _residual_ln(h, y, gate, g, b):
    return _normalise(ALPHA * h + gate * y) * g + b


def _swiglu(a, wi_ref, wo_ref):
    acc = None
    c0 = 0
    for width in FFN_CHUNKS:
        zg = jnp.dot(a, wi_ref[:, c0:c0 + width], preferred_element_type=F32)
        zu = jnp.dot(a, wi_ref[:, FFN_HIDDEN + c0:FFN_HIDDEN + c0 + width], preferred_element_type=F32)
        hm = (zg * jax.nn.sigmoid(zg) * zu).astype(BF16)
        y = jnp.dot(hm, wo_ref[c0:c0 + width, :], preferred_element_type=F32)
        acc = y if acc is None else acc + y
        c0 += width
    return acc


def _residual_pair(h, y_mix, m_ref, is_ctx, l1g_ref, l1b_ref, wi_ref, wo_ref, l2g_ref, l2b_ref):
    h1 = _residual_ln(h, y_mix, _mod_rows(m_ref, is_ctx, 2), l1g_ref[...], l1b_ref[...])
    a = (h1 * (1.0 + _mod_rows(m_ref, is_ctx, 4)) + _mod_rows(m_ref, is_ctx, 3)).astype(BF16)
    return _residual_ln(h1, _swiglu(a, wi_ref, wo_ref), _mod_rows(m_ref, is_ctx, 5), l2g_ref[...], l2b_ref[...])


def _sub_tiles(tm):
    return [slice(r * ROW_TILE, (r + 1) * ROW_TILE) for r in range(tm // ROW_TILE)]


def _post_tile(n_rows, first_tile):
    return 3 * ROW_TILE if first_tile == 0 and n_rows % (3 * ROW_TILE) == 0 else ROW_TILE


def _post_even_kernel(h_ref, m_ref, mr_ref, gu_ref, vn_ref, ws_ref, bs_ref, wm_ref, l1g_ref, l1b_ref,
                      wi_ref, wo_ref, l2g_ref, l2b_ref, o_ref, gm_ref, *, ctx_len):
    tm = h_ref.shape[0]
    gw = CMLP_WIDTH // CMLP_GROUPS
    for rows in _sub_tiles(tm):
        is_ctx = pl.program_id(1) * tm + rows.start < ctx_len
        for c in range(rows.start, rows.stop, CHUNK):
            crow = slice(c, c + CHUNK)
            for g in range(CMLP_GROUPS):
                cols = slice(g * gw, (g + 1) * gw)
                mixed = jnp.dot(ws_ref[g], vn_ref[crow, cols], preferred_element_type=F32) + bs_ref[g]
                gm_ref[crow, cols] = (gu_ref[crow, cols].astype(F32) * mixed).astype(BF16)
        y = jnp.dot(mr_ref[rows, :], wm_ref[0:RNN_WIDTH, :], preferred_element_type=F32)
        y = y + jnp.dot(gm_ref[rows, :], wm_ref[RNN_WIDTH:, :], preferred_element_type=F32)
        o_ref[rows, :] = _residual_pair(h_ref[rows, :], y, m_ref, is_ctx, l1g_ref, l1b_ref,
                                        wi_ref, wo_ref, l2g_ref, l2b_ref)


def _post_even(h, modv, mr, gu, vn, ws, bsb, w_mix, l1g, l1b, w_in, w_out, l2g, l2b, ctx_len):
    b, s, d = h.shape
    tm = _post_tile(s, 0)
    row = lambda width: pl.BlockSpec((None, tm, width), lambda bi, i: (bi, i, 0))
    vec = _const_spec((1, d))
    return pl.pallas_call(
        functools.partial(_post_even_kernel, ctx_len=ctx_len),
        grid=(b, s // tm),
        in_specs=[row(d),
                  pl.BlockSpec((None, 16, d), lambda bi, i: (bi, 0, 0)),
                  row(RNN_WIDTH), row(CMLP_WIDTH), row(CMLP_WIDTH),
                  _const_spec(ws.shape), _const_spec(bsb.shape), _const_spec(w_mix.shape), vec, vec,
                  _const_spec(w_in.shape), _const_spec(w_out.shape), vec, vec],
        out_specs=row(d),
        out_shape=jax.ShapeDtypeStruct((b, s, d), F32),
        scratch_shapes=[pltpu.VMEM((tm, CMLP_WIDTH), BF16)],
        compiler_params=_cparams(("parallel", "parallel")),
        name="even_out_ffn",
    )(h, modv, mr, gu, vn, ws, bsb, w_mix, l1g, l1b, w_in, w_out, l2g, l2b)


def _post_odd_kernel(h_ref, m_ref, yc_ref, yd_ref, wm_ref, l1g_ref, l1b_ref, wi_ref, wo_ref, l2g_ref, l2b_ref,
                     o_ref, *, ctx_len, first_tile):
    tm = h_ref.shape[0]
    half = yc_ref.shape[1]
    for rows in _sub_tiles(tm):
        is_ctx = pl.program_id(1) * tm + first_tile * ROW_TILE + rows.start < ctx_len
        y = jnp.dot(yc_ref[rows, :], wm_ref[0:half, :], preferred_element_type=F32)
        y = y + jnp.dot(yd_ref[rows, :], wm_ref[half:, :], preferred_element_type=F32)
        o_ref[rows, :] = _residual_pair(h_ref[rows, :], y, m_ref, is_ctx, l1g_ref, l1b_ref,
                                        wi_ref, wo_ref, l2g_ref, l2b_ref)


def _post_odd(h, modv, yc, yd, w_mix, l1g, l1b, w_in, w_out, l2g, l2b, ctx_len, first_tile):
    b, s, d = h.shape
    tm = _post_tile(s, first_tile)
    n = (s - first_tile * ROW_TILE) // tm
    assert yc.shape[1] == n * tm and yd.shape[1] == n * tm
    row = lambda width: pl.BlockSpec((None, tm, width), lambda bi, i: (bi, i, 0))
    vec = _const_spec((1, d))
    return pl.pallas_call(
        functools.partial(_post_odd_kernel, ctx_len=ctx_len, first_tile=first_tile),
        grid=(b, n),
        in_specs=[pl.BlockSpec((None, tm, d), lambda bi, i: (bi, i + first_tile, 0)),
                  pl.BlockSpec((None, 16, d), lambda bi, i: (bi, 0, 0)),
                  row(yc.shape[2]), row(yd.shape[2]),
                  _const_spec(w_mix.shape), vec, vec,
                  _const_spec(w_in.shape), _const_spec(w_out.shape), vec, vec],
        out_specs=row(d),
        out_shape=jax.ShapeDtypeStruct((b, n * tm, d), F32),
        compiler_params=_cparams(("parallel", "parallel")),
        name="odd_out_ffn",
    )(h, modv, yc, yd, w_mix, l1g, l1b, w_in, w_out, l2g, l2b)


def _in_odd_kernel(h_ref, m_ref, w_ref, bd_ref, gq_ref, gk_ref, cos_ref, sin_ref,
                   qc_ref, qd_ref, kc_ref, vc_ref, kd_ref, vd_ref, *, ctx_len):
    tm = h_ref.shape[0]
    is_ctx = pl.program_id(1) * tm < ctx_len
    sh = _mod_rows(m_ref, is_ctx, 0)
    sc = _mod_rows(m_ref, is_ctx, 1)
    a = (h_ref[...] * (1.0 + sc) + sh).astype(BF16)
    cos = cos_ref[...]
    sin = sin_ref[...]
    lane = lax.broadcasted_iota(jnp.int32, (tm, LANES), 1)
    first = (lane & 31) < 16
    scale = HEAD_DIM ** -0.5

    def rope(x):
        partner = jnp.where(first, pltpu.roll(x, LANES - 16, 1), pltpu.roll(x, 16, 1))
        return x * cos + partner * sin

    def rms(x, gain, width):
        ms = jnp.dot((x * x).astype(BF16), bd_ref[0:width, 0:width], preferred_element_type=F32)
        return x * lax.rsqrt(ms + EPS) * gain

    nq = Q_HEADS * HEAD_DIM
    cq = rms(jnp.dot(a, w_ref[:, 0:nq], preferred_element_type=F32), gq_ref[...], nq)
    for m in range(nq // LANES):
        qc_ref[m * LANES:(m + 1) * LANES, :] = (rope(cq[:, m * LANES:(m + 1) * LANES]) * (scale * LOG2E)).T.astype(BF16)
    dq = jnp.dot(a, w_ref[:, nq:2 * nq], preferred_element_type=F32)
    for m in range(nq // LANES):
        qd_ref[m * LANES:(m + 1) * LANES, :] = (rope(dq[:, m * LANES:(m + 1) * LANES]) * (scale * LOG2E)).T.astype(BF16)
    kv = jnp.dot(a, w_ref[:, 2 * nq:], preferred_element_type=F32)
    kc_ref[...] = rope(rms(kv[:, 0:LANES], gk_ref[...], LANES)).astype(BF16)
    kd_ref[...] = rope(kv[:, 2 * LANES:3 * LANES]).astype(BF16)
    rowi = lax.broadcasted_iota(jnp.int32, (LANES, tm), 0)
    pad = jnp.where(rowi == HEAD_DIM, 1.0, 0.0)

    def value_rows(v):
        vt = v.T
        return (jnp.where(rowi < HEAD_DIM, vt, pad).astype(BF16),
                jnp.where(rowi < HEAD_DIM, jnp.concatenate([vt[HEAD_DIM:], vt[:HEAD_DIM]], axis=0), pad).astype(BF16))

    vc_ref[0], vc_ref[1] = value_rows(kv[:, LANES:2 * LANES])
    vd = value_rows(kv[:, 3 * LANES:4 * LANES])
    for jj in range(KV_HEADS):
        for c in range(tm // LANES):
            vd_ref[jj, c] = vd[jj][:, c * LANES:(c + 1) * LANES]


def _in_odd(h, modv, w_in, bd, gq, gk, cos_t, sin_t, ctx_len):
    b, s, d = h.shape
    tm = ROW_TILE
    nq = Q_HEADS * HEAD_DIM
    row = lambda width: pl.BlockSpec((None, tm, width), lambda bi, i: (bi, i, 0))
    tab = pl.BlockSpec((tm, LANES), lambda bi, i: (i, 0))
    qt = pl.BlockSpec((None, nq, tm), lambda bi, i: (bi, 0, i))
    sds = lambda width: jax.ShapeDtypeStruct((b, s, width), BF16)
    return pl.pallas_call(
        functools.partial(_in_odd_kernel, ctx_len=ctx_len),
        grid=(b, s // tm),
        in_specs=[row(d),
                  pl.BlockSpec((None, 16, d), lambda bi, i: (bi, 0, 0)),
                  _const_spec(w_in.shape), _const_spec(bd.shape),
                  _const_spec(gq.shape), _const_spec(gk.shape), tab, tab],
        out_specs=[qt, qt, row(LANES),
                   pl.BlockSpec((None, KV_HEADS, None, LANES, tm), lambda bi, i: (bi, 0, i, 0, 0)),
                   row(LANES),
                   pl.BlockSpec((None, KV_HEADS, tm // LANES, LANES, LANES), lambda bi, i: (bi, 0, i, 0, 0))],
        out_shape=[jax.ShapeDtypeStruct((b, nq, s), BF16), jax.ShapeDtypeStruct((b, nq, s), BF16), sds(LANES),
                   jax.ShapeDtypeStruct((b, KV_HEADS, s // tm, LANES, tm), BF16), sds(LANES),
                   jax.ShapeDtypeStruct((b, KV_HEADS, s // LANES, LANES, LANES), BF16)],
        compiler_params=_cparams(("parallel", "parallel")),
        name="odd_in_proj",
    )(h, modv, w_in, bd, gq, gk, cos_t, sin_t)


def _stack_queries(q_ref, qs_ref, j, tq):
    for hh in range(GROUP):
        qh = q_ref[hh * HEAD_DIM:(hh + 1) * HEAD_DIM, :]
        zero = jnp.zeros_like(qh)
        qs_ref[0:HEAD_DIM, hh * tq:(hh + 1) * tq] = jnp.where(j == 0, qh, zero)
        qs_ref[HEAD_DIM:, hh * tq:(hh + 1) * tq] = jnp.where(j == 0, zero, qh)


def _store_heads(o, o_ref, tq):
    for m in range(GROUP // 2):
        pair = jnp.concatenate([o[:, (2 * m) * tq:(2 * m + 1) * tq],
                                o[:, (2 * m + 1) * tq:(2 * m + 2) * tq]], axis=0)
        o_ref[:, m * LANES:(m + 1) * LANES] = pair.T.astype(o_ref.dtype)


def _gattn_kernel(q_ref, k_ref, v_ref, o_ref, qs_ref, m_ref, acc_ref, sa_ref, sb_ref, *, ctx_len, first_tile):
    tq = q_ref.shape[1]
    ntile, _, tk = v_ref.shape
    j = pl.program_id(1)
    qi = pl.program_id(2) + first_tile
    _stack_queries(q_ref, qs_ref, j, tq)
    m_ref[...] = jnp.full(m_ref.shape, NEG_INF, F32)
    acc_ref[...] = jnp.zeros(acc_ref.shape, F32)
    nkv = jnp.where(qi * tq < ctx_len, ctx_len // tk, ntile)

    def scores(t, s_ref):
        start = pl.multiple_of(t * tk, tk)
        s_ref[...] = jnp.dot(k_ref[pl.ds(start, tk), :], qs_ref[...], preferred_element_type=F32)

    def update(t, s_ref):
        sc = s_ref[...]
        m_prev = m_ref[...]
        m_new = jnp.maximum(m_prev, jnp.max(sc, axis=0, keepdims=True))
        alpha = jnp.exp2(m_prev - m_new)
        p = jnp.exp2(sc - m_new)
        acc_ref[...] = alpha * acc_ref[...] + jnp.dot(v_ref[t], p.astype(BF16), preferred_element_type=F32)
        m_ref[...] = m_new

    scores(0, sa_ref)

    def quad(u, _):
        t = 4 * u
        scores(t + 1, sb_ref)
        update(t, sa_ref)
        scores(t + 2, sa_ref)
        update(t + 1, sb_ref)
        scores(t + 3, sb_ref)
        update(t + 2, sa_ref)
        scores(t + 4, sa_ref)
        update(t + 3, sb_ref)
        return 0

    lax.fori_loop(0, nkv // 4, quad, 0)
    update(nkv - 1, sa_ref)
    _store_heads(acc_ref[0:HEAD_DIM, :] / acc_ref[HEAD_DIM:HEAD_DIM + 1, :], o_ref, tq)


def _global_attention(qt, k, vt, ctx_len, first_tile):
    b, nq, s = qt.shape
    tq = ATT_TQ
    gw = GROUP * HEAD_DIM
    assert vt.shape[4] == ATT_TK
    assert (s // ATT_TK) % 4 == 1 and (ctx_len // ATT_TK) % 4 == 1
    return pl.pallas_call(
        functools.partial(_gattn_kernel, ctx_len=ctx_len, first_tile=first_tile),
        grid=(b, KV_HEADS, s // tq - first_tile),
        in_specs=[pl.BlockSpec((None, gw, tq), lambda bi, j, i: (bi, j, i + first_tile)),
                  pl.BlockSpec((None, s, LANES), lambda bi, j, i: (bi, 0, 0)),
                  pl.BlockSpec((None, None) + vt.shape[2:], lambda bi, j, i: (bi, j, 0, 0, 0))],
        out_specs=pl.BlockSpec((None, tq, gw), lambda bi, j, i: (bi, i, j)),
        out_shape=jax.ShapeDtypeStruct((b, s - first_tile * tq, nq), BF16),
        scratch_shapes=[pltpu.VMEM((LANES, GROUP * tq), BF16),
                        pltpu.VMEM((1, GROUP * tq), F32),
                        pltpu.VMEM((LANES, GROUP * tq), F32),
                        pltpu.VMEM((ATT_TK, GROUP * tq), F32),
                        pltpu.VMEM((ATT_TK, GROUP * tq), F32)],
        compiler_params=_cparams(("parallel", "parallel", "arbitrary")),
        name="global_attention",
    )(qt, k, vt)


def _wattn_kernel(q_ref, k_ref, v_ref, sink_ref, bias_ref, o_ref, qs_ref, *, ctx_len, span, first_tile):
    tq = q_ref.shape[1]
    s = k_ref.shape[0]
    j = pl.program_id(1)
    qi = pl.program_id(2) + first_tile
    _stack_queries(q_ref, qs_ref, j, tq)
    qs = qs_ref[...]
    sink = sink_ref[...] * LOG2E
    s_ctx = jnp.dot(k_ref[0:ctx_len, :], qs, preferred_element_type=F32)
    v_ctx = jnp.concatenate([v_ref[t] for t in range(ctx_len // LANES)], axis=1)
    is_ctx = qi * tq < ctx_len

    def finish(acc, m):
        den = acc[HEAD_DIM:HEAD_DIM + 1, :] + jnp.exp2(sink - m)
        _store_heads(acc[0:HEAD_DIM, :] / den, o_ref, tq)

    @pl.when(is_ctx)
    def _():
        m = jnp.maximum(jnp.max(s_ctx, axis=0, keepdims=True), sink)
        p = jnp.exp2(s_ctx - m)
        finish(jnp.dot(v_ctx, p.astype(BF16), preferred_element_type=F32), m)

    @pl.when(jnp.logical_not(is_ctx))
    def _():
        start = pl.multiple_of(jnp.clip(qi * tq - WINDOW, ctx_len, s - span), LANES)
        s_win = jnp.dot(k_ref[pl.ds(start, span), :], qs, preferred_element_type=F32) + bias_ref[...]
        m = jnp.maximum(jnp.maximum(jnp.max(s_ctx, axis=0, keepdims=True),
                                    jnp.max(s_win, axis=0, keepdims=True)), sink)
        p_ctx = jnp.exp2(s_ctx - m)
        p_win = jnp.exp2(s_win - m)
        t0 = start // LANES
        v_win = jnp.concatenate([v_ref[t0 + i] for i in range(span // LANES)], axis=1)
        acc = jnp.dot(v_ctx, p_ctx.astype(BF16), preferred_element_type=F32)
        finish(acc + jnp.dot(v_win, p_win.astype(BF16), preferred_element_type=F32), m)


def _window_attention(qt, k, vt, sink, ctx_len, first_tile):
    b, nq, s = qt.shape
    tq = ATT_TQ
    gw = GROUP * HEAD_DIM
    span = tq + 2 * WINDOW
    assert tq >= WINDOW and (s - ctx_len) // tq >= 2
    sink_row = jnp.repeat(sink.reshape(KV_HEADS, 1, GROUP), tq, axis=2)
    r = jnp.arange(span)[:, None]
    qcol = (jnp.arange(GROUP * tq) % tq)[None, :]
    bias = jnp.stack([jnp.where(jnp.abs(off + r - qcol) <= WINDOW, 0.0, NEG_INF)
                      for off in (0, -WINDOW, tq - span)]).astype(F32)

    def placement(i):
        lo = (i + first_tile) * tq - WINDOW
        return jnp.where(lo < ctx_len, 0, jnp.where(lo > s - span, 2, 1))

    return pl.pallas_call(
        functools.partial(_wattn_kernel, ctx_len=ctx_len, span=span, first_tile=first_tile),
        grid=(b, KV_HEADS, s // tq - first_tile),
        in_specs=[pl.BlockSpec((None, gw, tq), lambda bi, j, i: (bi, j, i + first_tile)),
                  pl.BlockSpec((None, s, LANES), lambda bi, j, i: (bi, 0, 0)),
                  pl.BlockSpec((None, None) + vt.shape[2:], lambda bi, j, i: (bi, j, 0, 0, 0)),
                  pl.BlockSpec((None, 1, GROUP * tq), lambda bi, j, i: (j, 0, 0)),
                  pl.BlockSpec((None, span, GROUP * tq), lambda bi, j, i: (placement(i), 0, 0))],
        out_specs=pl.BlockSpec((None, tq, gw), lambda bi, j, i: (bi, i, j)),
        out_shape=jax.ShapeDtypeStruct((b, s - first_tile * tq, nq), BF16),
        scratch_shapes=[pltpu.VMEM((LANES, GROUP * tq), BF16)],
        compiler_params=_cparams(("parallel", "parallel", "arbitrary")),
        name="window_attention",
    )(qt, k, vt, sink_row, bias)


def _rope_tables(n_lat, ctx_len):
    rows = n_lat // GRID_W
    row = jnp.repeat(jnp.arange(rows, dtype=F32), GRID_W)
    col = jnp.tile(jnp.arange(GRID_W, dtype=F32), rows)
    nf = HEAD_DIM // 4
    inv_freq = ROPE_THETA ** (-jnp.arange(nf, dtype=F32) / nf)
    ang_r = row[:, None] * inv_freq
    ang_c = col[:, None] * inv_freq
    cos = jnp.concatenate([jnp.cos(ang_r)] * 2 + [jnp.cos(ang_c)] * 2, axis=1)
    sin = jnp.concatenate([-jnp.sin(ang_r), jnp.sin(ang_r), -jnp.sin(ang_c), jnp.sin(ang_c)], axis=1)
    cos = jnp.concatenate([jnp.ones((ctx_len, HEAD_DIM), F32), cos], axis=0)
    sin = jnp.concatenate([jnp.zeros((ctx_len, HEAD_DIM), F32), sin], axis=0)
    return jnp.tile(cos, (1, LANES // HEAD_DIM)), jnp.tile(sin, (1, LANES // HEAD_DIM))


def kernel(x, c, ctx, c_ctx, ada_w, ada_b, ln1_g, ln1_b, ln2_g, ln2_b, ffn_w_in, ffn_w_out,
           ev_w_in, ev_w_out, rg_conv_w, rg_conv_b, rg_gate_w, rg_gate_b, rg_lambda, cm_w_s, cm_b_s,
           od_w_in, od_w_out, qn_g, kn_g, sink):
    b, t, d = x.shape
    ctx_len = ctx.shape[1]
    s = ctx_len + t
    depth = ada_w.shape[0]
    assert d == D_MODEL and depth == DEPTH and b + 1 <= 16
    assert ctx_len % ROW_TILE == 0 and t % ROW_TILE == 0 and t % GRID_W == 0 and ATT_TQ == ROW_TILE

    c_rows = jnp.zeros((16, d), F32).at[:b].set(c).at[b].set(c_ctx)
    mods = _ada_vectors(c_rows, ada_w, ada_b).reshape(depth, 16, 6, d)
    modv = jnp.zeros((depth, b, 16, d), F32)
    modv = modv.at[:, :, 0:6].set(mods[:, :b])
    modv = modv.at[:, :, 8:14].set(jnp.broadcast_to(mods[:, b][:, None], (depth, b, 6, d)))

    cos_t, sin_t = _rope_tables(t, ctx_len)
    nq = Q_HEADS * HEAD_DIM
    bd = jnp.kron(jnp.eye(nq // HEAD_DIM, dtype=F32), jnp.full((HEAD_DIM, HEAD_DIM), 1.0 / HEAD_DIM, F32)).astype(BF16)
    assert sum(FFN_CHUNKS) == FFN_HIDDEN

    h = jnp.concatenate([ctx, x], axis=1)
    rows_ctx = ctx_len
    for l in range(depth):
        j = l // 2
        mv = modv[l]
        norms = (ln1_g[l].reshape(1, d), ln1_b[l].reshape(1, d), ffn_w_in[l].astype(BF16),
                 ffn_w_out[l].astype(BF16), ln2_g[l].reshape(1, d), ln2_b[l].reshape(1, d))
        if l % 2 == 0:
            gg, xr, gu, vn = _in_even(h, mv, ev_w_in[j].astype(BF16), ctx_len)
            gw = jnp.transpose(rg_gate_w[j], (2, 0, 3, 1, 4)).reshape(RNN_HEADS, 2, LANES, 2 * LANES).astype(BF16)
            gb = jnp.transpose(rg_gate_b[j].reshape(2, 2, RNN_HEADS, LANES), (2, 0, 1, 3)).reshape(RNN_HEADS, 2, 1, 2 * LANES)
            mr = _rglru(xr, gg, rg_conv_w[j], rg_conv_b[j].reshape(1, -1), gw, gb, rg_lambda[j], ctx_len)
            bsb = jnp.broadcast_to(cm_b_s[j][:, :, None], (CMLP_GROUPS, CHUNK, CMLP_WIDTH // CMLP_GROUPS))
            h = _post_even(h, mv, mr, gu, vn, cm_w_s[j].astype(BF16), bsb, ev_w_out[j].astype(BF16), *norms, ctx_len)
        else:
            gq = jnp.tile(qn_g[j], Q_HEADS).reshape(1, nq)
            gk = jnp.tile(kn_g[j], LANES // HEAD_DIM).reshape(1, LANES)
            qc, qd, kc, vc, kd, vd = _in_odd(h, mv, od_w_in[j].astype(BF16), bd, gq, gk, cos_t, sin_t, ctx_len)
            skip = ctx_len // ATT_TQ if l == depth - 1 else 0
            yc = _global_attention(qc, kc, vc, ctx_len, skip)
            yd = _window_attention(qd, kd, vd, sink[j], ctx_len, skip)
            h = _post_odd(h, mv, yc, yd, od_w_out[j].astype(BF16), *norms, ctx_len, skip)
            if skip:
                rows_ctx = 0
    return h[:, rows_ctx:, :]
```

```python
import functools

import numpy as np
import jax
import jax.numpy as jnp
from jax import lax
from jax.experimental import pallas as pl
from jax.experimental.pallas import tpu as pltpu

F32 = jnp.float32
BF16 = jnp.bfloat16

D_MODEL = 1024
DEPTH = 4
GRID_W = 64
RNN_WIDTH = D_MODEL
RNN_HEADS = RNN_WIDTH // 128
CONV_W = 4
LRU_C = 8.0
CMLP_WIDTH = D_MODEL // 2
CMLP_GROUPS = 4
CHUNK = 128
HEAD_DIM = 64
Q_HEADS = 8
KV_HEADS = 2
GROUP = Q_HEADS // KV_HEADS
WINDOW = 128
ROPE_THETA = 10000.0
NEG_INF = -1e30
FFN_HIDDEN = 2816
FFN_CHUNKS = (1024, 1024, 768)
ALPHA = (2.0 * DEPTH) ** 0.25
LOG2E = 1.4426950408889634
EPS = 1e-6

LANES = 128
ROW_TILE = 256
SCAN_BLOCK = 128
ATT_TQ = 256
ATT_TK = 256
ATT_BIG = 1024
VMEM_LIMIT = 56 * 1024 * 1024


def _cparams(sem):
    return pltpu.CompilerParams(dimension_semantics=sem, vmem_limit_bytes=VMEM_LIMIT)


def _const_spec(shape):
    nd = len(shape)
    return pl.BlockSpec(shape, lambda *_: (0,) * nd, pipeline_mode=pl.Buffered(1))


def _gelu(x):
    return 0.5 * x * (1.0 + jnp.tanh(0.7978845608028654 * (x + 0.044715 * (x * x * x))))


def _normalise(x):
    mu = jnp.mean(x, axis=-1, keepdims=True)
    xc = x - mu
    var = jnp.mean(xc * xc, axis=-1, keepdims=True)
    return xc * lax.rsqrt(var + EPS)


def _mod_rows(m_ref, is_ctx, lat_row):
    return jnp.where(is_ctx, m_ref[lat_row + 8:lat_row + 9, :], m_ref[lat_row:lat_row + 1, :])


def _mod_kernel(c_ref, w_ref, b_ref, o_ref):
    c = c_ref[...]
    s = c * jax.nn.sigmoid(c)
    o_ref[...] = jnp.dot(s, w_ref[...], preferred_element_type=F32,
                         precision=lax.Precision.HIGHEST) + b_ref[...]


def _ada_vectors(c_rows, ada_w, ada_b):
    depth, d, n = ada_w.shape
    rows = c_rows.shape[0]
    nb = 1536
    return pl.pallas_call(
        _mod_kernel,
        grid=(depth, n // nb),
        in_specs=[pl.BlockSpec((rows, d), lambda l, j: (0, 0)),
                  pl.BlockSpec((None, d, nb), lambda l, j: (l, 0, j)),
                  pl.BlockSpec((None, 1, nb), lambda l, j: (l, 0, j))],
        out_specs=pl.BlockSpec((None, rows, nb), lambda l, j: (l, 0, j)),
        out_shape=jax.ShapeDtypeStruct((depth, rows, n), F32),
        compiler_params=_cparams(("parallel", "parallel")),
        name="ada_vectors",
    )(c_rows, ada_w, ada_b.reshape(depth, 1, n))


def _in_even_kernel(h_ref, m_ref, w_ref, gg_ref, xr_ref, gu_ref, vn_ref, *, ctx_len):
    tm = h_ref.shape[0]
    is_ctx = pl.program_id(1) * tm < ctx_len
    sh = _mod_rows(m_ref, is_ctx, 0)
    sc = _mod_rows(m_ref, is_ctx, 1)
    a = (h_ref[...] * (1.0 + sc) + sh).astype(BF16)
    w = RNN_WIDTH
    gate = jnp.dot(a, w_ref[:, 0:w], preferred_element_type=F32)
    gg_ref[...] = _gelu(gate).astype(BF16)
    xr_ref[...] = jnp.dot(a, w_ref[:, w:2 * w], preferred_element_type=F32)
    u = jnp.dot(a, w_ref[:, 2 * w:2 * w + CMLP_WIDTH], preferred_element_type=F32)
    gu_ref[...] = _gelu(u).astype(BF16)
    v = jnp.dot(a, w_ref[:, 2 * w + CMLP_WIDTH:], preferred_element_type=F32)
    vn_ref[...] = _normalise(_gelu(v)).astype(BF16)


def _in_even(h, modv, w_in, ctx_len):
    b, s, d = h.shape
    tm = ROW_TILE
    n_in = w_in.shape[1]
    row = lambda width: pl.BlockSpec((None, tm, width), lambda bi, i: (bi, i, 0))
    return pl.pallas_call(
        functools.partial(_in_even_kernel, ctx_len=ctx_len),
        grid=(b, s // tm),
        in_specs=[row(d),
                  pl.BlockSpec((None, 16, d), lambda bi, i: (bi, 0, 0)),
                  _const_spec((d, n_in))],
        out_specs=[row(RNN_WIDTH), row(RNN_WIDTH), row(CMLP_WIDTH), row(CMLP_WIDTH)],
        out_shape=[jax.ShapeDtypeStruct((b, s, RNN_WIDTH), BF16),
                   jax.ShapeDtypeStruct((b, s, RNN_WIDTH), F32),
                   jax.ShapeDtypeStruct((b, s, CMLP_WIDTH), BF16),
                   jax.ShapeDtypeStruct((b, s, CMLP_WIDTH), BF16)],
        compiler_params=_cparams(("parallel", "parallel")),
        name="even_in_proj",
    )(h, modv, w_in)


def _block_scan(a_ref, b_ref, carry, reverse):
    ntile = a_ref.shape[0] // 8
    sub = lax.broadcasted_iota(jnp.int32, (8, a_ref.shape[1]), 0)
    out = [None] * ntile
    for k in (range(ntile - 1, -1, -1) if reverse else range(ntile)):
        ak = a_ref[8 * k:8 * k + 8, :]
        bk = b_ref[8 * k:8 * k + 8, :]
        for d in (1, 2, 4):
            if reverse:
                a_s = pltpu.roll(ak, 8 - d, 0)
                b_s = pltpu.roll(bk, 8 - d, 0)
                ok = sub < 8 - d
            else:
                a_s = pltpu.roll(ak, d, 0)
                b_s = pltpu.roll(bk, d, 0)
                ok = sub >= d
            bk = jnp.where(ok, bk + ak * b_s, bk)
            ak = jnp.where(ok, ak * a_s, ak)
        out[k] = bk + ak * carry
        last = 0 if reverse else 7
        carry = (jnp.broadcast_to(bk[last:last + 1, :], bk.shape)
                 + jnp.broadcast_to(ak[last:last + 1, :], ak.shape) * carry)
    return jnp.concatenate(out, axis=0), carry


def _scan_kernel(xr_ref, gg_ref, cw_ref, cb_ref, gw_ref, gb_ref, lam_ref, out_ref, xc_ref, rf_ref, rr_ref,
                 ab0_ref, ab1_ref, g0_ref, g1_ref, *, ctx_len, tb):
    s = xr_ref.shape[0]
    nblk = s // tb
    ncb = ctx_len // tb
    cw = cw_ref[...]
    cb = cb_ref[...]
    row = lax.broadcasted_iota(jnp.int32, (tb, LANES), 0)

    def conv(blk, _):
        t0 = pl.multiple_of(blk * tb, tb)
        x = xr_ref[pl.ds(t0, tb), :]
        prev = xr_ref[pl.ds(pl.multiple_of(jnp.maximum(t0 - 8, 0), 8), 8), :]
        nxt = xr_ref[pl.ds(pl.multiple_of(jnp.minimum(t0 + tb, s - 8), 8), 8), :]
        pf = jnp.where((blk == 0) | (blk == ncb), 0.0, 1.0)
        nf = jnp.where((blk == ncb - 1) | (blk == nblk - 1), 0.0, 1.0)
        p6 = prev[6:7, :] * pf
        p7 = prev[7:8, :] * pf
        n0 = nxt[0:1, :] * nf
        xm1 = jnp.where(row == 0, p7, pltpu.roll(x, 1, 0))
        xm2 = jnp.where(row == 0, p6, jnp.where(row == 1, p7, pltpu.roll(x, 2, 0)))
        xp1 = jnp.where(row == tb - 1, n0, pltpu.roll(x, tb - 1, 0))
        xc_ref[pl.ds(t0, tb), :] = (xm2 * cw[0:1, :] + xm1 * cw[1:2, :] + x * cw[2:3, :] + xp1 * cw[3:4, :]
                                    + cb)
        return 0

    lax.fori_loop(0, nblk, conv, 0)

    def block_start(step, d):
        blk = step if d == 0 else jnp.where(step < ncb, ncb - 1 - step, nblk - 1 - (step - ncb))
        return pl.multiple_of(blk * tb, tb)

    def gate_matmuls(step, g_ref):
        step = jnp.minimum(step, nblk - 1)
        for d in range(2):
            xc = xc_ref[pl.ds(block_start(step, d), tb), :]
            g_ref[d] = jnp.dot(xc.astype(BF16), gw_ref[d], preferred_element_type=F32)

    def stash_coeffs(step, g_ref, ab_ref):
        step = jnp.minimum(step, nblk - 1)
        for d in range(2):
            xc = xc_ref[pl.ds(block_start(step, d), tb), :]
            g = g_ref[d] + gb_ref[d]
            r = jax.nn.sigmoid(g[:, :LANES])
            ig = jax.nn.sigmoid(g[:, LANES:])
            z = -lam_ref[d:d + 1, :]
            softplus = jnp.maximum(z, 0.0) + jnp.log(1.0 + jnp.exp(-jnp.abs(z)))
            a = jnp.exp((-LRU_C) * r * softplus)
            ab_ref[2 * d] = a
            ab_ref[2 * d + 1] = jnp.sqrt(1.0 - a * a) * (ig * xc)

    def scan(step, ab_ref, cf, cr):
        hf, cf = _block_scan(ab_ref.at[0], ab_ref.at[1], cf, False)
        rf_ref[pl.ds(block_start(step, 0), tb), :] = hf
        hr, cr = _block_scan(ab_ref.at[2], ab_ref.at[3], cr, True)
        rr_ref[pl.ds(block_start(step, 1), tb), :] = hr
        return cf, cr

    gate_matmuls(0, g0_ref)
    stash_coeffs(0, g0_ref, ab0_ref)
    gate_matmuls(1, g1_ref)
    gate_matmuls(2, g0_ref)

    def two_steps(u, carry):
        cf, cr = carry
        cf, cr = scan(2 * u, ab0_ref, cf, cr)
        stash_coeffs(2 * u + 1, g1_ref, ab1_ref)
        cf, cr = scan(2 * u + 1, ab1_ref, cf, cr)
        stash_coeffs(2 * u + 2, g0_ref, ab0_ref)
        gate_matmuls(2 * u + 3, g1_ref)
        gate_matmuls(2 * u + 4, g0_ref)
        return cf, cr

    zero = jnp.zeros((8, LANES), F32)
    lax.fori_loop(0, nblk // 2, two_steps, (zero, zero))

    def combine(j, _):
        t0 = pl.multiple_of(j * tb, tb)
        rec = rf_ref[pl.ds(t0, tb), :] + rr_ref[pl.ds(t0, tb), :]
        out_ref[pl.ds(t0, tb), :] = (gg_ref[pl.ds(t0, tb), :].astype(F32) * rec).astype(BF16)
        return 0

    lax.fori_loop(0, nblk, combine, 0)


def _rglru(xr, gg, conv_w, conv_b, gate_w, gate_b, lam, ctx_len):
    b, s, w = xr.shape
    nh = w // LANES
    assert (s // SCAN_BLOCK) % 2 == 0 and ctx_len % SCAN_BLOCK == 0
    col = lambda dt: pl.BlockSpec((None, s, LANES), lambda bi, hd: (bi, 0, hd))
    return pl.pallas_call(
        functools.partial(_scan_kernel, ctx_len=ctx_len, tb=SCAN_BLOCK),
        grid=(b, nh),
        in_specs=[col(F32), col(BF16),
                  pl.BlockSpec((CONV_W, LANES), lambda bi, hd: (0, hd)),
                  pl.BlockSpec((1, LANES), lambda bi, hd: (0, hd)),
                  pl.BlockSpec((None, 2, LANES, 2 * LANES), lambda bi, hd: (hd, 0, 0, 0)),
                  pl.BlockSpec((None, 2, 1, 2 * LANES), lambda bi, hd: (hd, 0, 0, 0)),
                  pl.BlockSpec((2, LANES), lambda bi, hd: (0, hd))],
        out_specs=col(BF16),
        out_shape=jax.ShapeDtypeStruct((b, s, w), BF16),
        scratch_shapes=[pltpu.VMEM((s, LANES), F32), pltpu.VMEM((s, LANES), F32), pltpu.VMEM((s, LANES), F32),
                        pltpu.VMEM((4, SCAN_BLOCK, LANES), F32), pltpu.VMEM((4, SCAN_BLOCK, LANES), F32),
                        pltpu.VMEM((2, SCAN_BLOCK, 2 * LANES), F32), pltpu.VMEM((2, SCAN_BLOCK, 2 * LANES), F32)],
        compiler_params=_cparams(("parallel", "parallel")),
        name="rglru_scan",
    )(xr, gg, conv_w, conv_b, gate_w, gate_b, lam)


def _residual_ln(h, y, gate, g, b):
    return _normalise(ALPHA * h + gate * y) * g + b


def _swiglu(a, wi_ref, wo_ref):
    acc = None
    c0 = 0
    for width in FFN_CHUNKS:
        zg = jnp.dot(a, wi_ref[:, c0:c0 + width], preferred_element_type=F32)
        zu = jnp.dot(a, wi_ref[:, FFN_HIDDEN + c0:FFN_HIDDEN + c0 + width], preferred_element_type=F32)
        hm = (zg * jax.nn.sigmoid(zg) * zu).astype(BF16)
        y = jnp.dot(hm, wo_ref[c0:c0 + width, :], preferred_element_type=F32)
        acc = y if acc is None else acc + y
        c0 += width
    return acc


def _residual_pair(h, y_mix, m_ref, is_ctx, l1g_ref, l1b_ref, wi_ref, wo_ref, l2g_ref, l2b_ref):
    h1 = _residual_ln(h, y_mix, _mod_rows(m_ref, is_ctx, 2), l1g_ref[...], l1b_ref[...])
    a = (h1 * (1.0 + _mod_rows(m_ref, is_ctx, 4)) + _mod_rows(m_ref, is_ctx, 3)).astype(BF16)
    return _residual_ln(h1, _swiglu(a, wi_ref, wo_ref), _mod_rows(m_ref, is_ctx, 5), l2g_ref[...], l2b_ref[...])


def _sub_tiles(tm):
    return [slice(r * ROW_TILE, (r + 1) * ROW_TILE) for r in range(tm // ROW_TILE)]


def _post_tile(n_rows, first_tile):
    return 3 * ROW_TILE if first_tile == 0 and n_rows % (3 * ROW_TILE) == 0 else ROW_TILE


def _post_even_kernel(h_ref, m_ref, mr_ref, gu_ref, vn_ref, ws_ref, bs_ref, wm_ref, l1g_ref, l1b_ref,
                      wi_ref, wo_ref, l2g_ref, l2b_ref, o_ref, gm_ref, *, ctx_len):
    tm = h_ref.shape[0]
    gw = CMLP_WIDTH // CMLP_GROUPS
    for rows in _sub_tiles(tm):
        is_ctx = pl.program_id(1) * tm + rows.start < ctx_len
        for c in range(rows.start, rows.stop, CHUNK):
            crow = slice(c, c + CHUNK)
            for g in range(CMLP_GROUPS):
                cols = slice(g * gw, (g + 1) * gw)
                mixed = jnp.dot(ws_ref[g], vn_ref[crow, cols], preferred_element_type=F32) + bs_ref[g]
                gm_ref[crow, cols] = (gu_ref[crow, cols].astype(F32) * mixed).astype(BF16)
        y = jnp.dot(mr_ref[rows, :], wm_ref[0:RNN_WIDTH, :], preferred_element_type=F32)
        y = y + jnp.dot(gm_ref[rows, :], wm_ref[RNN_WIDTH:, :], preferred_element_type=F32)
        o_ref[rows, :] = _residual_pair(h_ref[rows, :], y, m_ref, is_ctx, l1g_ref, l1b_ref,
                                        wi_ref, wo_ref, l2g_ref, l2b_ref)


def _post_even(h, modv, mr, gu, vn, ws, bsb, w_mix, l1g, l1b, w_in, w_out, l2g, l2b, ctx_len):
    b, s, d = h.shape
    tm = _post_tile(s, 0)
    row = lambda width: pl.BlockSpec((None, tm, width), lambda bi, i: (bi, i, 0))
    vec = _const_spec((1, d))
    return pl.pallas_call(
        functools.partial(_post_even_kernel, ctx_len=ctx_len),
        grid=(b, s // tm),
        in_specs=[row(d),
                  pl.BlockSpec((None, 16, d), lambda bi, i: (bi, 0, 0)),
                  row(RNN_WIDTH), row(CMLP_WIDTH), row(CMLP_WIDTH),
                  _const_spec(ws.shape), _const_spec(bsb.shape), _const_spec(w_mix.shape), vec, vec,
                  _const_spec(w_in.shape), _const_spec(w_out.shape), vec, vec],
        out_specs=row(d),
        out_shape=jax.ShapeDtypeStruct((b, s, d), F32),
        scratch_shapes=[pltpu.VMEM((tm, CMLP_WIDTH), BF16)],
        compiler_params=_cparams(("parallel", "parallel")),
        name="even_out_ffn",
    )(h, modv, mr, gu, vn, ws, bsb, w_mix, l1g, l1b, w_in, w_out, l2g, l2b)


def _post_odd_kernel(h_ref, m_ref, yc_ref, yd_ref, wm_ref, l1g_ref, l1b_ref, wi_ref, wo_ref, l2g_ref, l2b_ref,
                     o_ref, *, ctx_len, first_tile):
    tm = h_ref.shape[0]
    half = yc_ref.shape[1]
    for rows in _sub_tiles(tm):
        is_ctx = pl.program_id(1) * tm + first_tile * ROW_TILE + rows.start < ctx_len
        y = jnp.dot(yc_ref[rows, :], wm_ref[0:half, :], preferred_element_type=F32)
        y = y + jnp.dot(yd_ref[rows, :], wm_ref[half:, :], preferred_element_type=F32)
        o_ref[rows, :] = _residual_pair(h_ref[rows, :], y, m_ref, is_ctx, l1g_ref, l1b_ref,
                                        wi_ref, wo_ref, l2g_ref, l2b_ref)


def _post_odd(h, modv, yc, yd, w_mix, l1g, l1b, w_in, w_out, l2g, l2b, ctx_len, first_tile):
    b, s, d = h.shape
    tm = _post_tile(s, first_tile)
    n = (s - first_tile * ROW_TILE) // tm
    assert yc.shape[1] == n * tm and yd.shape[1] == n * tm
    row = lambda width: pl.BlockSpec((None, tm, width), lambda bi, i: (bi, i, 0))
    vec = _const_spec((1, d))
    return pl.pallas_call(
        functools.partial(_post_odd_kernel, ctx_len=ctx_len, first_tile=first_tile),
        grid=(b, n),
        in_specs=[pl.BlockSpec((None, tm, d), lambda bi, i: (bi, i + first_tile, 0)),
                  pl.BlockSpec((None, 16, d), lambda bi, i: (bi, 0, 0)),
                  row(yc.shape[2]), row(yd.shape[2]),
                  _const_spec(w_mix.shape), vec, vec,
                  _const_spec(w_in.shape), _const_spec(w_out.shape), vec, vec],
        out_specs=row(d),
        out_shape=jax.ShapeDtypeStruct((b, n * tm, d), F32),
        compiler_params=_cparams(("parallel", "parallel")),
        name="odd_out_ffn",
    )(h, modv, yc, yd, w_mix, l1g, l1b, w_in, w_out, l2g, l2b)


def _in_odd_kernel(h_ref, m_ref, w_ref, bd_ref, gq_ref, gk_ref, cos_ref, sin_ref,
                   qc_ref, qd_ref, kc_ref, vc_ref, kd_ref, vd_ref, *, ctx_len):
    tm = h_ref.shape[0]
    is_ctx = pl.program_id(1) * tm < ctx_len
    sh = _mod_rows(m_ref, is_ctx, 0)
    sc = _mod_rows(m_ref, is_ctx, 1)
    a = (h_ref[...] * (1.0 + sc) + sh).astype(BF16)
    cos = cos_ref[...]
    sin = sin_ref[...]
    lane = lax.broadcasted_iota(jnp.int32, (tm, LANES), 1)
    first = (lane & 31) < 16
    scale = HEAD_DIM ** -0.5

    def rope(x):
        partner = jnp.where(first, pltpu.roll(x, LANES - 16, 1), pltpu.roll(x, 16, 1))
        return x * cos + partner * sin

    def rms(x, gain, width):
        ms = jnp.dot((x * x).astype(BF16), bd_ref[0:width, 0:width], preferred_element_type=F32)
        return x * lax.rsqrt(ms + EPS) * gain

    nq = Q_HEADS * HEAD_DIM
    cq = rms(jnp.dot(a, w_ref[:, 0:nq], preferred_element_type=F32), gq_ref[...], nq)
    for m in range(nq // LANES):
        qc_ref[m * LANES:(m + 1) * LANES, :] = (rope(cq[:, m * LANES:(m + 1) * LANES]) * (scale * LOG2E)).T.astype(BF16)
    dq = jnp.dot(a, w_ref[:, nq:2 * nq], preferred_element_type=F32)
    for m in range(nq // LANES):
        qd_ref[m * LANES:(m + 1) * LANES, :] = (rope(dq[:, m * LANES:(m + 1) * LANES]) * (scale * LOG2E)).T.astype(BF16)
    kv = jnp.dot(a, w_ref[:, 2 * nq:], preferred_element_type=F32)
    kc_ref[...] = rope(rms(kv[:, 0:LANES], gk_ref[...], LANES)).astype(BF16)
    kd_ref[...] = rope(kv[:, 2 * LANES:3 * LANES]).astype(BF16)
    rowi = lax.broadcasted_iota(jnp.int32, (LANES, tm), 0)
    pad = jnp.where(rowi == HEAD_DIM, 1.0, 0.0)

    def value_rows(v):
        vt = v.T
        return (jnp.where(rowi < HEAD_DIM, vt, pad).astype(BF16),
                jnp.where(rowi < HEAD_DIM, jnp.concatenate([vt[HEAD_DIM:], vt[:HEAD_DIM]], axis=0), pad).astype(BF16))

    vc_ref[0], vc_ref[1] = value_rows(kv[:, LANES:2 * LANES])
    vd = value_rows(kv[:, 3 * LANES:4 * LANES])
    for jj in range(KV_HEADS):
        for c in range(tm // LANES):
            vd_ref[jj, c] = vd[jj][:, c * LANES:(c + 1) * LANES]


def _in_odd(h, modv, w_in, bd, gq, gk, cos_t, sin_t, ctx_len):
    b, s, d = h.shape
    tm = ROW_TILE
    nq = Q_HEADS * HEAD_DIM
    row = lambda width: pl.BlockSpec((None, tm, width), lambda bi, i: (bi, i, 0))
    tab = pl.BlockSpec((tm, LANES), lambda bi, i: (i, 0))
    qt = pl.BlockSpec((None, nq, tm), lambda bi, i: (bi, 0, i))
    sds = lambda width: jax.ShapeDtypeStruct((b, s, width), BF16)
    return pl.pallas_call(
        functools.partial(_in_odd_kernel, ctx_len=ctx_len),
        grid=(b, s // tm),
        in_specs=[row(d),
                  pl.BlockSpec((None, 16, d), lambda bi, i: (bi, 0, 0)),
                  _const_spec(w_in.shape), _const_spec(bd.shape),
                  _const_spec(gq.shape), _const_spec(gk.shape), tab, tab],
        out_specs=[qt, qt, row(LANES),
                   pl.BlockSpec((None, KV_HEADS, None, LANES, tm), lambda bi, i: (bi, 0, i, 0, 0)),
                   row(LANES),
                   pl.BlockSpec((None, KV_HEADS, tm // LANES, LANES, LANES), lambda bi, i: (bi, 0, i, 0, 0))],
        out_shape=[jax.ShapeDtypeStruct((b, nq, s), BF16), jax.ShapeDtypeStruct((b, nq, s), BF16), sds(LANES),
                   jax.ShapeDtypeStruct((b, KV_HEADS, s // tm, LANES, tm), BF16), sds(LANES),
                   jax.ShapeDtypeStruct((b, KV_HEADS, s // LANES, LANES, LANES), BF16)],
        compiler_params=_cparams(("parallel", "parallel")),
        name="odd_in_proj",
    )(h, modv, w_in, bd, gq, gk, cos_t, sin_t)


def _stack_queries(q_ref, qs_ref, j, tq):
    for hh in range(GROUP):
        qh = q_ref[hh * HEAD_DIM:(hh + 1) * HEAD_DIM, :]
        zero = jnp.zeros_like(qh)
        qs_ref[0:HEAD_DIM, hh * tq:(hh + 1) * tq] = jnp.where(j == 0, qh, zero)
        qs_ref[HEAD_DIM:, hh * tq:(hh + 1) * tq] = jnp.where(j == 0, zero, qh)


def _store_heads(o, o_ref, tq):
    for m in range(GROUP // 2):
        pair = jnp.concatenate([o[:, (2 * m) * tq:(2 * m + 1) * tq],
                                o[:, (2 * m + 1) * tq:(2 * m + 2) * tq]], axis=0)
        o_ref[:, m * LANES:(m + 1) * LANES] = pair.T.astype(o_ref.dtype)


def _gattn_kernel(q_ref, k_ref, v_ref, o_ref, qs_ref, m_ref, acc_ref, sc_ref, sa_ref, sb_ref,
                  *, ctx_len, first_tile):
    tq = q_ref.shape[1]
    ntile, _, tv = v_ref.shape
    big = sa_ref.shape[0]
    nbig = (ntile * tv - ctx_len) // big
    qi = pl.program_id(2) + first_tile
    _stack_queries(q_ref, qs_ref, pl.program_id(1), tq)
    m_ref[...] = jnp.full(m_ref.shape, NEG_INF, F32)
    acc_ref[...] = jnp.zeros(acc_ref.shape, F32)

    def scores(start, s_ref):
        if not isinstance(start, int):
            start = pl.multiple_of(start, tv)
        s_ref[...] =jnp.dot(k_ref[pl.ds(start, s_ref.shape[0]), :], qs_ref[...], preferred_element_type=F32)

    def update(start, s_ref):
        sc = s_ref[...]
        m_prev = m_ref[...]
        m_new = jnp.maximum(m_prev, jnp.max(sc, axis=0, keepdims=True))
        alpha = jnp.exp2(m_prev - m_new)
        p = jnp.exp2(sc - m_new).astype(BF16)
        t0 = start // tv
        vt = jnp.concatenate([v_ref[t0 + i] for i in range(s_ref.shape[0] // tv)], axis=1)
        acc_ref[...] = alpha * acc_ref[...] + jnp.dot(vt, p, preferred_element_type=F32)
        m_ref[...] = m_new

    def latent(n):
        return ctx_len + n * big

    scores(0, sc_ref)

    @pl.when(qi * tq >= ctx_len)
    def _():
        scores(latent(0), sa_ref)
        update(0, sc_ref)

        def pair(u, _):
            scores(latent(2 * u + 1), sb_ref)
            update(latent(2 * u), sa_ref)
            scores(latent(2 * u + 2), sa_ref)
            update(latent(2 * u + 1), sb_ref)
            return 0

        lax.fori_loop(0, (nbig - 1) // 2, pair, 0)
        if nbig % 2 == 0:
            scores(latent(nbig - 1), sb_ref)
            update(latent(nbig - 2), sa_ref)
            update(latent(nbig - 1), sb_ref)
        else:
            update(latent(nbig - 1), sa_ref)

    @pl.when(qi * tq < ctx_len)
    def _():
        update(0, sc_ref)

    _store_heads(acc_ref[0:HEAD_DIM, :] / acc_ref[HEAD_DIM:HEAD_DIM + 1, :], o_ref, tq)


def _global_attention(qt, k, vt, ctx_len, first_tile):
    b, nq, s = qt.shape
    tq = ATT_TQ
    gw = GROUP * HEAD_DIM
    assert vt.shape[4] == ATT_TK and ctx_len % ATT_TK == 0 and (s - ctx_len) % ATT_BIG == 0
    return pl.pallas_call(
        functools.partial(_gattn_kernel, ctx_len=ctx_len, first_tile=first_tile),
        grid=(b, KV_HEADS, s // tq - first_tile),
        in_specs=[pl.BlockSpec((None, gw, tq), lambda bi, j, i: (bi, j, i + first_tile)),
                  pl.BlockSpec((None, s, LANES), lambda bi, j, i: (bi, 0, 0)),
                  pl.BlockSpec((None, None) + vt.shape[2:], lambda bi, j, i: (bi, j, 0, 0, 0))],
        out_specs=pl.BlockSpec((None, tq, gw), lambda bi, j, i: (bi, i, j)),
        out_shape=jax.ShapeDtypeStruct((b, s - first_tile * tq, nq), BF16),
        scratch_shapes=[pltpu.VMEM((LANES, GROUP * tq), BF16),
                        pltpu.VMEM((1, GROUP * tq), F32),
                        pltpu.VMEM((LANES, GROUP * tq), F32),
                        pltpu.VMEM((ctx_len, GROUP * tq), F32),
                        pltpu.VMEM((ATT_BIG, GROUP * tq), F32),
                        pltpu.VMEM((ATT_BIG, GROUP * tq), F32)],
        compiler_params=_cparams(("parallel", "parallel", "arbitrary")),
        name="global_attention",
    )(qt, k, vt)


def _wattn_kernel(q_ref, k_ref, v_ref, sink_ref, bias_ref, o_ref, qs_ref, *, ctx_len, span, first_tile):
    tq = q_ref.shape[1]
    s = k_ref.shape[0]
    j = pl.program_id(1)
    qi = pl.program_id(2) + first_tile
    _stack_queries(q_ref, qs_ref, j, tq)
    qs = qs_ref[...]
    sink = sink_ref[...] * LOG2E
    s_ctx = jnp.dot(k_ref[0:ctx_len, :], qs, preferred_element_type=F32)
    v_ctx = jnp.concatenate([v_ref[t] for t in range(ctx_len // LANES)], axis=1)
    is_ctx = qi * tq < ctx_len

    def finish(acc, m):
        den = acc[HEAD_DIM:HEAD_DIM + 1, :] + jnp.exp2(sink - m)
        _store_heads(acc[0:HEAD_DIM, :] / den, o_ref, tq)

    @pl.when(is_ctx)
    def _():
        m = jnp.maximum(jnp.max(s_ctx, axis=0, keepdims=True), sink)
        p = jnp.exp2(s_ctx - m)
        finish(jnp.dot(v_ctx, p.astype(BF16), preferred_element_type=F32), m)

    @pl.when(jnp.logical_not(is_ctx))
    def _():
        start = pl.multiple_of(jnp.clip(qi * tq - WINDOW, ctx_len, s - span), LANES)
        s_win = jnp.dot(k_ref[pl.ds(start, span), :], qs, preferred_element_type=F32) + bias_ref[...]
        m = jnp.maximum(jnp.maximum(jnp.max(s_ctx, axis=0, keepdims=True),
                                    jnp.max(s_win, axis=0, keepdims=True)), sink)
        p_ctx = jnp.exp2(s_ctx - m)
        p_win = jnp.exp2(s_win - m)
        t0 = start // LANES
        v_win = jnp.concatenate([v_ref[t0 + i] for i in range(span // LANES)], axis=1)
        acc = jnp.dot(v_ctx, p_ctx.astype(BF16), preferred_element_type=F32)
        finish(acc + jnp.dot(v_win, p_win.astype(BF16), preferred_element_type=F32), m)


def _window_attention(qt, k, vt, sink, ctx_len, first_tile):
    b, nq, s = qt.shape
    tq = ATT_TQ
    gw = GROUP * HEAD_DIM
    span = tq + 2 * WINDOW
    assert tq >= WINDOW and (s - ctx_len) // tq >= 2
    sink_row = jnp.repeat(sink.reshape(KV_HEADS, 1, GROUP), tq, axis=2)
    r = jnp.arange(span)[:, None]
    qcol = (jnp.arange(GROUP * tq) % tq)[None, :]
    bias = jnp.stack([jnp.where(jnp.abs(off + r - qcol) <= WINDOW, 0.0, NEG_INF)
                      for off in (0, -WINDOW, tq - span)]).astype(F32)

    def placement(i):
        lo = (i + first_tile) * tq - WINDOW
        return jnp.where(lo < ctx_len, 0, jnp.where(lo > s - span, 2, 1))

    return pl.pallas_call(
        functools.partial(_wattn_kernel, ctx_len=ctx_len, span=span, first_tile=first_tile),
        grid=(b, KV_HEADS, s // tq - first_tile),
        in_specs=[pl.BlockSpec((None, gw, tq), lambda bi, j, i: (bi, j, i + first_tile)),
                  pl.BlockSpec((None, s, LANES), lambda bi, j, i: (bi, 0, 0)),
                  pl.BlockSpec((None, None) + vt.shape[2:], lambda bi, j, i: (bi, j, 0, 0, 0)),
                  pl.BlockSpec((None, 1, GROUP * tq), lambda bi, j, i: (j, 0, 0)),
                  pl.BlockSpec((None, span, GROUP * tq), lambda bi, j, i: (placement(i), 0, 0))],
        out_specs=pl.BlockSpec((None, tq, gw), lambda bi, j, i: (bi, i, j)),
        out_shape=jax.ShapeDtypeStruct((b, s - first_tile * tq, nq), BF16),
        scratch_shapes=[pltpu.VMEM((LANES, GROUP * tq), BF16)],
        compiler_params=_cparams(("parallel", "parallel", "arbitrary")),
        name="window_attention",
    )(qt, k, vt, sink_row, bias)


def _rope_tables(n_lat, ctx_len):
    rows = n_lat // GRID_W
    row = jnp.repeat(jnp.arange(rows, dtype=F32), GRID_W)
    col = jnp.tile(jnp.arange(GRID_W, dtype=F32), rows)
    nf = HEAD_DIM // 4
    inv_freq = ROPE_THETA ** (-jnp.arange(nf, dtype=F32) / nf)
    ang_r = row[:, None] * inv_freq
    ang_c = col[:, None] * inv_freq
    cos = jnp.concatenate([jnp.cos(ang_r)] * 2 + [jnp.cos(ang_c)] * 2, axis=1)
    sin = jnp.concatenate([-jnp.sin(ang_r), jnp.sin(ang_r), -jnp.sin(ang_c), jnp.sin(ang_c)], axis=1)
    cos = jnp.concatenate([jnp.ones((ctx_len, HEAD_DIM), F32), cos], axis=0)
    sin = jnp.concatenate([jnp.zeros((ctx_len, HEAD_DIM), F32), sin], axis=0)
    return jnp.tile(cos, (1, LANES // HEAD_DIM)), jnp.tile(sin, (1, LANES // HEAD_DIM))


def kernel(x, c, ctx, c_ctx, ada_w, ada_b, ln1_g, ln1_b, ln2_g, ln2_b, ffn_w_in, ffn_w_out,
           ev_w_in, ev_w_out, rg_conv_w, rg_conv_b, rg_gate_w, rg_gate_b, rg_lambda, cm_w_s, cm_b_s,
           od_w_in, od_w_out, qn_g, kn_g, sink):
    b, t, d = x.shape
    ctx_len = ctx.shape[1]
    s = ctx_len + t
    depth = ada_w.shape[0]
    assert d == D_MODEL and depth == DEPTH and b + 1 <= 16
    assert ctx_len % ROW_TILE == 0 and t % ROW_TILE == 0 and t % GRID_W == 0 and ATT_TQ == ROW_TILE

    c_rows = jnp.zeros((16, d), F32).at[:b].set(c).at[b].set(c_ctx)
    mods = _ada_vectors(c_rows, ada_w, ada_b).reshape(depth, 16, 6, d)
    modv = jnp.zeros((depth, b, 16, d), F32)
    modv = modv.at[:, :, 0:6].set(mods[:, :b])
    modv = modv.at[:, :, 8:14].set(jnp.broadcast_to(mods[:, b][:, None], (depth, b, 6, d)))

    cos_t, sin_t = _rope_tables(t, ctx_len)
    nq = Q_HEADS * HEAD_DIM
    bd = jnp.kron(jnp.eye(nq // HEAD_DIM, dtype=F32), jnp.full((HEAD_DIM, HEAD_DIM), 1.0 / HEAD_DIM, F32)).astype(BF16)
    assert sum(FFN_CHUNKS) == FFN_HIDDEN

    h = jnp.concatenate([ctx, x], axis=1)
    rows_ctx = ctx_len
    for l in range(depth):
        j = l // 2
        mv = modv[l]
        norms = (ln1_g[l].reshape(1, d), ln1_b[l].reshape(1, d), ffn_w_in[l].astype(BF16),
                 ffn_w_out[l].astype(BF16), ln2_g[l].reshape(1, d), ln2_b[l].reshape(1, d))
        if l % 2 == 0:
            gg, xr, gu, vn = _in_even(h, mv, ev_w_in[j].astype(BF16), ctx_len)
            gw = jnp.transpose(rg_gate_w[j], (2, 0, 3, 1, 4)).reshape(RNN_HEADS, 2, LANES, 2 * LANES).astype(BF16)
            gb = jnp.transpose(rg_gate_b[j].reshape(2, 2, RNN_HEADS, LANES), (2, 0, 1, 3)).reshape(RNN_HEADS, 2, 1, 2 * LANES)
            mr = _rglru(xr, gg, rg_conv_w[j], rg_conv_b[j].reshape(1, -1), gw, gb, rg_lambda[j], ctx_len)
            bsb = jnp.broadcast_to(cm_b_s[j][:, :, None], (CMLP_GROUPS, CHUNK, CMLP_WIDTH // CMLP_GROUPS))
            h = _post_even(h, mv, mr, gu, vn, cm_w_s[j].astype(BF16), bsb, ev_w_out[j].astype(BF16), *norms, ctx_len)
        else:
            gq = jnp.tile(qn_g[j], Q_HEADS).reshape(1, nq)
            gk = jnp.tile(kn_g[j], LANES // HEAD_DIM).reshape(1, LANES)
            qc, qd, kc, vc, kd, vd = _in_odd(h, mv, od_w_in[j].astype(BF16), bd, gq, gk, cos_t, sin_t, ctx_len)
            skip = ctx_len // ATT_TQ if l == depth - 1 else 0
            yc = _global_attention(qc, kc, vc, ctx_len, skip)
            yd = _window_attention(qd, kd, vd, sink[j], ctx_len, skip)
            h = _post_odd(h, mv, yc, yd, od_w_out[j].astype(BF16), *norms, ctx_len, skip)
            if skip:
                rows_ctx = 0
    return h[:, rows_ctx:, :]
```

```python
import functools

import numpy as np
import jax
import jax.numpy as jnp
from jax import lax
from jax.experimental import pallas as pl
from jax.experimental.pallas import tpu as pltpu

F32 = jnp.float32
BF16 = jnp.bfloat16

D_MODEL = 1024
DEPTH = 4
GRID_W = 64
RNN_WIDTH = D_MODEL
RNN_HEADS = RNN_WIDTH // 128
CONV_W = 4
LRU_C = 8.0
CMLP_WIDTH = D_MODEL // 2
CMLP_GROUPS = 4
CHUNK = 128
HEAD_DIM = 64
Q_HEADS = 8
KV_HEADS = 2
GROUP = Q_HEADS // KV_HEADS
WINDOW = 128
ROPE_THETA = 10000.0
NEG_INF = -1e30
FFN_HIDDEN = 2816
FFN_CHUNKS = (1024, 1024, 768)
ALPHA = (2.0 * DEPTH) ** 0.25
LOG2E = 1.4426950408889634
EPS = 1e-6

LANES = 128
ROW_TILE = 256
SCAN_BLOCK = 128
ATT_TQ = 256
ATT_TK = 256
ATT_BIG = 1024
VMEM_LIMIT = 56 * 1024 * 1024


def _cparams(sem):
    return pltpu.CompilerParams(dimension_semantics=sem, vmem_limit_bytes=VMEM_LIMIT)


def _const_spec(shape):
    nd = len(shape)
    return pl.BlockSpec(shape, lambda *_: (0,) * nd, pipeline_mode=pl.Buffered(1))


def _gelu(x):
    return 0.5 * x * (1.0 + jnp.tanh(0.7978845608028654 * (x + 0.044715 * (x * x * x))))


def _normalise(x):
    mu = jnp.mean(x, axis=-1, keepdims=True)
    xc = x - mu
    var = jnp.mean(xc * xc, axis=-1, keepdims=True)
    return xc * lax.rsqrt(var + EPS)


def _mod_rows(m_ref, is_ctx, lat_row):
    return jnp.where(is_ctx, m_ref[lat_row + 8:lat_row + 9, :], m_ref[lat_row:lat_row + 1, :])


def _mod_kernel(c_ref, w_ref, b_ref, o_ref):
    c = c_ref[...]
    s = c * jax.nn.sigmoid(c)
    o_ref[...] = jnp.dot(s, w_ref[...], preferred_element_type=F32,
                         precision=lax.Precision.HIGHEST) + b_ref[...]


def _ada_vectors(c_rows, ada_w, ada_b):
    depth, d, n = ada_w.shape
    rows = c_rows.shape[0]
    nb = 1536
    return pl.pallas_call(
        _mod_kernel,
        grid=(depth, n // nb),
        in_specs=[pl.BlockSpec((rows, d), lambda l, j: (0, 0)),
                  pl.BlockSpec((None, d, nb), lambda l, j: (l, 0, j)),
                  pl.BlockSpec((None, 1, nb), lambda l, j: (l, 0, j))],
        out_specs=pl.BlockSpec((None, rows, nb), lambda l, j: (l, 0, j)),
        out_shape=jax.ShapeDtypeStruct((depth, rows, n), F32),
        compiler_params=_cparams(("parallel", "parallel")),
        name="ada_vectors",
    )(c_rows, ada_w, ada_b.reshape(depth, 1, n))


def _in_even_kernel(h_ref, m_ref, w_ref, gg_ref, xr_ref, gu_ref, vn_ref, *, ctx_len):
    tm = h_ref.shape[0]
    w = RNN_WIDTH
    for rows in _sub_tiles(tm):
        is_ctx = pl.program_id(1) * tm + rows.start < ctx_len
        sh = _mod_rows(m_ref, is_ctx, 0)
        sc = _mod_rows(m_ref, is_ctx, 1)
        a = (h_ref[rows, :] * (1.0 + sc) + sh).astype(BF16)
        gate = jnp.dot(a, w_ref[:, 0:w], preferred_element_type=F32)
        gg_ref[rows, :] = _gelu(gate).astype(BF16)
        xr_ref[rows, :] = jnp.dot(a, w_ref[:, w:2 * w], preferred_element_type=F32)
        u = jnp.dot(a, w_ref[:, 2 * w:2 * w + CMLP_WIDTH], preferred_element_type=F32)
        gu_ref[rows, :] = _gelu(u).astype(BF16)
        v = jnp.dot(a, w_ref[:, 2 * w + CMLP_WIDTH:], preferred_element_type=F32)
        vn_ref[rows, :] = _normalise(_gelu(v)).astype(BF16)


def _in_even(h, modv, w_in, ctx_len):
    b, s, d = h.shape
    tm = _post_tile(s, 0)
    n_in = w_in.shape[1]
    row = lambda width: pl.BlockSpec((None, tm, width), lambda bi, i: (bi, i, 0))
    return pl.pallas_call(
        functools.partial(_in_even_kernel, ctx_len=ctx_len),
        grid=(b, s // tm),
        in_specs=[row(d),
                  pl.BlockSpec((None, 16, d), lambda bi, i: (bi, 0, 0)),
                  _const_spec((d, n_in))],
        out_specs=[row(RNN_WIDTH), row(RNN_WIDTH), row(CMLP_WIDTH), row(CMLP_WIDTH)],
        out_shape=[jax.ShapeDtypeStruct((b, s, RNN_WIDTH), BF16),
                   jax.ShapeDtypeStruct((b, s, RNN_WIDTH), F32),
                   jax.ShapeDtypeStruct((b, s, CMLP_WIDTH), BF16),
                   jax.ShapeDtypeStruct((b, s, CMLP_WIDTH), BF16)],
        compiler_params=_cparams(("parallel", "parallel")),
        name="even_in_proj",
    )(h, modv, w_in)


def _block_scan(a_ref, b_ref, carry, reverse):
    ntile = a_ref.shape[0] // 8
    sub = lax.broadcasted_iota(jnp.int32, (8, a_ref.shape[1]), 0)
    out = [None] * ntile
    for k in (range(ntile - 1, -1, -1) if reverse else range(ntile)):
        ak = a_ref[8 * k:8 * k + 8, :]
        bk = b_ref[8 * k:8 * k + 8, :]
        for d in (1, 2, 4):
            if reverse:
                a_s = pltpu.roll(ak, 8 - d, 0)
                b_s = pltpu.roll(bk, 8 - d, 0)
                ok = sub < 8 - d
            else:
                a_s = pltpu.roll(ak, d, 0)
                b_s = pltpu.roll(bk, d, 0)
                ok = sub >= d
            bk = jnp.where(ok, bk + ak * b_s, bk)
            ak = jnp.where(ok, ak * a_s, ak)
        out[k] = bk + ak * carry
        last = 0 if reverse else 7
        carry = (jnp.broadcast_to(bk[last:last + 1, :], bk.shape)
                 + jnp.broadcast_to(ak[last:last + 1, :], ak.shape) * carry)
    return jnp.concatenate(out, axis=0), carry


def _scan_kernel(xr_ref, gg_ref, cw_ref, cb_ref, gw_ref, gb_ref, lam_ref, out_ref, xc_ref, rf_ref, rr_ref,
                 ab0_ref, ab1_ref, g0_ref, g1_ref, *, ctx_len, tb):
    s = xr_ref.shape[0]
    nblk = s // tb
    ncb = ctx_len // tb
    cw = cw_ref[...]
    cb = cb_ref[...]
    row = lax.broadcasted_iota(jnp.int32, (tb, LANES), 0)

    def conv(blk, _):
        t0 = pl.multiple_of(blk * tb, tb)
        x = xr_ref[pl.ds(t0, tb), :]
        prev = xr_ref[pl.ds(pl.multiple_of(jnp.maximum(t0 - 8, 0), 8), 8), :]
        nxt = xr_ref[pl.ds(pl.multiple_of(jnp.minimum(t0 + tb, s - 8), 8), 8), :]
        pf = jnp.where((blk == 0) | (blk == ncb), 0.0, 1.0)
        nf = jnp.where((blk == ncb - 1) | (blk == nblk - 1), 0.0, 1.0)
        p6 = prev[6:7, :] * pf
        p7 = prev[7:8, :] * pf
        n0 = nxt[0:1, :] * nf
        xm1 = jnp.where(row == 0, p7, pltpu.roll(x, 1, 0))
        xm2 = jnp.where(row == 0, p6, jnp.where(row == 1, p7, pltpu.roll(x, 2, 0)))
        xp1 = jnp.where(row == tb - 1, n0, pltpu.roll(x, tb - 1, 0))
        xc_ref[pl.ds(t0, tb), :] = (xm2 * cw[0:1, :] + xm1 * cw[1:2, :] + x * cw[2:3, :] + xp1 * cw[3:4, :]
                                    + cb)
        return 0

    lax.fori_loop(0, nblk, conv, 0)

    def block_start(step, d):
        blk = step if d == 0 else jnp.where(step < ncb, ncb - 1 - step, nblk - 1 - (step - ncb))
        return pl.multiple_of(blk * tb, tb)

    def gate_matmuls(step, g_ref):
        step = jnp.minimum(step, nblk - 1)
        for d in range(2):
            xc = xc_ref[pl.ds(block_start(step, d), tb), :]
            g_ref[d] = jnp.dot(xc.astype(BF16), gw_ref[d], preferred_element_type=F32)

    def stash_coeffs(step, g_ref, ab_ref):
        step = jnp.minimum(step, nblk - 1)
        for d in range(2):
            xc = xc_ref[pl.ds(block_start(step, d), tb), :]
            g = g_ref[d] + gb_ref[d]
            r = jax.nn.sigmoid(g[:, :LANES])
            ig = jax.nn.sigmoid(g[:, LANES:])
            z = -lam_ref[d:d + 1, :]
            softplus = jnp.maximum(z, 0.0) + jnp.log(1.0 + jnp.exp(-jnp.abs(z)))
            a = jnp.exp((-LRU_C) * r * softplus)
            ab_ref[2 * d] = a
            ab_ref[2 * d + 1] = jnp.sqrt(1.0 - a * a) * (ig * xc)

    def scan(step, ab_ref, cf, cr):
        hf, cf = _block_scan(ab_ref.at[0], ab_ref.at[1], cf, False)
        rf_ref[pl.ds(block_start(step, 0), tb), :] = hf
        hr, cr = _block_scan(ab_ref.at[2], ab_ref.at[3], cr, True)
        rr_ref[pl.ds(block_start(step, 1), tb), :] = hr
        return cf, cr

    gate_matmuls(0, g0_ref)
    stash_coeffs(0, g0_ref, ab0_ref)
    gate_matmuls(1, g1_ref)
    gate_matmuls(2, g0_ref)

    def two_steps(u, carry):
        cf, cr = carry
        cf, cr = scan(2 * u, ab0_ref, cf, cr)
        stash_coeffs(2 * u + 1, g1_ref, ab1_ref)
        cf, cr = scan(2 * u + 1, ab1_ref, cf, cr)
        stash_coeffs(2 * u + 2, g0_ref, ab0_ref)
        gate_matmuls(2 * u + 3, g1_ref)
        gate_matmuls(2 * u + 4, g0_ref)
        return cf, cr

    zero = jnp.zeros((8, LANES), F32)
    lax.fori_loop(0, nblk // 2, two_steps, (zero, zero))

    def combine(j, _):
        t0 = pl.multiple_of(j * tb, tb)
        rec = rf_ref[pl.ds(t0, tb), :] + rr_ref[pl.ds(t0, tb), :]
        out_ref[pl.ds(t0, tb), :] = (gg_ref[pl.ds(t0, tb), :].astype(F32) * rec).astype(BF16)
        return 0

    lax.fori_loop(0, nblk, combine, 0)


def _rglru(xr, gg, conv_w, conv_b, gate_w, gate_b, lam, ctx_len):
    b, s, w = xr.shape
    nh = w // LANES
    assert (s // SCAN_BLOCK) % 2 == 0 and ctx_len % SCAN_BLOCK == 0
    col = lambda dt: pl.BlockSpec((None, s, LANES), lambda bi, hd: (bi, 0, hd))
    return pl.pallas_call(
        functools.partial(_scan_kernel, ctx_len=ctx_len, tb=SCAN_BLOCK),
        grid=(b, nh),
        in_specs=[col(F32), col(BF16),
                  pl.BlockSpec((CONV_W, LANES), lambda bi, hd: (0, hd)),
                  pl.BlockSpec((1, LANES), lambda bi, hd: (0, hd)),
                  pl.BlockSpec((None, 2, LANES, 2 * LANES), lambda bi, hd: (hd, 0, 0, 0)),
                  pl.BlockSpec((None, 2, 1, 2 * LANES), lambda bi, hd: (hd, 0, 0, 0)),
                  pl.BlockSpec((2, LANES), lambda bi, hd: (0, hd))],
        out_specs=col(BF16),
        out_shape=jax.ShapeDtypeStruct((b, s, w), BF16),
        scratch_shapes=[pltpu.VMEM((s, LANES), F32), pltpu.VMEM((s, LANES), F32), pltpu.VMEM((s, LANES), F32),
                        pltpu.VMEM((4, SCAN_BLOCK, LANES), F32), pltpu.VMEM((4, SCAN_BLOCK, LANES), F32),
                        pltpu.VMEM((2, SCAN_BLOCK, 2 * LANES), F32), pltpu.VMEM((2, SCAN_BLOCK, 2 * LANES), F32)],
        compiler_params=_cparams(("parallel", "parallel")),
        name="rglru_scan",
    )(xr, gg, conv_w, conv_b, gate_w, gate_b, lam)


def _residual_ln(h, y, gate, g, b):
    return _normalise(ALPHA * h + gate * y) * g + b


def _swiglu(a, wi_ref, wo_ref):
    acc = None
    c0 = 0
    for width in FFN_CHUNKS:
        zg = jnp.dot(a, wi_ref[:, c0:c0 + width], preferred_element_type=F32)
        zu = jnp.dot(a, wi_ref[:, FFN_HIDDEN + c0:FFN_HIDDEN + c0 + width], preferred_element_type=F32)
        hm = (zg * jax.nn.sigmoid(zg) * zu).astype(BF16)
        y = jnp.dot(hm, wo_ref[c0:c0 + width, :], preferred_element_type=F32)
        acc = y if acc is None else acc + y
        c0 += width
    return acc


def _residual_pair(h, y_mix, m_ref, is_ctx, l1g_ref, l1b_ref, wi_ref, wo_ref, l2g_ref, l2b_ref):
    h1 = _residual_ln(h, y_mix, _mod_rows(m_ref, is_ctx, 2), l1g_ref[...], l1b_ref[...])
    a = (h1 * (1.0 + _mod_rows(m_ref, is_ctx, 4)) + _mod_rows(m_ref, is_ctx, 3)).astype(BF16)
    return _residual_ln(h1, _swiglu(a, wi_ref, wo_ref), _mod_rows(m_ref, is_ctx, 5), l2g_ref[...], l2b_ref[...])


def _sub_tiles(tm):
    return [slice(r * ROW_TILE, (r + 1) * ROW_TILE) for r in range(tm // ROW_TILE)]


def _post_tile(n_rows, first_tile):
    return 3 * ROW_TILE if first_tile == 0 and n_rows % (3 * ROW_TILE) == 0 else ROW_TILE


def _post_even_kernel(h_ref, m_ref, mr_ref, gu_ref, vn_ref, ws_ref, bs_ref, wm_ref, l1g_ref, l1b_ref,
                      wi_ref, wo_ref, l2g_ref, l2b_ref, o_ref, gm_ref, *, ctx_len):
    tm = h_ref.shape[0]
    gw = CMLP_WIDTH // CMLP_GROUPS
    for rows in _sub_tiles(tm):
        is_ctx = pl.program_id(1) * tm + rows.start < ctx_len
        for c in range(rows.start, rows.stop, CHUNK):
            crow = slice(c, c + CHUNK)
            for g in range(CMLP_GROUPS):
                cols = slice(g * gw, (g + 1) * gw)
                mixed = jnp.dot(ws_ref[g], vn_ref[crow, cols], preferred_element_type=F32) + bs_ref[g]
                gm_ref[crow, cols] = (gu_ref[crow, cols].astype(F32) * mixed).astype(BF16)
        y = jnp.dot(mr_ref[rows, :], wm_ref[0:RNN_WIDTH, :], preferred_element_type=F32)
        y = y + jnp.dot(gm_ref[rows, :], wm_ref[RNN_WIDTH:, :], preferred_element_type=F32)
        o_ref[rows, :] = _residual_pair(h_ref[rows, :], y, m_ref, is_ctx, l1g_ref, l1b_ref,
                                        wi_ref, wo_ref, l2g_ref, l2b_ref)


def _post_even(h, modv, mr, gu, vn, ws, bsb, w_mix, l1g, l1b, w_in, w_out, l2g, l2b, ctx_len):
    b, s, d = h.shape
    tm = _post_tile(s, 0)
    row = lambda width: pl.BlockSpec((None, tm, width), lambda bi, i: (bi, i, 0))
    vec = _const_spec((1, d))
    return pl.pallas_call(
        functools.partial(_post_even_kernel, ctx_len=ctx_len),
        grid=(b, s // tm),
        in_specs=[row(d),
                  pl.BlockSpec((None, 16, d), lambda bi, i: (bi, 0, 0)),
                  row(RNN_WIDTH), row(CMLP_WIDTH), row(CMLP_WIDTH),
                  _const_spec(ws.shape), _const_spec(bsb.shape), _const_spec(w_mix.shape), vec, vec,
                  _const_spec(w_in.shape), _const_spec(w_out.shape), vec, vec],
        out_specs=row(d),
        out_shape=jax.ShapeDtypeStruct((b, s, d), F32),
        scratch_shapes=[pltpu.VMEM((tm, CMLP_WIDTH), BF16)],
        compiler_params=_cparams(("parallel", "parallel")),
        name="even_out_ffn",
    )(h, modv, mr, gu, vn, ws, bsb, w_mix, l1g, l1b, w_in, w_out, l2g, l2b)


def _post_odd_kernel(h_ref, m_ref, yc_ref, yd_ref, wm_ref, l1g_ref, l1b_ref, wi_ref, wo_ref, l2g_ref, l2b_ref,
                     o_ref, *, ctx_len, first_tile):
    tm = h_ref.shape[0]
    half = yc_ref.shape[1]
    for rows in _sub_tiles(tm):
        is_ctx = pl.program_id(1) * tm + first_tile * ROW_TILE + rows.start < ctx_len
        y = jnp.dot(yc_ref[rows, :], wm_ref[0:half, :], preferred_element_type=F32)
        y = y + jnp.dot(yd_ref[rows, :], wm_ref[half:, :], preferred_element_type=F32)
        o_ref[rows, :] = _residual_pair(h_ref[rows, :], y, m_ref, is_ctx, l1g_ref, l1b_ref,
                                        wi_ref, wo_ref, l2g_ref, l2b_ref)


def _post_odd(h, modv, yc, yd, w_mix, l1g, l1b, w_in, w_out, l2g, l2b, ctx_len, first_tile):
    b, s, d = h.shape
    tm = _post_tile(s, first_tile)
    n = (s - first_tile * ROW_TILE) // tm
    assert yc.shape[1] == n * tm and yd.shape[1] == n * tm
    row = lambda width: pl.BlockSpec((None, tm, width), lambda bi, i: (bi, i, 0))
    vec = _const_spec((1, d))
    return pl.pallas_call(
        functools.partial(_post_odd_kernel, ctx_len=ctx_len, first_tile=first_tile),
        grid=(b, n),
        in_specs=[pl.BlockSpec((None, tm, d), lambda bi, i: (bi, i + first_tile, 0)),
                  pl.BlockSpec((None, 16, d), lambda bi, i: (bi, 0, 0)),
                  row(yc.shape[2]), row(yd.shape[2]),
                  _const_spec(w_mix.shape), vec, vec,
                  _const_spec(w_in.shape), _const_spec(w_out.shape), vec, vec],
        out_specs=row(d),
        out_shape=jax.ShapeDtypeStruct((b, n * tm, d), F32),
        compiler_params=_cparams(("parallel", "parallel")),
        name="odd_out_ffn",
    )(h, modv, yc, yd, w_mix, l1g, l1b, w_in, w_out, l2g, l2b)


def _in_odd_kernel(h_ref, m_ref, w_ref, bd_ref, gq_ref, gk_ref, cos_ref, sin_ref,
                   qc_ref, qd_ref, kc_ref, vc_ref, kd_ref, vd_ref, *, ctx_len):
    tm = h_ref.shape[0]
    is_ctx = pl.program_id(1) * tm < ctx_len
    sh = _mod_rows(m_ref, is_ctx, 0)
    sc = _mod_rows(m_ref, is_ctx, 1)
    a = (h_ref[...] * (1.0 + sc) + sh).astype(BF16)
    cos = cos_ref[...]
    sin = sin_ref[...]
    lane = lax.broadcasted_iota(jnp.int32, (tm, LANES), 1)
    first = (lane & 31) < 16
    scale = HEAD_DIM ** -0.5

    def rope(x):
        partner = jnp.where(first, pltpu.roll(x, LANES - 16, 1), pltpu.roll(x, 16, 1))
        return x * cos + partner * sin

    def rms(x, gain, width):
        ms = jnp.dot((x * x).astype(BF16), bd_ref[0:width, 0:width], preferred_element_type=F32)
        return x * lax.rsqrt(ms + EPS) * gain

    nq = Q_HEADS * HEAD_DIM
    cq = rms(jnp.dot(a, w_ref[:, 0:nq], preferred_element_type=F32), gq_ref[...], nq)
    for m in range(nq // LANES):
        qc_ref[m * LANES:(m + 1) * LANES, :] = (rope(cq[:, m * LANES:(m + 1) * LANES]) * (scale * LOG2E)).T.astype(BF16)
    dq = jnp.dot(a, w_ref[:, nq:2 * nq], preferred_element_type=F32)
    for m in range(nq // LANES):
        qd_ref[m * LANES:(m + 1) * LANES, :] = (rope(dq[:, m * LANES:(m + 1) * LANES]) * (scale * LOG2E)).T.astype(BF16)
    kv = jnp.dot(a, w_ref[:, 2 * nq:], preferred_element_type=F32)
    kc_ref[...] = rope(rms(kv[:, 0:LANES], gk_ref[...], LANES)).astype(BF16)
    kd_ref[...] = rope(kv[:, 2 * LANES:3 * LANES]).astype(BF16)
    rowi = lax.broadcasted_iota(jnp.int32, (LANES, tm), 0)
    pad = jnp.where(rowi == HEAD_DIM, 1.0, 0.0)

    def value_rows(v):
        vt = v.T
        return (jnp.where(rowi < HEAD_DIM, vt, pad).astype(BF16),
                jnp.where(rowi < HEAD_DIM, jnp.concatenate([vt[HEAD_DIM:], vt[:HEAD_DIM]], axis=0), pad).astype(BF16))

    vc_ref[0], vc_ref[1] = value_rows(kv[:, LANES:2 * LANES])
    vd = value_rows(kv[:, 3 * LANES:4 * LANES])
    for jj in range(KV_HEADS):
        for c in range(tm // LANES):
            vd_ref[jj, c] = vd[jj][:, c * LANES:(c + 1) * LANES]


def _in_odd(h, modv, w_in, bd, gq, gk, cos_t, sin_t, ctx_len):
    b, s, d = h.shape
    tm = ROW_TILE
    nq = Q_HEADS * HEAD_DIM
    row = lambda width: pl.BlockSpec((None, tm, width), lambda bi, i: (bi, i, 0))
    tab = pl.BlockSpec((tm, LANES), lambda bi, i: (i, 0))
    qt = pl.BlockSpec((None, nq, tm), lambda bi, i: (bi, 0, i))
    sds = lambda width: jax.ShapeDtypeStruct((b, s, width), BF16)
    return pl.pallas_call(
        functools.partial(_in_odd_kernel, ctx_len=ctx_len),
        grid=(b, s // tm),
        in_specs=[row(d),
                  pl.BlockSpec((None, 16, d), lambda bi, i: (bi, 0, 0)),
                  _const_spec(w_in.shape), _const_spec(bd.shape),
                  _const_spec(gq.shape), _const_spec(gk.shape), tab, tab],
        out_specs=[qt, qt, row(LANES),
                   pl.BlockSpec((None, KV_HEADS, None, LANES, tm), lambda bi, i: (bi, 0, i, 0, 0)),
                   row(LANES),
                   pl.BlockSpec((None, KV_HEADS, tm // LANES, LANES, LANES), lambda bi, i: (bi, 0, i, 0, 0))],
        out_shape=[jax.ShapeDtypeStruct((b, nq, s), BF16), jax.ShapeDtypeStruct((b, nq, s), BF16), sds(LANES),
                   jax.ShapeDtypeStruct((b, KV_HEADS, s // tm, LANES, tm), BF16), sds(LANES),
                   jax.ShapeDtypeStruct((b, KV_HEADS, s // LANES, LANES, LANES), BF16)],
        compiler_params=_cparams(("parallel", "parallel")),
        name="odd_in_proj",
    )(h, modv, w_in, bd, gq, gk, cos_t, sin_t)


def _stack_queries(q_ref, qs_ref, j, tq):
    for hh in range(GROUP):
        qh = q_ref[hh * HEAD_DIM:(hh + 1) * HEAD_DIM, :]
        zero = jnp.zeros_like(qh)
        qs_ref[0:HEAD_DIM, hh * tq:(hh + 1) * tq] = jnp.where(j == 0, qh, zero)
        qs_ref[HEAD_DIM:, hh * tq:(hh + 1) * tq] = jnp.where(j == 0, zero, qh)


def _store_heads(o, o_ref, tq):
    for m in range(GROUP // 2):
        pair = jnp.concatenate([o[:, (2 * m) * tq:(2 * m + 1) * tq],
                                o[:, (2 * m + 1) * tq:(2 * m + 2) * tq]], axis=0)
        o_ref[:, m * LANES:(m + 1) * LANES] = pair.T.astype(o_ref.dtype)


def _gattn_kernel(q_ref, k_ref, v_ref, o_ref, *scratch, ctx_len, first_tile):
    tq = q_ref.shape[1]
    gw = GROUP * HEAD_DIM
    _, ntile, _, tv = v_ref.shape
    chains = [scratch[6 * j:6 * j + 6] for j in range(KV_HEADS)]
    big = chains[0][4].shape[0]
    nbig = (ntile * tv - ctx_len) // big
    qi = pl.program_id(1) + first_tile

    def scores(j, start, s_ref):
        if not isinstance(start, int):
            start = pl.multiple_of(start, tv)
        s_ref[...] = jnp.dot(k_ref[pl.ds(start, s_ref.shape[0]), :], chains[j][0][...],
                             preferred_element_type=F32)

    def update(j, start, s_ref):
        m_ref, acc_ref = chains[j][1], chains[j][2]
        sc = s_ref[...]
        m_prev = m_ref[...]
        m_new = jnp.maximum(m_prev, jnp.max(sc, axis=0, keepdims=True))
        alpha = jnp.exp2(m_prev - m_new)
        p = jnp.exp2(sc - m_new).astype(BF16)
        t0 = start // tv
        vt = jnp.concatenate([v_ref[j, t0 + i] for i in range(s_ref.shape[0] // tv)], axis=1)
        acc_ref[...] = alpha * acc_ref[...] + jnp.dot(vt, p, preferred_element_type=F32)
        m_ref[...] = m_new

    def latent(n):
        return ctx_len + n * big

    for j, (qs_ref, m_ref, acc_ref, sc_ref, _, _) in enumerate(chains):
        _stack_queries(q_ref.at[j * gw:(j + 1) * gw, :], qs_ref, j, tq)
        m_ref[...] = jnp.full(m_ref.shape, NEG_INF, F32)
        acc_ref[...] = jnp.zeros(acc_ref.shape, F32)
        scores(j, 0, sc_ref)

    @pl.when(qi * tq >= ctx_len)
    def _():
        for j, (_, _, _, sc_ref, sa_ref, _) in enumerate(chains):
            scores(j, latent(0), sa_ref)
            update(j, 0, sc_ref)

        def pair(u, _):
            for j, (_, _, _, _, sa_ref, sb_ref) in enumerate(chains):
                scores(j, latent(2 * u + 1), sb_ref)
                update(j, latent(2 * u), sa_ref)
            for j, (_, _, _, _, sa_ref, sb_ref) in enumerate(chains):
                scores(j, latent(2 * u + 2), sa_ref)
                update(j, latent(2 * u + 1), sb_ref)
            return 0

        lax.fori_loop(0, (nbig - 1) // 2, pair, 0)
        for j, (_, _, _, _, sa_ref, sb_ref) in enumerate(chains):
            if nbig % 2 == 0:
                scores(j, latent(nbig - 1), sb_ref)
                update(j, latent(nbig - 2), sa_ref)
                update(j, latent(nbig - 1), sb_ref)
            else:
                update(j, latent(nbig - 1), sa_ref)

    @pl.when(qi * tq < ctx_len)
    def _():
        for j in range(KV_HEADS):
            update(j, 0, chains[j][3])

    for j in range(KV_HEADS):
        acc_ref = chains[j][2]
        _store_heads(acc_ref[0:HEAD_DIM, :] / acc_ref[HEAD_DIM:HEAD_DIM + 1, :],
                     o_ref.at[:, j * gw:(j + 1) * gw], tq)


def _global_attention(qt, k, vt, ctx_len, first_tile):
    b, nq, s = qt.shape
    tq = ATT_TQ
    assert vt.shape[4] == ATT_TK and ctx_len % ATT_TK == 0 and (s - ctx_len) % ATT_BIG == 0
    cols = GROUP * tq
    per_head = [pltpu.VMEM((LANES, cols), BF16), pltpu.VMEM((1, cols), F32), pltpu.VMEM((LANES, cols), F32),
                pltpu.VMEM((ctx_len, cols), F32), pltpu.VMEM((ATT_BIG, cols), F32),
                pltpu.VMEM((ATT_BIG, cols), F32)]
    return pl.pallas_call(
        functools.partial(_gattn_kernel, ctx_len=ctx_len, first_tile=first_tile),
        grid=(b, s // tq - first_tile),
        in_specs=[pl.BlockSpec((None, nq, tq), lambda bi, i: (bi, 0, i + first_tile)),
                  pl.BlockSpec((None, s, LANES), lambda bi, i: (bi, 0, 0)),
                  pl.BlockSpec((None,) + vt.shape[1:], lambda bi, i: (bi, 0, 0, 0, 0))],
        out_specs=pl.BlockSpec((None, tq, nq), lambda bi, i: (bi, i, 0)),
        out_shape=jax.ShapeDtypeStruct((b, s - first_tile * tq, nq), BF16),
        scratch_shapes=per_head * KV_HEADS,
        compiler_params=_cparams(("parallel", "arbitrary")),
        name="global_attention",
    )(qt, k, vt)


def _wattn_kernel(q_ref, k_ref, v_ref, sink_ref, bias_ref, o_ref, *qs_refs, ctx_len, span, first_tile):
    tq = q_ref.shape[1]
    s = k_ref.shape[0]
    gw = GROUP * HEAD_DIM
    qi = pl.program_id(1) + first_tile
    is_ctx = qi * tq < ctx_len

    def head(j, window_start):
        qs_ref = qs_refs[j]
        _stack_queries(q_ref.at[j * gw:(j + 1) * gw, :], qs_ref, j, tq)
        qs = qs_ref[...]
        sink = sink_ref[j] * LOG2E
        s_ctx = jnp.dot(k_ref[0:ctx_len, :], qs, preferred_element_type=F32)
        v_ctx = jnp.concatenate([v_ref[j, t] for t in range(ctx_len // LANES)], axis=1)
        m = jnp.maximum(jnp.max(s_ctx, axis=0, keepdims=True), sink)
        if window_start is not None:
            s_win = (jnp.dot(k_ref[pl.ds(window_start, span), :], qs, preferred_element_type=F32)
                     + bias_ref[...])
            m = jnp.maximum(m, jnp.max(s_win, axis=0, keepdims=True))
        acc = jnp.dot(v_ctx, jnp.exp2(s_ctx - m).astype(BF16), preferred_element_type=F32)
        if window_start is not None:
            t0 = window_start // LANES
            v_win = jnp.concatenate([v_ref[j, t0 + i] for i in range(span // LANES)], axis=1)
            acc = acc + jnp.dot(v_win, jnp.exp2(s_win - m).astype(BF16), preferred_element_type=F32)
        den = acc[HEAD_DIM:HEAD_DIM + 1, :] + jnp.exp2(sink - m)
        _store_heads(acc[0:HEAD_DIM, :] / den, o_ref.at[:, j * gw:(j + 1) * gw], tq)

    @pl.when(is_ctx)
    def _():
        for j in range(KV_HEADS):
            head(j, None)

    @pl.when(jnp.logical_not(is_ctx))
    def _():
        start = pl.multiple_of(jnp.clip(qi * tq - WINDOW, ctx_len, s - span), LANES)
        for j in range(KV_HEADS):
            head(j, start)


def _window_attention(qt, k, vt, sink, ctx_len, first_tile):
    b, nq, s = qt.shape
    tq = ATT_TQ
    gw = GROUP * HEAD_DIM
    span = tq + 2 * WINDOW
    assert tq >= WINDOW and (s - ctx_len) // tq >= 2
    sink_row = jnp.repeat(sink.reshape(KV_HEADS, 1, GROUP), tq, axis=2)
    r = jnp.arange(span)[:, None]
    qcol = (jnp.arange(GROUP * tq) % tq)[None, :]
    bias = jnp.stack([jnp.where(jnp.abs(off + r - qcol) <= WINDOW, 0.0, NEG_INF)
                      for off in (0, -WINDOW, tq - span)]).astype(F32)

    def placement(i):
        lo = (i + first_tile) * tq - WINDOW
        return jnp.where(lo < ctx_len, 0, jnp.where(lo > s - span, 2, 1))

    return pl.pallas_call(
        functools.partial(_wattn_kernel, ctx_len=ctx_len, span=span, first_tile=first_tile),
        grid=(b, s // tq - first_tile),
        in_specs=[pl.BlockSpec((None, nq, tq), lambda bi, i: (bi, 0, i + first_tile)),
                  pl.BlockSpec((None, s, LANES), lambda bi, i: (bi, 0, 0)),
                  pl.BlockSpec((None,) + vt.shape[1:], lambda bi, i: (bi, 0, 0, 0, 0)),
                  pl.BlockSpec((KV_HEADS, 1, GROUP * tq), lambda bi, i: (0, 0, 0)),
                  pl.BlockSpec((None, span, GROUP * tq), lambda bi, i: (placement(i), 0, 0))],
        out_specs=pl.BlockSpec((None, tq, nq), lambda bi, i: (bi, i, 0)),
        out_shape=jax.ShapeDtypeStruct((b, s - first_tile * tq, nq), BF16),
        scratch_shapes=[pltpu.VMEM((LANES, GROUP * tq), BF16)] * KV_HEADS,
        compiler_params=_cparams(("parallel", "parallel")),
        name="window_attention",
    )(qt, k, vt, sink_row, bias)


def _rope_tables(n_lat, ctx_len):
    rows = n_lat // GRID_W
    row = jnp.repeat(jnp.arange(rows, dtype=F32), GRID_W)
    col = jnp.tile(jnp.arange(GRID_W, dtype=F32), rows)
    nf = HEAD_DIM // 4
    inv_freq = ROPE_THETA ** (-jnp.arange(nf, dtype=F32) / nf)
    ang_r = row[:, None] * inv_freq
    ang_c = col[:, None] * inv_freq
    cos = jnp.concatenate([jnp.cos(ang_r)] * 2 + [jnp.cos(ang_c)] * 2, axis=1)
    sin = jnp.concatenate([-jnp.sin(ang_r), jnp.sin(ang_r), -jnp.sin(ang_c), jnp.sin(ang_c)], axis=1)
    cos = jnp.concatenate([jnp.ones((ctx_len, HEAD_DIM), F32), cos], axis=0)
    sin = jnp.concatenate([jnp.zeros((ctx_len, HEAD_DIM), F32), sin], axis=0)
    return jnp.tile(cos, (1, LANES // HEAD_DIM)), jnp.tile(sin, (1, LANES // HEAD_DIM))


def kernel(x, c, ctx, c_ctx, ada_w, ada_b, ln1_g, ln1_b, ln2_g, ln2_b, ffn_w_in, ffn_w_out,
           ev_w_in, ev_w_out, rg_conv_w, rg_conv_b, rg_gate_w, rg_gate_b, rg_lambda, cm_w_s, cm_b_s,
           od_w_in, od_w_out, qn_g, kn_g, sink):
    b, t, d = x.shape
    ctx_len = ctx.shape[1]
    s = ctx_len + t
    depth = ada_w.shape[0]
    assert d == D_MODEL and depth == DEPTH and b + 1 <= 16
    assert ctx_len % ROW_TILE == 0 and t % ROW_TILE == 0 and t % GRID_W == 0 and ATT_TQ == ROW_TILE

    c_rows = jnp.zeros((16, d), F32).at[:b].set(c).at[b].set(c_ctx)
    mods = _ada_vectors(c_rows, ada_w, ada_b).reshape(depth, 16, 6, d)
    modv = jnp.zeros((depth, b, 16, d), F32)
    modv = modv.at[:, :, 0:6].set(mods[:, :b])
    modv = modv.at[:, :, 8:14].set(jnp.broadcast_to(mods[:, b][:, None], (depth, b, 6, d)))

    cos_t, sin_t = _rope_tables(t, ctx_len)
    nq = Q_HEADS * HEAD_DIM
    bd = jnp.kron(jnp.eye(nq // HEAD_DIM, dtype=F32), jnp.full((HEAD_DIM, HEAD_DIM), 1.0 / HEAD_DIM, F32)).astype(BF16)
    assert sum(FFN_CHUNKS) == FFN_HIDDEN

    h = jnp.concatenate([ctx, x], axis=1)
    rows_ctx = ctx_len
    for l in range(depth):
        j = l // 2
        mv = modv[l]
        norms = (ln1_g[l].reshape(1, d), ln1_b[l].reshape(1, d), ffn_w_in[l].astype(BF16),
                 ffn_w_out[l].astype(BF16), ln2_g[l].reshape(1, d), ln2_b[l].reshape(1, d))
        if l % 2 == 0:
            gg, xr, gu, vn = _in_even(h, mv, ev_w_in[j].astype(BF16), ctx_len)
            gw = jnp.transpose(rg_gate_w[j], (2, 0, 3, 1, 4)).reshape(RNN_HEADS, 2, LANES, 2 * LANES).astype(BF16)
            gb = jnp.transpose(rg_gate_b[j].reshape(2, 2, RNN_HEADS, LANES), (2, 0, 1, 3)).reshape(RNN_HEADS, 2, 1, 2 * LANES)
            mr = _rglru(xr, gg, rg_conv_w[j], rg_conv_b[j].reshape(1, -1), gw, gb, rg_lambda[j], ctx_len)
            bsb = jnp.broadcast_to(cm_b_s[j][:, :, None], (CMLP_GROUPS, CHUNK, CMLP_WIDTH // CMLP_GROUPS))
            h = _post_even(h, mv, mr, gu, vn, cm_w_s[j].astype(BF16), bsb, ev_w_out[j].astype(BF16), *norms, ctx_len)
        else:
            gq = jnp.tile(qn_g[j], Q_HEADS).reshape(1, nq)
            gk = jnp.tile(kn_g[j], LANES // HEAD_DIM).reshape(1, LANES)
            qc, qd, kc, vc, kd, vd = _in_odd(h, mv, od_w_in[j].astype(BF16), bd, gq, gk, cos_t, sin_t, ctx_len)
            skip = ctx_len // ATT_TQ if l == depth - 1 else 0
            yc = _global_attention(qc, kc, vc, ctx_len, skip)
            yd = _window_attention(qd, kd, vd, sink[j], ctx_len, skip)
            h = _post_odd(h, mv, yc, yd, od_w_out[j].astype(BF16), *norms, ctx_len, skip)
            if skip:
                rows_ctx = 0
    return h[:, rows_ctx:, :]
```

```python
import functools

import numpy as np
import jax
import jax.numpy as jnp
from jax import lax
from jax.experimental import pallas as pl
from jax.experimental.pallas import tpu as pltpu

F32 = jnp.float32
BF16 = jnp.bfloat16

D_MODEL = 1024
DEPTH = 4
GRID_W = 64
RNN_WIDTH = D_MODEL
RNN_HEADS = RNN_WIDTH // 128
CONV_W = 4
LRU_C = 8.0
CMLP_WIDTH = D_MODEL // 2
CMLP_GROUPS = 4
CHUNK = 128
HEAD_DIM = 64
Q_HEADS = 8
KV_HEADS = 2
GROUP = Q_HEADS // KV_HEADS
WINDOW = 128
ROPE_THETA = 10000.0
NEG_INF = -1e30
FFN_HIDDEN = 2816
FFN_CHUNKS = (1024, 1024, 768)
ALPHA = (2.0 * DEPTH) ** 0.25
LOG2E = 1.4426950408889634
EPS = 1e-6

LANES = 128
ROW_TILE = 256
SCAN_BLOCK = 128
SCAN_RUN = 4
ATT_TQ = 256
ATT_TK = 256
ATT_BIG = 1024
VMEM_LIMIT = 56 * 1024 * 1024


def _cparams(sem):
    return pltpu.CompilerParams(dimension_semantics=sem, vmem_limit_bytes=VMEM_LIMIT)


def _const_spec(shape):
    nd = len(shape)
    return pl.BlockSpec(shape, lambda *_: (0,) * nd, pipeline_mode=pl.Buffered(1))


def _gelu(x):
    return 0.5 * x * (1.0 + jnp.tanh(0.7978845608028654 * (x + 0.044715 * (x * x * x))))


def _normalise(x):
    mu = jnp.mean(x, axis=-1, keepdims=True)
    xc = x - mu
    var = jnp.mean(xc * xc, axis=-1, keepdims=True)
    return xc * lax.rsqrt(var + EPS)


def _mod_rows(m_ref, is_ctx, lat_row):
    return jnp.where(is_ctx, m_ref[lat_row + 8:lat_row + 9, :], m_ref[lat_row:lat_row + 1, :])


def _mod_kernel(c_ref, w_ref, b_ref, o_ref):
    c = c_ref[...]
    s = c * jax.nn.sigmoid(c)
    o_ref[...] = jnp.dot(s, w_ref[...], preferred_element_type=F32,
                         precision=lax.Precision.HIGHEST) + b_ref[...]


def _ada_vectors(c_rows, ada_w, ada_b):
    depth, d, n = ada_w.shape
    rows = c_rows.shape[0]
    nb = 1536
    return pl.pallas_call(
        _mod_kernel,
        grid=(depth, n // nb),
        in_specs=[pl.BlockSpec((rows, d), lambda l, j: (0, 0)),
                  pl.BlockSpec((None, d, nb), lambda l, j: (l, 0, j)),
                  pl.BlockSpec((None, 1, nb), lambda l, j: (l, 0, j))],
        out_specs=pl.BlockSpec((None, rows, nb), lambda l, j: (l, 0, j)),
        out_shape=jax.ShapeDtypeStruct((depth, rows, n), F32),
        compiler_params=_cparams(("parallel", "parallel")),
        name="ada_vectors",
    )(c_rows, ada_w, ada_b.reshape(depth, 1, n))


def _in_even_kernel(h_ref, m_ref, w_ref, gg_ref, xr_ref, gu_ref, vn_ref, *, ctx_len):
    tm = h_ref.shape[0]
    w = RNN_WIDTH
    for rows in _sub_tiles(tm):
        is_ctx = pl.program_id(1) * tm + rows.start < ctx_len
        sh = _mod_rows(m_ref, is_ctx, 0)
        sc = _mod_rows(m_ref, is_ctx, 1)
        a = (h_ref[rows, :] * (1.0 + sc) + sh).astype(BF16)
        gate = jnp.dot(a, w_ref[:, 0:w], preferred_element_type=F32)
        gg_ref[rows, :] = _gelu(gate).astype(BF16)
        xr_ref[rows, :] = jnp.dot(a, w_ref[:, w:2 * w], preferred_element_type=F32)
        u = jnp.dot(a, w_ref[:, 2 * w:2 * w + CMLP_WIDTH], preferred_element_type=F32)
        gu_ref[rows, :] = _gelu(u).astype(BF16)
        v = jnp.dot(a, w_ref[:, 2 * w + CMLP_WIDTH:], preferred_element_type=F32)
        vn_ref[rows, :] = _normalise(_gelu(v)).astype(BF16)


def _in_even(h, modv, w_in, ctx_len):
    b, s, d = h.shape
    tm = _post_tile(s, 0)
    n_in = w_in.shape[1]
    row = lambda width: pl.BlockSpec((None, tm, width), lambda bi, i: (bi, i, 0))
    return pl.pallas_call(
        functools.partial(_in_even_kernel, ctx_len=ctx_len),
        grid=(b, s // tm),
        in_specs=[row(d),
                  pl.BlockSpec((None, 16, d), lambda bi, i: (bi, 0, 0)),
                  _const_spec((d, n_in))],
        out_specs=[row(RNN_WIDTH), row(RNN_WIDTH), row(CMLP_WIDTH), row(CMLP_WIDTH)],
        out_shape=[jax.ShapeDtypeStruct((b, s, RNN_WIDTH), BF16),
                   jax.ShapeDtypeStruct((b, s, RNN_WIDTH), F32),
                   jax.ShapeDtypeStruct((b, s, CMLP_WIDTH), BF16),
                   jax.ShapeDtypeStruct((b, s, CMLP_WIDTH), BF16)],
        compiler_params=_cparams(("parallel", "parallel")),
        name="even_in_proj",
    )(h, modv, w_in)


def _block_scan(a_ref, b_ref, o_ref, t0, carry, reverse):
    n = SCAN_RUN
    sub = lax.broadcasted_iota(jnp.int32, (8, a_ref.shape[1]), 0)
    ngroup = a_ref.shape[0] // (8 * n)
    steps = range(n - 1, -1, -1) if reverse else range(n)
    for k in (range(ngroup - 1, -1, -1) if reverse else range(ngroup)):
        base = 8 * n * k
        av = [a_ref[pl.ds(base + i, 8, stride=n), :] for i in range(n)]
        bv = [b_ref[pl.ds(base + i, 8, stride=n), :] for i in range(n)]
        hloc, aloc = [None] * n, [None] * n
        prev = None
        for i in steps:
            hloc[i] = bv[i] if prev is None else av[i] * hloc[prev] + bv[i]
            aloc[i] = av[i] if prev is None else av[i] * aloc[prev]
            prev = i
        atot, htot = aloc[prev], hloc[prev]
        for d in (1, 2, 4):
            shift, ok = (8 - d, sub < 8 - d) if reverse else (d, sub >= d)
            htot = jnp.where(ok, htot + atot * pltpu.roll(htot, shift, 0), htot)
            atot = jnp.where(ok, atot * pltpu.roll(atot, shift, 0), atot)
        shift, ok = (7, sub < 7) if reverse else (1, sub >= 1)
        enter = (jnp.where(ok, pltpu.roll(htot, shift, 0), 0.0)
                 + jnp.where(ok, pltpu.roll(atot, shift, 0), 1.0) * carry)
        for i in range(n):
            o_ref[pl.ds(t0 + base + i, 8, stride=n), :] = hloc[i] + aloc[i] * enter
        last = 0 if reverse else 7
        carry = (jnp.broadcast_to(htot[last:last + 1, :], htot.shape)
                 + jnp.broadcast_to(atot[last:last + 1, :], atot.shape) * carry)
    return carry


def _scan_kernel(xr_ref, gg_ref, cw_ref, cb_ref, gw_ref, gb_ref, lam_ref, out_ref, xc_ref, rf_ref, rr_ref,
                 ab0_ref, ab1_ref, g0_ref, g1_ref, *, ctx_len, tb):
    s = xr_ref.shape[0]
    nblk = s // tb
    ncb = ctx_len // tb
    cw = cw_ref[...]
    cb = cb_ref[...]
    row = lax.broadcasted_iota(jnp.int32, (tb, LANES), 0)

    def conv(blk, _):
        t0 = pl.multiple_of(blk * tb, tb)
        x = xr_ref[pl.ds(t0, tb), :]
        prev = xr_ref[pl.ds(pl.multiple_of(jnp.maximum(t0 - 8, 0), 8), 8), :]
        nxt = xr_ref[pl.ds(pl.multiple_of(jnp.minimum(t0 + tb, s - 8), 8), 8), :]
        pf = jnp.where((blk == 0) | (blk == ncb), 0.0, 1.0)
        nf = jnp.where((blk == ncb - 1) | (blk == nblk - 1), 0.0, 1.0)
        p6 = prev[6:7, :] * pf
        p7 = prev[7:8, :] * pf
        n0 = nxt[0:1, :] * nf
        xm1 = jnp.where(row == 0, p7, pltpu.roll(x, 1, 0))
        xm2 = jnp.where(row == 0, p6, jnp.where(row == 1, p7, pltpu.roll(x, 2, 0)))
        xp1 = jnp.where(row == tb - 1, n0, pltpu.roll(x, tb - 1, 0))
        xc_ref[pl.ds(t0, tb), :] = (xm2 * cw[0:1, :] + xm1 * cw[1:2, :] + x * cw[2:3, :] + xp1 * cw[3:4, :]
                                    + cb)
        return 0

    lax.fori_loop(0, nblk, conv, 0)

    def block_start(step, d):
        blk = step if d == 0 else jnp.where(step < ncb, ncb - 1 - step, nblk - 1 - (step - ncb))
        return pl.multiple_of(blk * tb, tb)

    def gate_matmuls(step, g_ref):
        step = jnp.minimum(step, nblk - 1)
        for d in range(2):
            xc = xc_ref[pl.ds(block_start(step, d), tb), :]
            g_ref[d] = jnp.dot(xc.astype(BF16), gw_ref[d], preferred_element_type=F32)

    def stash_coeffs(step, g_ref, ab_ref):
        step = jnp.minimum(step, nblk - 1)
        for d in range(2):
            xc = xc_ref[pl.ds(block_start(step, d), tb), :]
            g = g_ref[d] + gb_ref[d]
            r = jax.nn.sigmoid(g[:, :LANES])
            ig = jax.nn.sigmoid(g[:, LANES:])
            z = -lam_ref[d:d + 1, :]
            softplus = jnp.maximum(z, 0.0) + jnp.log(1.0 + jnp.exp(-jnp.abs(z)))
            a = jnp.exp2(r * ((-LRU_C * LOG2E) * softplus))
            ab_ref[2 * d] = a
            ab_ref[2 * d + 1] = jnp.sqrt(1.0 - a * a) * (ig * xc)

    def scan(step, ab_ref, cf, cr):
        cf = _block_scan(ab_ref.at[0], ab_ref.at[1], rf_ref, block_start(step, 0), cf, False)
        cr = _block_scan(ab_ref.at[2], ab_ref.at[3], rr_ref, block_start(step, 1), cr, True)
        return cf, cr

    gate_matmuls(0, g0_ref)
    stash_coeffs(0, g0_ref, ab0_ref)
    gate_matmuls(1, g1_ref)
    gate_matmuls(2, g0_ref)

    def two_steps(u, carry):
        cf, cr = carry
        cf, cr = scan(2 * u, ab0_ref, cf, cr)
        stash_coeffs(2 * u + 1, g1_ref, ab1_ref)
        cf, cr = scan(2 * u + 1, ab1_ref, cf, cr)
        stash_coeffs(2 * u + 2, g0_ref, ab0_ref)
        gate_matmuls(2 * u + 3, g1_ref)
        gate_matmuls(2 * u + 4, g0_ref)
        return cf, cr

    zero = jnp.zeros((8, LANES), F32)
    lax.fori_loop(0, nblk // 2, two_steps, (zero, zero))

    def combine(j, _):
        t0 = pl.multiple_of(j * tb, tb)
        rec = rf_ref[pl.ds(t0, tb), :] + rr_ref[pl.ds(t0, tb), :]
        out_ref[pl.ds(t0, tb), :] = (gg_ref[pl.ds(t0, tb), :].astype(F32) * rec).astype(BF16)
        return 0

    lax.fori_loop(0, nblk, combine, 0)


def _rglru(xr, gg, conv_w, conv_b, gate_w, gate_b, lam, ctx_len):
    b, s, w = xr.shape
    nh = w // LANES
    assert (s // SCAN_BLOCK) % 2 == 0 and ctx_len % SCAN_BLOCK == 0
    col = lambda dt: pl.BlockSpec((None, s, LANES), lambda bi, hd: (bi, 0, hd))
    return pl.pallas_call(
        functools.partial(_scan_kernel, ctx_len=ctx_len, tb=SCAN_BLOCK),
        grid=(b, nh),
        in_specs=[col(F32), col(BF16),
                  pl.BlockSpec((CONV_W, LANES), lambda bi, hd: (0, hd)),
                  pl.BlockSpec((1, LANES), lambda bi, hd: (0, hd)),
                  pl.BlockSpec((None, 2, LANES, 2 * LANES), lambda bi, hd: (hd, 0, 0, 0)),
                  pl.BlockSpec((None, 2, 1, 2 * LANES), lambda bi, hd: (hd, 0, 0, 0)),
                  pl.BlockSpec((2, LANES), lambda bi, hd: (0, hd))],
        out_specs=col(BF16),
        out_shape=jax.ShapeDtypeStruct((b, s, w), BF16),
        scratch_shapes=[pltpu.VMEM((s, LANES), F32), pltpu.VMEM((s, LANES), F32), pltpu.VMEM((s, LANES), F32),
                        pltpu.VMEM((4, SCAN_BLOCK, LANES), F32), pltpu.VMEM((4, SCAN_BLOCK, LANES), F32),
                        pltpu.VMEM((2, SCAN_BLOCK, 2 * LANES), F32), pltpu.VMEM((2, SCAN_BLOCK, 2 * LANES), F32)],
        compiler_params=_cparams(("parallel", "parallel")),
        name="rglru_scan",
    )(xr, gg, conv_w, conv_b, gate_w, gate_b, lam)


def _residual_ln(h, y, gate, g, b):
    return _normalise(ALPHA * h + gate * y) * g + b


def _swiglu(a, wi_ref, wo_ref):
    acc = None
    c0 = 0
    for width in FFN_CHUNKS:
        zg = jnp.dot(a, wi_ref[:, c0:c0 + width], preferred_element_type=F32)
        zu = jnp.dot(a, wi_ref[:, FFN_HIDDEN + c0:FFN_HIDDEN + c0 + width], preferred_element_type=F32)
        hm = (zg * jax.nn.sigmoid(zg) * zu).astype(BF16)
        y = jnp.dot(hm, wo_ref[c0:c0 + width, :], preferred_element_type=F32)
        acc = y if acc is None else acc + y
        c0 += width
    return acc


def _residual_pair(h, y_mix, m_ref, is_ctx, l1g_ref, l1b_ref, wi_ref, wo_ref, l2g_ref, l2b_ref):
    h1 = _residual_ln(h, y_mix, _mod_rows(m_ref, is_ctx, 2), l1g_ref[...], l1b_ref[...])
    a = (h1 * (1.0 + _mod_rows(m_ref, is_ctx, 4)) + _mod_rows(m_ref, is_ctx, 3)).astype(BF16)
    return _residual_ln(h1, _swiglu(a, wi_ref, wo_ref), _mod_rows(m_ref, is_ctx, 5), l2g_ref[...], l2b_ref[...])


def _sub_tiles(tm):
    return [slice(r * ROW_TILE, (r + 1) * ROW_TILE) for r in range(tm // ROW_TILE)]


def _post_tile(n_rows, first_tile):
    return 3 * ROW_TILE if first_tile == 0 and n_rows % (3 * ROW_TILE) == 0 else ROW_TILE


def _post_even_kernel(h_ref, m_ref, mr_ref, gu_ref, vn_ref, ws_ref, bs_ref, wm_ref, l1g_ref, l1b_ref,
                      wi_ref, wo_ref, l2g_ref, l2b_ref, o_ref, gm_ref, *, ctx_len):
    tm = h_ref.shape[0]
    gw = CMLP_WIDTH // CMLP_GROUPS
    for rows in _sub_tiles(tm):
        is_ctx = pl.program_id(1) * tm + rows.start < ctx_len
        for c in range(rows.start, rows.stop, CHUNK):
            crow = slice(c, c + CHUNK)
            for g in range(CMLP_GROUPS):
                cols = slice(g * gw, (g + 1) * gw)
                mixed = jnp.dot(ws_ref[g], vn_ref[crow, cols], preferred_element_type=F32) + bs_ref[g]
                gm_ref[crow, cols] = (gu_ref[crow, cols].astype(F32) * mixed).astype(BF16)
        y = jnp.dot(mr_ref[rows, :], wm_ref[0:RNN_WIDTH, :], preferred_element_type=F32)
        y = y + jnp.dot(gm_ref[rows, :], wm_ref[RNN_WIDTH:, :], preferred_element_type=F32)
        o_ref[rows, :] = _residual_pair(h_ref[rows, :], y, m_ref, is_ctx, l1g_ref, l1b_ref,
                                        wi_ref, wo_ref, l2g_ref, l2b_ref)


def _post_even(h, modv, mr, gu, vn, ws, bsb, w_mix, l1g, l1b, w_in, w_out, l2g, l2b, ctx_len):
    b, s, d = h.shape
    tm = _post_tile(s, 0)
    row = lambda width: pl.BlockSpec((None, tm, width), lambda bi, i: (bi, i, 0))
    vec = _const_spec((1, d))
    return pl.pallas_call(
        functools.partial(_post_even_kernel, ctx_len=ctx_len),
        grid=(b, s // tm),
        in_specs=[row(d),
                  pl.BlockSpec((None, 16, d), lambda bi, i: (bi, 0, 0)),
                  row(RNN_WIDTH), row(CMLP_WIDTH), row(CMLP_WIDTH),
                  _const_spec(ws.shape), _const_spec(bsb.shape), _const_spec(w_mix.shape), vec, vec,
                  _const_spec(w_in.shape), _const_spec(w_out.shape), vec, vec],
        out_specs=row(d),
        out_shape=jax.ShapeDtypeStruct((b, s, d), F32),
        scratch_shapes=[pltpu.VMEM((tm, CMLP_WIDTH), BF16)],
        compiler_params=_cparams(("parallel", "parallel")),
        name="even_out_ffn",
    )(h, modv, mr, gu, vn, ws, bsb, w_mix, l1g, l1b, w_in, w_out, l2g, l2b)


def _post_odd_kernel(h_ref, m_ref, yc_ref, yd_ref, wm_ref, l1g_ref, l1b_ref, wi_ref, wo_ref, l2g_ref, l2b_ref,
                     o_ref, *, ctx_len, first_tile):
    tm = h_ref.shape[0]
    half = yc_ref.shape[1]
    for rows in _sub_tiles(tm):
        is_ctx = pl.program_id(1) * tm + first_tile * ROW_TILE + rows.start < ctx_len
        y = jnp.dot(yc_ref[rows, :], wm_ref[0:half, :], preferred_element_type=F32)
        y = y + jnp.dot(yd_ref[rows, :], wm_ref[half:, :], preferred_element_type=F32)
        o_ref[rows, :] = _residual_pair(h_ref[rows, :], y, m_ref, is_ctx, l1g_ref, l1b_ref,
                                        wi_ref, wo_ref, l2g_ref, l2b_ref)


def _post_odd(h, modv, yc, yd, w_mix, l1g, l1b, w_in, w_out, l2g, l2b, ctx_len, first_tile):
    b, s, d = h.shape
    tm = _post_tile(s, first_tile)
    n = (s - first_tile * ROW_TILE) // tm
    assert yc.shape[1] == n * tm and yd.shape[1] == n * tm
    row = lambda width: pl.BlockSpec((None, tm, width), lambda bi, i: (bi, i, 0))
    vec = _const_spec((1, d))
    return pl.pallas_call(
        functools.partial(_post_odd_kernel, ctx_len=ctx_len, first_tile=first_tile),
        grid=(b, n),
        in_specs=[pl.BlockSpec((None, tm, d), lambda bi, i: (bi, i + first_tile, 0)),
                  pl.BlockSpec((None, 16, d), lambda bi, i: (bi, 0, 0)),
                  row(yc.shape[2]), row(yd.shape[2]),
                  _const_spec(w_mix.shape), vec, vec,
                  _const_spec(w_in.shape), _const_spec(w_out.shape), vec, vec],
        out_specs=row(d),
        out_shape=jax.ShapeDtypeStruct((b, n * tm, d), F32),
        compiler_params=_cparams(("parallel", "parallel")),
        name="odd_out_ffn",
    )(h, modv, yc, yd, w_mix, l1g, l1b, w_in, w_out, l2g, l2b)


def _in_odd_kernel(h_ref, m_ref, w_ref, bd_ref, gq_ref, gk_ref, cos_ref, sin_ref,
                   qc_ref, qd_ref, kc_ref, vc_ref, kd_ref, vd_ref, *, ctx_len):
    tm = h_ref.shape[0]
    is_ctx = pl.program_id(1) * tm < ctx_len
    sh = _mod_rows(m_ref, is_ctx, 0)
    sc = _mod_rows(m_ref, is_ctx, 1)
    a = (h_ref[...] * (1.0 + sc) + sh).astype(BF16)
    cos = cos_ref[...]
    sin = sin_ref[...]
    lane = lax.broadcasted_iota(jnp.int32, (tm, LANES), 1)
    first = (lane & 31) < 16
    scale = HEAD_DIM ** -0.5

    def rope(x):
        partner = jnp.where(first, pltpu.roll(x, LANES - 16, 1), pltpu.roll(x, 16, 1))
        return x * cos + partner * sin

    def rms(x, gain, width):
        ms = jnp.dot((x * x).astype(BF16), bd_ref[0:width, 0:width], preferred_element_type=F32)
        return x * lax.rsqrt(ms + EPS) * gain

    nq = Q_HEADS * HEAD_DIM
    cq = rms(jnp.dot(a, w_ref[:, 0:nq], preferred_element_type=F32), gq_ref[...], nq)
    for m in range(nq // LANES):
        qc_ref[m * LANES:(m + 1) * LANES, :] = (rope(cq[:, m * LANES:(m + 1) * LANES]) * (scale * LOG2E)).T.astype(BF16)
    dq = jnp.dot(a, w_ref[:, nq:2 * nq], preferred_element_type=F32)
    for m in range(nq // LANES):
        qd_ref[m * LANES:(m + 1) * LANES, :] = (rope(dq[:, m * LANES:(m + 1) * LANES]) * (scale * LOG2E)).T.astype(BF16)
    kv = jnp.dot(a, w_ref[:, 2 * nq:], preferred_element_type=F32)
    kc_ref[...] = rope(rms(kv[:, 0:LANES], gk_ref[...], LANES)).astype(BF16)
    kd_ref[...] = rope(kv[:, 2 * LANES:3 * LANES]).astype(BF16)
    rowi = lax.broadcasted_iota(jnp.int32, (LANES, tm), 0)
    pad = jnp.where(rowi == HEAD_DIM, 1.0, 0.0)

    def value_rows(v):
        vt = v.T
        return (jnp.where(rowi < HEAD_DIM, vt, pad).astype(BF16),
                jnp.where(rowi < HEAD_DIM, jnp.concatenate([vt[HEAD_DIM:], vt[:HEAD_DIM]], axis=0), pad).astype(BF16))

    vc_ref[0], vc_ref[1] = value_rows(kv[:, LANES:2 * LANES])
    vd = value_rows(kv[:, 3 * LANES:4 * LANES])
    for jj in range(KV_HEADS):
        for c in range(tm // LANES):
            vd_ref[jj, c] = vd[jj][:, c * LANES:(c + 1) * LANES]


def _in_odd(h, modv, w_in, bd, gq, gk, cos_t, sin_t, ctx_len):
    b, s, d = h.shape
    tm = ROW_TILE
    nq = Q_HEADS * HEAD_DIM
    row = lambda width: pl.BlockSpec((None, tm, width), lambda bi, i: (bi, i, 0))
    tab = pl.BlockSpec((tm, LANES), lambda bi, i: (i, 0))
    qt = pl.BlockSpec((None, nq, tm), lambda bi, i: (bi, 0, i))
    sds = lambda width: jax.ShapeDtypeStruct((b, s, width), BF16)
    return pl.pallas_call(
        functools.partial(_in_odd_kernel, ctx_len=ctx_len),
        grid=(b, s // tm),
        in_specs=[row(d),
                  pl.BlockSpec((None, 16, d), lambda bi, i: (bi, 0, 0)),
                  _const_spec(w_in.shape), _const_spec(bd.shape),
                  _const_spec(gq.shape), _const_spec(gk.shape), tab, tab],
        out_specs=[qt, qt, row(LANES),
                   pl.BlockSpec((None, KV_HEADS, None, LANES, tm), lambda bi, i: (bi, 0, i, 0, 0)),
                   row(LANES),
                   pl.BlockSpec((None, KV_HEADS, tm // LANES, LANES, LANES), lambda bi, i: (bi, 0, i, 0, 0))],
        out_shape=[jax.ShapeDtypeStruct((b, nq, s), BF16), jax.ShapeDtypeStruct((b, nq, s), BF16), sds(LANES),
                   jax.ShapeDtypeStruct((b, KV_HEADS, s // tm, LANES, tm), BF16), sds(LANES),
                   jax.ShapeDtypeStruct((b, KV_HEADS, s // LANES, LANES, LANES), BF16)],
        compiler_params=_cparams(("parallel", "parallel")),
        name="odd_in_proj",
    )(h, modv, w_in, bd, gq, gk, cos_t, sin_t)


def _stack_queries(q_ref, qs_ref, j, tq):
    for hh in range(GROUP):
        qh = q_ref[hh * HEAD_DIM:(hh + 1) * HEAD_DIM, :]
        zero = jnp.zeros_like(qh)
        qs_ref[0:HEAD_DIM, hh * tq:(hh + 1) * tq] = jnp.where(j == 0, qh, zero)
        qs_ref[HEAD_DIM:, hh * tq:(hh + 1) * tq] = jnp.where(j == 0, zero, qh)


def _store_heads(o, o_ref, tq):
    for m in range(GROUP // 2):
        pair = jnp.concatenate([o[:, (2 * m) * tq:(2 * m + 1) * tq],
                                o[:, (2 * m + 1) * tq:(2 * m + 2) * tq]], axis=0)
        o_ref[:, m * LANES:(m + 1) * LANES] = pair.T.astype(o_ref.dtype)


def _gattn_kernel(q_ref, k_ref, v_ref, o_ref, *scratch, ctx_len, first_tile):
    tq = q_ref.shape[1]
    gw = GROUP * HEAD_DIM
    _, ntile, _, tv = v_ref.shape
    chains = [scratch[6 * j:6 * j + 6] for j in range(KV_HEADS)]
    big = chains[0][4].shape[0]
    nbig = (ntile * tv - ctx_len) // big
    qi = pl.program_id(1) + first_tile

    def scores(j, start, s_ref):
        if not isinstance(start, int):
            start = pl.multiple_of(start, tv)
        s_ref[...] = jnp.dot(k_ref[pl.ds(start, s_ref.shape[0]), :], chains[j][0][...],
                             preferred_element_type=F32)

    def update(j, start, s_ref):
        m_ref, acc_ref = chains[j][1], chains[j][2]
        sc = s_ref[...]
        m_prev = m_ref[...]
        m_new = jnp.maximum(m_prev, jnp.max(sc, axis=0, keepdims=True))
        alpha = jnp.exp2(m_prev - m_new)
        p = jnp.exp2(sc - m_new).astype(BF16)
        t0 = start // tv
        vt = jnp.concatenate([v_ref[j, t0 + i] for i in range(s_ref.shape[0] // tv)], axis=1)
        acc_ref[...] = alpha * acc_ref[...] + jnp.dot(vt, p, preferred_element_type=F32)
        m_ref[...] = m_new

    def latent(n):
        return ctx_len + n * big

    for j, (qs_ref, m_ref, acc_ref, sc_ref, _, _) in enumerate(chains):
        _stack_queries(q_ref.at[j * gw:(j + 1) * gw, :], qs_ref, j, tq)
        m_ref[...] = jnp.full(m_ref.shape, NEG_INF, F32)
        acc_ref[...] = jnp.zeros(acc_ref.shape, F32)
        scores(j, 0, sc_ref)

    @pl.when(qi * tq >= ctx_len)
    def _():
        for j, (_, _, _, sc_ref, sa_ref, _) in enumerate(chains):
            scores(j, latent(0), sa_ref)
            update(j, 0, sc_ref)

        def pair(u, _):
            for j, (_, _, _, _, sa_ref, sb_ref) in enumerate(chains):
                scores(j, latent(2 * u + 1), sb_ref)
                update(j, latent(2 * u), sa_ref)
            for j, (_, _, _, _, sa_ref, sb_ref) in enumerate(chains):
                scores(j, latent(2 * u + 2), sa_ref)
                update(j, latent(2 * u + 1), sb_ref)
            return 0

        lax.fori_loop(0, (nbig - 1) // 2, pair, 0)
        for j, (_, _, _, _, sa_ref, sb_ref) in enumerate(chains):
            if nbig % 2 == 0:
                scores(j, latent(nbig - 1), sb_ref)
                update(j, latent(nbig - 2), sa_ref)
                update(j, latent(nbig - 1), sb_ref)
            else:
                update(j, latent(nbig - 1), sa_ref)

    @pl.when(qi * tq < ctx_len)
    def _():
        for j in range(KV_HEADS):
            update(j, 0, chains[j][3])

    for j in range(KV_HEADS):
        acc_ref = chains[j][2]
        _store_heads(acc_ref[0:HEAD_DIM, :] / acc_ref[HEAD_DIM:HEAD_DIM + 1, :],
                     o_ref.at[:, j * gw:(j + 1) * gw], tq)


def _global_attention(qt, k, vt, ctx_len, first_tile):
    b, nq, s = qt.shape
    tq = ATT_TQ
    assert vt.shape[4] == ATT_TK and ctx_len % ATT_TK == 0 and (s - ctx_len) % ATT_BIG == 0
    cols = GROUP * tq
    per_head = [pltpu.VMEM((LANES, cols), BF16), pltpu.VMEM((1, cols), F32), pltpu.VMEM((LANES, cols), F32),
                pltpu.VMEM((ctx_len, cols), F32), pltpu.VMEM((ATT_BIG, cols), F32),
                pltpu.VMEM((ATT_BIG, cols), F32)]
    return pl.pallas_call(
        functools.partial(_gattn_kernel, ctx_len=ctx_len, first_tile=first_tile),
        grid=(b, s // tq - first_tile),
        in_specs=[pl.BlockSpec((None, nq, tq), lambda bi, i: (bi, 0, i + first_tile)),
                  pl.BlockSpec((None, s, LANES), lambda bi, i: (bi, 0, 0)),
                  pl.BlockSpec((None,) + vt.shape[1:], lambda bi, i: (bi, 0, 0, 0, 0))],
        out_specs=pl.BlockSpec((None, tq, nq), lambda bi, i: (bi, i, 0)),
        out_shape=jax.ShapeDtypeStruct((b, s - first_tile * tq, nq), BF16),
        scratch_shapes=per_head * KV_HEADS,
        compiler_params=_cparams(("parallel", "arbitrary")),
        name="global_attention",
    )(qt, k, vt)


def _wattn_kernel(q_ref, k_ref, v_ref, sink_ref, bias_ref, o_ref, *qs_refs, ctx_len, span, first_tile):
    tq = q_ref.shape[1]
    s = k_ref.shape[0]
    gw = GROUP * HEAD_DIM
    qi = pl.program_id(1) + first_tile
    is_ctx = qi * tq < ctx_len

    def head(j, window_start):
        qs_ref = qs_refs[j]
        _stack_queries(q_ref.at[j * gw:(j + 1) * gw, :], qs_ref, j, tq)
        qs = qs_ref[...]
        sink = sink_ref[j] * LOG2E
        s_ctx = jnp.dot(k_ref[0:ctx_len, :], qs, preferred_element_type=F32)
        v_ctx = jnp.concatenate([v_ref[j, t] for t in range(ctx_len // LANES)], axis=1)
        m = jnp.maximum(jnp.max(s_ctx, axis=0, keepdims=True), sink)
        if window_start is not None:
            s_win = (jnp.dot(k_ref[pl.ds(window_start, span), :], qs, preferred_element_type=F32)
                     + bias_ref[...])
            m = jnp.maximum(m, jnp.max(s_win, axis=0, keepdims=True))
        acc = jnp.dot(v_ctx, jnp.exp2(s_ctx - m).astype(BF16), preferred_element_type=F32)
        if window_start is not None:
            t0 = window_start // LANES
            v_win = jnp.concatenate([v_ref[j, t0 + i] for i in range(span // LANES)], axis=1)
            acc = acc + jnp.dot(v_win, jnp.exp2(s_win - m).astype(BF16), preferred_element_type=F32)
        den = acc[HEAD_DIM:HEAD_DIM + 1, :] + jnp.exp2(sink - m)
        _store_heads(acc[0:HEAD_DIM, :] / den, o_ref.at[:, j * gw:(j + 1) * gw], tq)

    @pl.when(is_ctx)
    def _():
        for j in range(KV_HEADS):
            head(j, None)

    @pl.when(jnp.logical_not(is_ctx))
    def _():
        start = pl.multiple_of(jnp.clip(qi * tq - WINDOW, ctx_len, s - span), LANES)
        for j in range(KV_HEADS):
            head(j, start)


def _window_attention(qt, k, vt, sink, ctx_len, first_tile):
    b, nq, s = qt.shape
    tq = ATT_TQ
    gw = GROUP * HEAD_DIM
    span = tq + 2 * WINDOW
    assert tq >= WINDOW and (s - ctx_len) // tq >= 2
    sink_row = jnp.repeat(sink.reshape(KV_HEADS, 1, GROUP), tq, axis=2)
    r = jnp.arange(span)[:, None]
    qcol = (jnp.arange(GROUP * tq) % tq)[None, :]
    bias = jnp.stack([jnp.where(jnp.abs(off + r - qcol) <= WINDOW, 0.0, NEG_INF)
                      for off in (0, -WINDOW, tq - span)]).astype(F32)

    def placement(i):
        lo = (i + first_tile) * tq - WINDOW
        return jnp.where(lo < ctx_len, 0, jnp.where(lo > s - span, 2, 1))

    return pl.pallas_call(
        functools.partial(_wattn_kernel, ctx_len=ctx_len, span=span, first_tile=first_tile),
        grid=(b, s // tq - first_tile),
        in_specs=[pl.BlockSpec((None, nq, tq), lambda bi, i: (bi, 0, i + first_tile)),
                  pl.BlockSpec((None, s, LANES), lambda bi, i: (bi, 0, 0)),
                  pl.BlockSpec((None,) + vt.shape[1:], lambda bi, i: (bi, 0, 0, 0, 0)),
                  pl.BlockSpec((KV_HEADS, 1, GROUP * tq), lambda bi, i: (0, 0, 0)),
                  pl.BlockSpec((None, span, GROUP * tq), lambda bi, i: (placement(i), 0, 0))],
        out_specs=pl.BlockSpec((None, tq, nq), lambda bi, i: (bi, i, 0)),
        out_shape=jax.ShapeDtypeStruct((b, s - first_tile * tq, nq), BF16),
        scratch_shapes=[pltpu.VMEM((LANES, GROUP * tq), BF16)] * KV_HEADS,
        compiler_params=_cparams(("parallel", "parallel")),
        name="window_attention",
    )(qt, k, vt, sink_row, bias)


def _rope_tables(n_lat, ctx_len):
    rows = n_lat // GRID_W
    row = jnp.repeat(jnp.arange(rows, dtype=F32), GRID_W)
    col = jnp.tile(jnp.arange(GRID_W, dtype=F32), rows)
    nf = HEAD_DIM // 4
    inv_freq = ROPE_THETA ** (-jnp.arange(nf, dtype=F32) / nf)
    ang_r = row[:, None] * inv_freq
    ang_c = col[:, None] * inv_freq
    cos = jnp.concatenate([jnp.cos(ang_r)] * 2 + [jnp.cos(ang_c)] * 2, axis=1)
    sin = jnp.concatenate([-jnp.sin(ang_r), jnp.sin(ang_r), -jnp.sin(ang_c), jnp.sin(ang_c)], axis=1)
    cos = jnp.concatenate([jnp.ones((ctx_len, HEAD_DIM), F32), cos], axis=0)
    sin = jnp.concatenate([jnp.zeros((ctx_len, HEAD_DIM), F32), sin], axis=0)
    return jnp.tile(cos, (1, LANES // HEAD_DIM)), jnp.tile(sin, (1, LANES // HEAD_DIM))


def kernel(x, c, ctx, c_ctx, ada_w, ada_b, ln1_g, ln1_b, ln2_g, ln2_b, ffn_w_in, ffn_w_out,
           ev_w_in, ev_w_out, rg_conv_w, rg_conv_b, rg_gate_w, rg_gate_b, rg_lambda, cm_w_s, cm_b_s,
           od_w_in, od_w_out, qn_g, kn_g, sink):
    b, t, d = x.shape
    ctx_len = ctx.shape[1]
    s = ctx_len + t
    depth = ada_w.shape[0]
    assert d == D_MODEL and depth == DEPTH and b + 1 <= 16
    assert ctx_len % ROW_TILE == 0 and t % ROW_TILE == 0 and t % GRID_W == 0 and ATT_TQ == ROW_TILE

    c_rows = jnp.zeros((16, d), F32).at[:b].set(c).at[b].set(c_ctx)
    mods = _ada_vectors(c_rows, ada_w, ada_b).reshape(depth, 16, 6, d)
    modv = jnp.zeros((depth, b, 16, d), F32)
    modv = modv.at[:, :, 0:6].set(mods[:, :b])
    modv = modv.at[:, :, 8:14].set(jnp.broadcast_to(mods[:, b][:, None], (depth, b, 6, d)))

    cos_t, sin_t = _rope_tables(t, ctx_len)
    nq = Q_HEADS * HEAD_DIM
    bd = jnp.kron(jnp.eye(nq // HEAD_DIM, dtype=F32), jnp.full((HEAD_DIM, HEAD_DIM), 1.0 / HEAD_DIM, F32)).astype(BF16)
    assert sum(FFN_CHUNKS) == FFN_HIDDEN

    h = jnp.concatenate([ctx, x], axis=1)
    rows_ctx = ctx_len
    for l in range(depth):
        j = l // 2
        mv = modv[l]
        norms = (ln1_g[l].reshape(1, d), ln1_b[l].reshape(1, d), ffn_w_in[l].astype(BF16),
                 ffn_w_out[l].astype(BF16), ln2_g[l].reshape(1, d), ln2_b[l].reshape(1, d))
        if l % 2 == 0:
            gg, xr, gu, vn = _in_even(h, mv, ev_w_in[j].astype(BF16), ctx_len)
            gw = jnp.transpose(rg_gate_w[j], (2, 0, 3, 1, 4)).reshape(RNN_HEADS, 2, LANES, 2 * LANES).astype(BF16)
            gb = jnp.transpose(rg_gate_b[j].reshape(2, 2, RNN_HEADS, LANES), (2, 0, 1, 3)).reshape(RNN_HEADS, 2, 1, 2 * LANES)
            mr = _rglru(xr, gg, rg_conv_w[j], rg_conv_b[j].reshape(1, -1), gw, gb, rg_lambda[j], ctx_len)
            bsb = jnp.broadcast_to(cm_b_s[j][:, :, None], (CMLP_GROUPS, CHUNK, CMLP_WIDTH // CMLP_GROUPS))
            h = _post_even(h, mv, mr, gu, vn, cm_w_s[j].astype(BF16), bsb, ev_w_out[j].astype(BF16), *norms, ctx_len)
        else:
            gq = jnp.tile(qn_g[j], Q_HEADS).reshape(1, nq)
            gk = jnp.tile(kn_g[j], LANES // HEAD_DIM).reshape(1, LANES)
            qc, qd, kc, vc, kd, vd = _in_odd(h, mv, od_w_in[j].astype(BF16), bd, gq, gk, cos_t, sin_t, ctx_len)
            skip = ctx_len // ATT_TQ if l == depth - 1 else 0
            yc = _global_attention(qc, kc, vc, ctx_len, skip)
            yd = _window_attention(qd, kd, vd, sink[j], ctx_len, skip)
            h = _post_odd(h, mv, yc, yd, od_w_out[j].astype(BF16), *norms, ctx_len, skip)
            if skip:
                rows_ctx = 0
    return h[:, rows_ctx:, :]
```

```python
import functools

import jax
import jax.numpy as jnp
from jax import lax
from jax.experimental import pallas as pl
from jax.experimental.pallas import tpu as pltpu

F32 = jnp.float32
BF16 = jnp.bfloat16

D_MODEL = 1024
DEPTH = 4
GRID_W = 64
RNN_WIDTH = D_MODEL
RNN_HEADS = RNN_WIDTH // 128
CONV_W = 4
LRU_C = 8.0
CMLP_WIDTH = D_MODEL // 2
CMLP_GROUPS = 4
CHUNK = 128
HEAD_DIM = 64
Q_HEADS = 8
KV_HEADS = 2
GROUP = Q_HEADS // KV_HEADS
WINDOW = 128
ROPE_THETA = 10000.0
NEG_INF = -1e30
FFN_HIDDEN = 2816
FFN_CHUNKS = (1024, 1024, 768)
ALPHA = (2.0 * DEPTH) ** 0.25
LOG2E = 1.4426950408889634
EPS = 1e-6

LANES = 128
ROW_TILE = 256
SCAN_BLOCK = 128
SCAN_RUN = 4
ATT_TQ = 256
ATT_TK = 256
ATT_BIG = 1024
VMEM_LIMIT = 56 * 1024 * 1024


def _cparams(sem):
    return pltpu.CompilerParams(dimension_semantics=sem, vmem_limit_bytes=VMEM_LIMIT)


def _const_spec(shape):
    nd = len(shape)
    return pl.BlockSpec(shape, lambda *_: (0,) * nd, pipeline_mode=pl.Buffered(1))


def _gelu(x):
    return 0.5 * x * (1.0 + jnp.tanh(0.7978845608028654 * (x + 0.044715 * (x * x * x))))


def _normalise(x):
    mu = jnp.mean(x, axis=-1, keepdims=True)
    xc = x - mu
    var = jnp.mean(xc * xc, axis=-1, keepdims=True)
    return xc * lax.rsqrt(var + EPS)


def _mod_rows(m_ref, is_ctx, lat_row):
    return jnp.where(is_ctx, m_ref[lat_row + 8:lat_row + 9, :], m_ref[lat_row:lat_row + 1, :])


def _mod_kernel(c_ref, w_ref, b_ref, o_ref):
    c = c_ref[...]
    s = c * jax.nn.sigmoid(c)
    o_ref[...] = jnp.dot(s, w_ref[...], preferred_element_type=F32,
                         precision=lax.Precision.HIGHEST) + b_ref[...]


def _ada_vectors(c_rows, ada_w, ada_b):
    depth, d, n = ada_w.shape
    rows = c_rows.shape[0]
    nb = 1536
    return pl.pallas_call(
        _mod_kernel,
        grid=(depth, n // nb),
        in_specs=[pl.BlockSpec((rows, d), lambda l, j: (0, 0)),
                  pl.BlockSpec((None, d, nb), lambda l, j: (l, 0, j)),
                  pl.BlockSpec((None, 1, nb), lambda l, j: (l, 0, j))],
        out_specs=pl.BlockSpec((None, rows, nb), lambda l, j: (l, 0, j)),
        out_shape=jax.ShapeDtypeStruct((depth, rows, n), F32),
        compiler_params=_cparams(("parallel", "parallel")),
        name="ada_vectors",
    )(c_rows, ada_w, ada_b.reshape(depth, 1, n))


def _in_even_kernel(h_ref, m_ref, w_ref, gg_ref, xr_ref, gu_ref, vn_ref, *, ctx_len):
    tm = h_ref.shape[0]
    w = RNN_WIDTH
    for rows in _sub_tiles(tm):
        is_ctx = pl.program_id(1) * tm + rows.start < ctx_len
        sh = _mod_rows(m_ref, is_ctx, 0)
        sc = _mod_rows(m_ref, is_ctx, 1)
        a = (h_ref[rows, :] * (1.0 + sc) + sh).astype(BF16)
        gate = jnp.dot(a, w_ref[:, 0:w], preferred_element_type=F32)
        gg_ref[rows, :] = _gelu(gate).astype(BF16)
        xr_ref[rows, :] = jnp.dot(a, w_ref[:, w:2 * w], preferred_element_type=F32)
        u = jnp.dot(a, w_ref[:, 2 * w:2 * w + CMLP_WIDTH], preferred_element_type=F32)
        gu_ref[rows, :] = _gelu(u).astype(BF16)
        v = jnp.dot(a, w_ref[:, 2 * w + CMLP_WIDTH:], preferred_element_type=F32)
        vn_ref[rows, :] = _normalise(_gelu(v)).astype(BF16)


def _in_even(h, modv, w_in, ctx_len):
    b, s, d = h.shape
    tm = _post_tile(s, 0)
    n_in = w_in.shape[1]
    row = lambda width: pl.BlockSpec((None, tm, width), lambda bi, i: (bi, i, 0))
    return pl.pallas_call(
        functools.partial(_in_even_kernel, ctx_len=ctx_len),
        grid=(b, s // tm),
        in_specs=[row(d),
                  pl.BlockSpec((None, 16, d), lambda bi, i: (bi, 0, 0)),
                  _const_spec((d, n_in))],
        out_specs=[row(RNN_WIDTH), row(RNN_WIDTH), row(CMLP_WIDTH), row(CMLP_WIDTH)],
        out_shape=[jax.ShapeDtypeStruct((b, s, RNN_WIDTH), BF16),
                   jax.ShapeDtypeStruct((b, s, RNN_WIDTH), F32),
                   jax.ShapeDtypeStruct((b, s, CMLP_WIDTH), BF16),
                   jax.ShapeDtypeStruct((b, s, CMLP_WIDTH), BF16)],
        compiler_params=_cparams(("parallel", "parallel")),
        name="even_in_proj",
    )(h, modv, w_in)


def _block_scan(a_ref, b_ref, o_ref, t0, carry, reverse):
    n = SCAN_RUN
    sub = lax.broadcasted_iota(jnp.int32, (8, a_ref.shape[1]), 0)
    ngroup = a_ref.shape[0] // (8 * n)
    steps = range(n - 1, -1, -1) if reverse else range(n)
    for k in (range(ngroup - 1, -1, -1) if reverse else range(ngroup)):
        base = 8 * n * k
        av = [a_ref[pl.ds(base + i, 8, stride=n), :] for i in range(n)]
        bv = [b_ref[pl.ds(base + i, 8, stride=n), :] for i in range(n)]
        hloc, aloc = [None] * n, [None] * n
        prev = None
        for i in steps:
            hloc[i] = bv[i] if prev is None else av[i] * hloc[prev] + bv[i]
            aloc[i] = av[i] if prev is None else av[i] * aloc[prev]
            prev = i
        atot, htot = aloc[prev], hloc[prev]
        for d in (1, 2, 4):
            shift, ok = (8 - d, sub < 8 - d) if reverse else (d, sub >= d)
            htot = jnp.where(ok, htot + atot * pltpu.roll(htot, shift, 0), htot)
            atot = jnp.where(ok, atot * pltpu.roll(atot, shift, 0), atot)
        shift, ok = (7, sub < 7) if reverse else (1, sub >= 1)
        enter = (jnp.where(ok, pltpu.roll(htot, shift, 0), 0.0)
                 + jnp.where(ok, pltpu.roll(atot, shift, 0), 1.0) * carry)
        for i in range(n):
            o_ref[pl.ds(t0 + base + i, 8, stride=n), :] = hloc[i] + aloc[i] * enter
        last = 0 if reverse else 7
        carry = (jnp.broadcast_to(htot[last:last + 1, :], htot.shape)
                 + jnp.broadcast_to(atot[last:last + 1, :], atot.shape) * carry)
    return carry


def _scan_kernel(xr_ref, gg_ref, cw_ref, cb_ref, gw_ref, gb_ref, lam_ref, out_ref, xc_ref, rf_ref, rr_ref,
                 ab0_ref, ab1_ref, g0_ref, g1_ref, *, ctx_len, tb):
    s = xr_ref.shape[0]
    nblk = s // tb
    ncb = ctx_len // tb
    cw = cw_ref[...]
    cb = cb_ref[...]
    row = lax.broadcasted_iota(jnp.int32, (tb, LANES), 0)

    def conv(blk, _):
        t0 = pl.multiple_of(blk * tb, tb)
        x = xr_ref[pl.ds(t0, tb), :]
        prev = xr_ref[pl.ds(pl.multiple_of(jnp.maximum(t0 - 8, 0), 8), 8), :]
        nxt = xr_ref[pl.ds(pl.multiple_of(jnp.minimum(t0 + tb, s - 8), 8), 8), :]
        pf = jnp.where((blk == 0) | (blk == ncb), 0.0, 1.0)
        nf = jnp.where((blk == ncb - 1) | (blk == nblk - 1), 0.0, 1.0)
        p6 = prev[6:7, :] * pf
        p7 = prev[7:8, :] * pf
        n0 = nxt[0:1, :] * nf
        xm1 = jnp.where(row == 0, p7, pltpu.roll(x, 1, 0))
        xm2 = jnp.where(row == 0, p6, jnp.where(row == 1, p7, pltpu.roll(x, 2, 0)))
        xp1 = jnp.where(row == tb - 1, n0, pltpu.roll(x, tb - 1, 0))
        xc_ref[pl.ds(t0, tb), :] = (xm2 * cw[0:1, :] + xm1 * cw[1:2, :] + x * cw[2:3, :] + xp1 * cw[3:4, :]
                                    + cb)
        return 0

    lax.fori_loop(0, nblk, conv, 0)

    def block_start(step, d):
        blk = step if d == 0 else jnp.where(step < ncb, ncb - 1 - step, nblk - 1 - (step - ncb))
        return pl.multiple_of(blk * tb, tb)

    def gate_matmuls(step, g_ref):
        step = jnp.minimum(step, nblk - 1)
        for d in range(2):
            xc = xc_ref[pl.ds(block_start(step, d), tb), :]
            g_ref[d] = jnp.dot(xc.astype(BF16), gw_ref[d], preferred_element_type=F32)

    def stash_coeffs(step, g_ref, ab_ref):
        step = jnp.minimum(step, nblk - 1)
        for d in range(2):
            xc = xc_ref[pl.ds(block_start(step, d), tb), :]
            g = g_ref[d] + gb_ref[d]
            r = jax.nn.sigmoid(g[:, :LANES])
            ig = jax.nn.sigmoid(g[:, LANES:])
            z = -lam_ref[d:d + 1, :]
            softplus = jnp.maximum(z, 0.0) + jnp.log(1.0 + jnp.exp(-jnp.abs(z)))
            a = jnp.exp2(r * ((-LRU_C * LOG2E) * softplus))
            ab_ref[2 * d] = a
            ab_ref[2 * d + 1] = jnp.sqrt(1.0 - a * a) * (ig * xc)

    def scan(step, ab_ref, cf, cr):
        cf = _block_scan(ab_ref.at[0], ab_ref.at[1], rf_ref, block_start(step, 0), cf, False)
        cr = _block_scan(ab_ref.at[2], ab_ref.at[3], rr_ref, block_start(step, 1), cr, True)
        return cf, cr

    gate_matmuls(0, g0_ref)
    stash_coeffs(0, g0_ref, ab0_ref)
    gate_matmuls(1, g1_ref)
    gate_matmuls(2, g0_ref)

    def two_steps(u, carry):
        cf, cr = carry
        cf, cr = scan(2 * u, ab0_ref, cf, cr)
        stash_coeffs(2 * u + 1, g1_ref, ab1_ref)
        cf, cr = scan(2 * u + 1, ab1_ref, cf, cr)
        stash_coeffs(2 * u + 2, g0_ref, ab0_ref)
        gate_matmuls(2 * u + 3, g1_ref)
        gate_matmuls(2 * u + 4, g0_ref)
        return cf, cr

    zero = jnp.zeros((8, LANES), F32)
    lax.fori_loop(0, nblk // 2, two_steps, (zero, zero))

    def combine(j, _):
        t0 = pl.multiple_of(j * tb, tb)
        rec = rf_ref[pl.ds(t0, tb), :] + rr_ref[pl.ds(t0, tb), :]
        out_ref[pl.ds(t0, tb), :] = (gg_ref[pl.ds(t0, tb), :].astype(F32) * rec).astype(BF16)
        return 0

    lax.fori_loop(0, nblk, combine, 0)


def _rglru(xr, gg, conv_w, conv_b, gate_w, gate_b, lam, ctx_len):
    b, s, w = xr.shape
    nh = w // LANES
    assert (s // SCAN_BLOCK) % 2 == 0 and ctx_len % SCAN_BLOCK == 0
    col = lambda dt: pl.BlockSpec((None, s, LANES), lambda bi, hd: (bi, 0, hd))
    return pl.pallas_call(
        functools.partial(_scan_kernel, ctx_len=ctx_len, tb=SCAN_BLOCK),
        grid=(b, nh),
        in_specs=[col(F32), col(BF16),
                  pl.BlockSpec((CONV_W, LANES), lambda bi, hd: (0, hd)),
                  pl.BlockSpec((1, LANES), lambda bi, hd: (0, hd)),
                  pl.BlockSpec((None, 2, LANES, 2 * LANES), lambda bi, hd: (hd, 0, 0, 0)),
                  pl.BlockSpec((None, 2, 1, 2 * LANES), lambda bi, hd: (hd, 0, 0, 0)),
                  pl.BlockSpec((2, LANES), lambda bi, hd: (0, hd))],
        out_specs=col(BF16),
        out_shape=jax.ShapeDtypeStruct((b, s, w), BF16),
        scratch_shapes=[pltpu.VMEM((s, LANES), F32), pltpu.VMEM((s, LANES), F32), pltpu.VMEM((s, LANES), F32),
                        pltpu.VMEM((4, SCAN_BLOCK, LANES), F32), pltpu.VMEM((4, SCAN_BLOCK, LANES), F32),
                        pltpu.VMEM((2, SCAN_BLOCK, 2 * LANES), F32), pltpu.VMEM((2, SCAN_BLOCK, 2 * LANES), F32)],
        compiler_params=_cparams(("parallel", "parallel")),
        name="rglru_scan",
    )(xr, gg, conv_w, conv_b, gate_w, gate_b, lam)


def _residual_ln(h, y, gate, g, b):
    return _normalise(ALPHA * h + gate * y) * g + b


def _swiglu(a, wi_ref, wo_ref):
    acc = None
    c0 = 0
    for width in FFN_CHUNKS:
        zg = jnp.dot(a, wi_ref[:, c0:c0 + width], preferred_element_type=F32)
        zu = jnp.dot(a, wi_ref[:, FFN_HIDDEN + c0:FFN_HIDDEN + c0 + width], preferred_element_type=F32)
        hm = (zg * jax.nn.sigmoid(zg) * zu).astype(BF16)
        y = jnp.dot(hm, wo_ref[c0:c0 + width, :], preferred_element_type=F32)
        acc = y if acc is None else acc + y
        c0 += width
    return acc


def _residual_pair(h, y_mix, m_ref, is_ctx, l1g_ref, l1b_ref, wi_ref, wo_ref, l2g_ref, l2b_ref):
    h1 = _residual_ln(h, y_mix, _mod_rows(m_ref, is_ctx, 2), l1g_ref[...], l1b_ref[...])
    a = (h1 * (1.0 + _mod_rows(m_ref, is_ctx, 4)) + _mod_rows(m_ref, is_ctx, 3)).astype(BF16)
    return _residual_ln(h1, _swiglu(a, wi_ref, wo_ref), _mod_rows(m_ref, is_ctx, 5), l2g_ref[...], l2b_ref[...])


def _sub_tiles(tm):
    return [slice(r * ROW_TILE, (r + 1) * ROW_TILE) for r in range(tm // ROW_TILE)]


def _post_tile(n_rows, first_tile):
    return 3 * ROW_TILE if first_tile == 0 and n_rows % (3 * ROW_TILE) == 0 else ROW_TILE


def _post_even_kernel(h_ref, m_ref, mr_ref, gu_ref, vn_ref, ws_ref, bs_ref, wm_ref, l1g_ref, l1b_ref,
                      wi_ref, wo_ref, l2g_ref, l2b_ref, o_ref, gm_ref, *, ctx_len):
    tm = h_ref.shape[0]
    gw = CMLP_WIDTH // CMLP_GROUPS
    for rows in _sub_tiles(tm):
        is_ctx = pl.program_id(1) * tm + rows.start < ctx_len
        for c in range(rows.start, rows.stop, CHUNK):
            crow = slice(c, c + CHUNK)
            for g in range(CMLP_GROUPS):
                cols = slice(g * gw, (g + 1) * gw)
                mixed = jnp.dot(ws_ref[g], vn_ref[crow, cols], preferred_element_type=F32) + bs_ref[g]
                gm_ref[crow, cols] = (gu_ref[crow, cols].astype(F32) * mixed).astype(BF16)
        y = jnp.dot(mr_ref[rows, :], wm_ref[0:RNN_WIDTH, :], preferred_element_type=F32)
        y = y + jnp.dot(gm_ref[rows, :], wm_ref[RNN_WIDTH:, :], preferred_element_type=F32)
        o_ref[rows, :] = _residual_pair(h_ref[rows, :], y, m_ref, is_ctx, l1g_ref, l1b_ref,
                                        wi_ref, wo_ref, l2g_ref, l2b_ref)


def _post_even(h, modv, mr, gu, vn, ws, bsb, w_mix, l1g, l1b, w_in, w_out, l2g, l2b, ctx_len):
    b, s, d = h.shape
    tm = _post_tile(s, 0)
    row = lambda width: pl.BlockSpec((None, tm, width), lambda bi, i: (bi, i, 0))
    vec = _const_spec((1, d))
    return pl.pallas_call(
        functools.partial(_post_even_kernel, ctx_len=ctx_len),
        grid=(b, s // tm),
        in_specs=[row(d),
                  pl.BlockSpec((None, 16, d), lambda bi, i: (bi, 0, 0)),
                  row(RNN_WIDTH), row(CMLP_WIDTH), row(CMLP_WIDTH),
                  _const_spec(ws.shape), _const_spec(bsb.shape), _const_spec(w_mix.shape), vec, vec,
                  _const_spec(w_in.shape), _const_spec(w_out.shape), vec, vec],
        out_specs=row(d),
        out_shape=jax.ShapeDtypeStruct((b, s, d), F32),
        scratch_shapes=[pltpu.VMEM((tm, CMLP_WIDTH), BF16)],
        compiler_params=_cparams(("parallel", "parallel")),
        name="even_out_ffn",
    )(h, modv, mr, gu, vn, ws, bsb, w_mix, l1g, l1b, w_in, w_out, l2g, l2b)


def _post_odd_kernel(h_ref, m_ref, yc_ref, yd_ref, wm_ref, l1g_ref, l1b_ref, wi_ref, wo_ref, l2g_ref, l2b_ref,
                     o_ref, *, ctx_len, first_tile):
    tm = h_ref.shape[0]
    half = yc_ref.shape[1]
    for rows in _sub_tiles(tm):
        is_ctx = pl.program_id(1) * tm + first_tile * ROW_TILE + rows.start < ctx_len
        y = jnp.dot(yc_ref[rows, :], wm_ref[0:half, :], preferred_element_type=F32)
        y = y + jnp.dot(yd_ref[rows, :], wm_ref[half:, :], preferred_element_type=F32)
        o_ref[rows, :] = _residual_pair(h_ref[rows, :], y, m_ref, is_ctx, l1g_ref, l1b_ref,
                                        wi_ref, wo_ref, l2g_ref, l2b_ref)


def _post_odd(h, modv, yc, yd, w_mix, l1g, l1b, w_in, w_out, l2g, l2b, ctx_len, first_tile):
    b, s, d = h.shape
    tm = _post_tile(s, first_tile)
    n = (s - first_tile * ROW_TILE) // tm
    assert yc.shape[1] == n * tm and yd.shape[1] == n * tm
    row = lambda width: pl.BlockSpec((None, tm, width), lambda bi, i: (bi, i, 0))
    vec = _const_spec((1, d))
    return pl.pallas_call(
        functools.partial(_post_odd_kernel, ctx_len=ctx_len, first_tile=first_tile),
        grid=(b, n),
        in_specs=[pl.BlockSpec((None, tm, d), lambda bi, i: (bi, i + first_tile, 0)),
                  pl.BlockSpec((None, 16, d), lambda bi, i: (bi, 0, 0)),
                  row(yc.shape[2]), row(yd.shape[2]),
                  _const_spec(w_mix.shape), vec, vec,
                  _const_spec(w_in.shape), _const_spec(w_out.shape), vec, vec],
        out_specs=row(d),
        out_shape=jax.ShapeDtypeStruct((b, n * tm, d), F32),
        compiler_params=_cparams(("parallel", "parallel")),
        name="odd_out_ffn",
    )(h, modv, yc, yd, w_mix, l1g, l1b, w_in, w_out, l2g, l2b)


def _in_odd_kernel(h_ref, m_ref, w_ref, bd_ref, gq_ref, gk_ref, cos_ref, sin_ref,
                   qc_ref, qd_ref, kc_ref, vc_ref, kd_ref, vd_ref, *, ctx_len):
    tm = h_ref.shape[0]
    sub = ROW_TILE
    lane = lax.broadcasted_iota(jnp.int32, (sub, LANES), 1)
    first = (lane & 31) < 16
    rowi = lax.broadcasted_iota(jnp.int32, (LANES, sub), 0)
    pad = jnp.where(rowi == HEAD_DIM, 1.0, 0.0)
    scale = HEAD_DIM ** -0.5
    nq = Q_HEADS * HEAD_DIM

    def rms(x, gain, width):
        ms = jnp.dot((x * x).astype(BF16), bd_ref[0:width, 0:width], preferred_element_type=F32)
        return x * lax.rsqrt(ms + EPS) * gain

    def value_rows(v):
        vt = v.T
        return (jnp.where(rowi < HEAD_DIM, vt, pad).astype(BF16),
                jnp.where(rowi < HEAD_DIM, jnp.concatenate([vt[HEAD_DIM:], vt[:HEAD_DIM]], axis=0), pad).astype(BF16))

    for r, rows in enumerate(_sub_tiles(tm)):
        is_ctx = pl.program_id(1) * tm + rows.start < ctx_len
        sh = _mod_rows(m_ref, is_ctx, 0)
        sc = _mod_rows(m_ref, is_ctx, 1)
        a = (h_ref[rows, :] * (1.0 + sc) + sh).astype(BF16)
        cos = cos_ref[rows, :]
        sin = sin_ref[rows, :]

        def rope(x):
            partner = jnp.where(first, pltpu.roll(x, LANES - 16, 1), pltpu.roll(x, 16, 1))
            return x * cos + partner * sin

        cq = rms(jnp.dot(a, w_ref[:, 0:nq], preferred_element_type=F32), gq_ref[...], nq)
        for m in range(nq // LANES):
            qc_ref[m * LANES:(m + 1) * LANES, rows] = (
                rope(cq[:, m * LANES:(m + 1) * LANES]) * (scale * LOG2E)).T.astype(BF16)
        dq = jnp.dot(a, w_ref[:, nq:2 * nq], preferred_element_type=F32)
        for m in range(nq // LANES):
            qd_ref[m * LANES:(m + 1) * LANES, rows] = (
                rope(dq[:, m * LANES:(m + 1) * LANES]) * (scale * LOG2E)).T.astype(BF16)
        kv = jnp.dot(a, w_ref[:, 2 * nq:], preferred_element_type=F32)
        kc_ref[rows, :] = rope(rms(kv[:, 0:LANES], gk_ref[...], LANES)).astype(BF16)
        kd_ref[rows, :] = rope(kv[:, 2 * LANES:3 * LANES]).astype(BF16)
        vc_ref[0, r], vc_ref[1, r] = value_rows(kv[:, LANES:2 * LANES])
        vd = value_rows(kv[:, 3 * LANES:4 * LANES])
        for jj in range(KV_HEADS):
            for c in range(sub // LANES):
                vd_ref[jj, r * (sub // LANES) + c] = vd[jj][:, c * LANES:(c + 1) * LANES]


def _in_odd(h, modv, w_in, bd, gq, gk, cos_t, sin_t, ctx_len):
    b, s, d = h.shape
    tm = _post_tile(s, 0)
    nq = Q_HEADS * HEAD_DIM
    row = lambda width: pl.BlockSpec((None, tm, width), lambda bi, i: (bi, i, 0))
    tab = pl.BlockSpec((tm, LANES), lambda bi, i: (i, 0))
    qt = pl.BlockSpec((None, nq, tm), lambda bi, i: (bi, 0, i))
    sds = lambda width: jax.ShapeDtypeStruct((b, s, width), BF16)
    return pl.pallas_call(
        functools.partial(_in_odd_kernel, ctx_len=ctx_len),
        grid=(b, s // tm),
        in_specs=[row(d),
                  pl.BlockSpec((None, 16, d), lambda bi, i: (bi, 0, 0)),
                  _const_spec(w_in.shape), _const_spec(bd.shape),
                  _const_spec(gq.shape), _const_spec(gk.shape), tab, tab],
        out_specs=[qt, qt, row(LANES),
                   pl.BlockSpec((None, KV_HEADS, tm // ATT_TK, LANES, ATT_TK), lambda bi, i: (bi, 0, i, 0, 0)),
                   row(LANES),
                   pl.BlockSpec((None, KV_HEADS, tm // LANES, LANES, LANES), lambda bi, i: (bi, 0, i, 0, 0))],
        out_shape=[jax.ShapeDtypeStruct((b, nq, s), BF16), jax.ShapeDtypeStruct((b, nq, s), BF16), sds(LANES),
                   jax.ShapeDtypeStruct((b, KV_HEADS, s // ATT_TK, LANES, ATT_TK), BF16), sds(LANES),
                   jax.ShapeDtypeStruct((b, KV_HEADS, s // LANES, LANES, LANES), BF16)],
        compiler_params=_cparams(("parallel", "parallel")),
        name="odd_in_proj",
    )(h, modv, w_in, bd, gq, gk, cos_t, sin_t)


def _stack_queries(q_ref, qs_ref, j, tq):
    for hh in range(GROUP):
        qh = q_ref[hh * HEAD_DIM:(hh + 1) * HEAD_DIM, :]
        zero = jnp.zeros_like(qh)
        qs_ref[0:HEAD_DIM, hh * tq:(hh + 1) * tq] = jnp.where(j == 0, qh, zero)
        qs_ref[HEAD_DIM:, hh * tq:(hh + 1) * tq] = jnp.where(j == 0, zero, qh)


def _store_heads(o, o_ref, tq):
    for m in range(GROUP // 2):
        pair = jnp.concatenate([o[:, (2 * m) * tq:(2 * m + 1) * tq],
                                o[:, (2 * m + 1) * tq:(2 * m + 2) * tq]], axis=0)
        o_ref[:, m * LANES:(m + 1) * LANES] = pair.T.astype(o_ref.dtype)


def _gattn_kernel(q_ref, k_ref, v_ref, o_ref, *scratch, ctx_len, first_tile):
    tq = q_ref.shape[1]
    gw = GROUP * HEAD_DIM
    _, ntile, _, tv = v_ref.shape
    chains = [scratch[6 * j:6 * j + 6] for j in range(KV_HEADS)]
    big = chains[0][4].shape[0]
    nbig = (ntile * tv - ctx_len) // big
    qi = pl.program_id(1) + first_tile

    def scores(j, start, s_ref):
        if not isinstance(start, int):
            start = pl.multiple_of(start, tv)
        s_ref[...] = jnp.dot(k_ref[pl.ds(start, s_ref.shape[0]), :], chains[j][0][...],
                             preferred_element_type=F32)

    def update(j, start, s_ref):
        m_ref, acc_ref = chains[j][1], chains[j][2]
        sc = s_ref[...]
        m_prev = m_ref[...]
        m_new = jnp.maximum(m_prev, jnp.max(sc, axis=0, keepdims=True))
        alpha = jnp.exp2(m_prev - m_new)
        p = jnp.exp2(sc - m_new).astype(BF16)
        t0 = start // tv
        vt = jnp.concatenate([v_ref[j, t0 + i] for i in range(s_ref.shape[0] // tv)], axis=1)
        acc_ref[...] = alpha * acc_ref[...] + jnp.dot(vt, p, preferred_element_type=F32)
        m_ref[...] = m_new

    def latent(n):
        return ctx_len + n * big

    for j, (qs_ref, m_ref, acc_ref, sc_ref, _, _) in enumerate(chains):
        _stack_queries(q_ref.at[j * gw:(j + 1) * gw, :], qs_ref, j, tq)
        m_ref[...] = jnp.full(m_ref.shape, NEG_INF, F32)
        acc_ref[...] = jnp.zeros(acc_ref.shape, F32)
        scores(j, 0, sc_ref)

    @pl.when(qi * tq >= ctx_len)
    def _():
        for j, (_, _, _, sc_ref, sa_ref, _) in enumerate(chains):
            scores(j, latent(0), sa_ref)
            update(j, 0, sc_ref)

        def pair(u, _):
            for j, (_, _, _, _, sa_ref, sb_ref) in enumerate(chains):
                scores(j, latent(2 * u + 1), sb_ref)
                update(j, latent(2 * u), sa_ref)
            for j, (_, _, _, _, sa_ref, sb_ref) in enumerate(chains):
                scores(j, latent(2 * u + 2), sa_ref)
                update(j, latent(2 * u + 1), sb_ref)
            return 0

        lax.fori_loop(0, (nbig - 1) // 2, pair, 0)
        for j, (_, _, _, _, sa_ref, sb_ref) in enumerate(chains):
            if nbig % 2 == 0:
                scores(j, latent(nbig - 1), sb_ref)
                update(j, latent(nbig - 2), sa_ref)
                update(j, latent(nbig - 1), sb_ref)
            else:
                update(j, latent(nbig - 1), sa_ref)

    @pl.when(qi * tq < ctx_len)
    def _():
        for j in range(KV_HEADS):
            update(j, 0, chains[j][3])

    for j in range(KV_HEADS):
        acc_ref = chains[j][2]
        _store_heads(acc_ref[0:HEAD_DIM, :] / acc_ref[HEAD_DIM:HEAD_DIM + 1, :],
                     o_ref.at[:, j * gw:(j + 1) * gw], tq)


def _global_attention(qt, k, vt, ctx_len, first_tile):
    b, nq, s = qt.shape
    tq = ATT_TQ
    assert vt.shape[4] == ATT_TK and ctx_len % ATT_TK == 0 and (s - ctx_len) % ATT_BIG == 0
    cols = GROUP * tq
    per_head = [pltpu.VMEM((LANES, cols), BF16), pltpu.VMEM((1, cols), F32), pltpu.VMEM((LANES, cols), F32),
                pltpu.VMEM((ctx_len, cols), F32), pltpu.VMEM((ATT_BIG, cols), F32),
                pltpu.VMEM((ATT_BIG, cols), F32)]
    return pl.pallas_call(
        functools.partial(_gattn_kernel, ctx_len=ctx_len, first_tile=first_tile),
        grid=(b, s // tq - first_tile),
        in_specs=[pl.BlockSpec((None, nq, tq), lambda bi, i: (bi, 0, i + first_tile)),
                  pl.BlockSpec((None, s, LANES), lambda bi, i: (bi, 0, 0)),
                  pl.BlockSpec((None,) + vt.shape[1:], lambda bi, i: (bi, 0, 0, 0, 0))],
        out_specs=pl.BlockSpec((None, tq, nq), lambda bi, i: (bi, i, 0)),
        out_shape=jax.ShapeDtypeStruct((b, s - first_tile * tq, nq), BF16),
        scratch_shapes=per_head * KV_HEADS,
        compiler_params=_cparams(("parallel", "arbitrary")),
        name="global_attention",
    )(qt, k, vt)


def _wattn_kernel(q_ref, k_ref, v_ref, sink_ref, bias_ref, o_ref, *qs_refs, ctx_len, span, first_tile):
    tq = q_ref.shape[1]
    s = k_ref.shape[0]
    gw = GROUP * HEAD_DIM
    qi = pl.program_id(1) + first_tile
    is_ctx = qi * tq < ctx_len

    def head(j, window_start):
        qs_ref = qs_refs[j]
        _stack_queries(q_ref.at[j * gw:(j + 1) * gw, :], qs_ref, j, tq)
        qs = qs_ref[...]
        sink = sink_ref[j] * LOG2E
        s_ctx = jnp.dot(k_ref[0:ctx_len, :], qs, preferred_element_type=F32)
        v_ctx = jnp.concatenate([v_ref[j, t] for t in range(ctx_len // LANES)], axis=1)
        m = jnp.maximum(jnp.max(s_ctx, axis=0, keepdims=True), sink)
        if window_start is not None:
            s_win = (jnp.dot(k_ref[pl.ds(window_start, span), :], qs, preferred_element_type=F32)
                     + bias_ref[...])
            m = jnp.maximum(m, jnp.max(s_win, axis=0, keepdims=True))
        acc = jnp.dot(v_ctx, jnp.exp2(s_ctx - m).astype(BF16), preferred_element_type=F32)
        if window_start is not None:
            t0 = window_start // LANES
            v_win = jnp.concatenate([v_ref[j, t0 + i] for i in range(span // LANES)], axis=1)
            acc = acc + jnp.dot(v_win, jnp.exp2(s_win - m).astype(BF16), preferred_element_type=F32)
        den = acc[HEAD_DIM:HEAD_DIM + 1, :] + jnp.exp2(sink - m)
        _store_heads(acc[0:HEAD_DIM, :] / den, o_ref.at[:, j * gw:(j + 1) * gw], tq)

    @pl.when(is_ctx)
    def _():
        for j in range(KV_HEADS):
            head(j, None)

    @pl.when(jnp.logical_not(is_ctx))
    def _():
        start = pl.multiple_of(jnp.clip(qi * tq - WINDOW, ctx_len, s - span), LANES)
        for j in range(KV_HEADS):
            head(j, start)


def _window_attention(qt, k, vt, sink, ctx_len, first_tile):
    b, nq, s = qt.shape
    tq = ATT_TQ
    gw = GROUP * HEAD_DIM
    span = tq + 2 * WINDOW
    assert tq >= WINDOW and (s - ctx_len) // tq >= 2
    sink_row = jnp.repeat(sink.reshape(KV_HEADS, 1, GROUP), tq, axis=2)
    r = jnp.arange(span)[:, None]
    qcol = (jnp.arange(GROUP * tq) % tq)[None, :]
    bias = jnp.stack([jnp.where(jnp.abs(off + r - qcol) <= WINDOW, 0.0, NEG_INF)
                      for off in (0, -WINDOW, tq - span)]).astype(F32)

    def placement(i):
        lo = (i + first_tile) * tq - WINDOW
        return jnp.where(lo < ctx_len, 0, jnp.where(lo > s - span, 2, 1))

    return pl.pallas_call(
        functools.partial(_wattn_kernel, ctx_len=ctx_len, span=span, first_tile=first_tile),
        grid=(b, s // tq - first_tile),
        in_specs=[pl.BlockSpec((None, nq, tq), lambda bi, i: (bi, 0, i + first_tile)),
                  pl.BlockSpec((None, s, LANES), lambda bi, i: (bi, 0, 0)),
                  pl.BlockSpec((None,) + vt.shape[1:], lambda bi, i: (bi, 0, 0, 0, 0)),
                  pl.BlockSpec((KV_HEADS, 1, GROUP * tq), lambda bi, i: (0, 0, 0)),
                  pl.BlockSpec((None, span, GROUP * tq), lambda bi, i: (placement(i), 0, 0))],
        out_specs=pl.BlockSpec((None, tq, nq), lambda bi, i: (bi, i, 0)),
        out_shape=jax.ShapeDtypeStruct((b, s - first_tile * tq, nq), BF16),
        scratch_shapes=[pltpu.VMEM((LANES, GROUP * tq), BF16)] * KV_HEADS,
        compiler_params=_cparams(("parallel", "parallel")),
        name="window_attention",
    )(qt, k, vt, sink_row, bias)


def _rope_tables(n_lat, ctx_len):
    rows = n_lat // GRID_W
    row = jnp.repeat(jnp.arange(rows, dtype=F32), GRID_W)
    col = jnp.tile(jnp.arange(GRID_W, dtype=F32), rows)
    nf = HEAD_DIM // 4
    inv_freq = ROPE_THETA ** (-jnp.arange(nf, dtype=F32) / nf)
    ang_r = row[:, None] * inv_freq
    ang_c = col[:, None] * inv_freq
    cos = jnp.concatenate([jnp.cos(ang_r)] * 2 + [jnp.cos(ang_c)] * 2, axis=1)
    sin = jnp.concatenate([-jnp.sin(ang_r), jnp.sin(ang_r), -jnp.sin(ang_c), jnp.sin(ang_c)], axis=1)
    cos = jnp.concatenate([jnp.ones((ctx_len, HEAD_DIM), F32), cos], axis=0)
    sin = jnp.concatenate([jnp.zeros((ctx_len, HEAD_DIM), F32), sin], axis=0)
    return jnp.tile(cos, (1, LANES // HEAD_DIM)), jnp.tile(sin, (1, LANES // HEAD_DIM))


def kernel(x, c, ctx, c_ctx, ada_w, ada_b, ln1_g, ln1_b, ln2_g, ln2_b, ffn_w_in, ffn_w_out,
           ev_w_in, ev_w_out, rg_conv_w, rg_conv_b, rg_gate_w, rg_gate_b, rg_lambda, cm_w_s, cm_b_s,
           od_w_in, od_w_out, qn_g, kn_g, sink):
    b, t, d = x.shape
    ctx_len = ctx.shape[1]
    s = ctx_len + t
    depth = ada_w.shape[0]
    assert d == D_MODEL and depth == DEPTH and b + 1 <= 16
    assert ctx_len % ROW_TILE == 0 and t % ROW_TILE == 0 and t % GRID_W == 0 and ATT_TQ == ATT_TK == ROW_TILE

    c_rows = jnp.zeros((16, d), F32).at[:b].set(c).at[b].set(c_ctx)
    mods = _ada_vectors(c_rows, ada_w, ada_b).reshape(depth, 16, 6, d)
    modv = jnp.zeros((depth, b, 16, d), F32)
    modv = modv.at[:, :, 0:6].set(mods[:, :b])
    modv = modv.at[:, :, 8:14].set(jnp.broadcast_to(mods[:, b][:, None], (depth, b, 6, d)))

    cos_t, sin_t = _rope_tables(t, ctx_len)
    nq = Q_HEADS * HEAD_DIM
    bd = jnp.kron(jnp.eye(nq // HEAD_DIM, dtype=F32), jnp.full((HEAD_DIM, HEAD_DIM), 1.0 / HEAD_DIM, F32)).astype(BF16)
    assert sum(FFN_CHUNKS) == FFN_HIDDEN

    h = jnp.concatenate([ctx, x], axis=1)
    rows_ctx = ctx_len
    for l in range(depth):
        j = l // 2
        mv = modv[l]
        norms = (ln1_g[l].reshape(1, d), ln1_b[l].reshape(1, d), ffn_w_in[l].astype(BF16),
                 ffn_w_out[l].astype(BF16), ln2_g[l].reshape(1, d), ln2_b[l].reshape(1, d))
        if l % 2 == 0:
            gg, xr, gu, vn = _in_even(h, mv, ev_w_in[j].astype(BF16), ctx_len)
            gw = jnp.transpose(rg_gate_w[j], (2, 0, 3, 1, 4)).reshape(RNN_HEADS, 2, LANES, 2 * LANES).astype(BF16)
            gb = jnp.transpose(rg_gate_b[j].reshape(2, 2, RNN_HEADS, LANES), (2, 0, 1, 3)).reshape(RNN_HEADS, 2, 1, 2 * LANES)
            mr = _rglru(xr, gg, rg_conv_w[j], rg_conv_b[j].reshape(1, -1), gw, gb, rg_lambda[j], ctx_len)
            bsb = jnp.broadcast_to(cm_b_s[j][:, :, None], (CMLP_GROUPS, CHUNK, CMLP_WIDTH // CMLP_GROUPS))
            h = _post_even(h, mv, mr, gu, vn, cm_w_s[j].astype(BF16), bsb, ev_w_out[j].astype(BF16), *norms, ctx_len)
        else:
            gq = jnp.tile(qn_g[j], Q_HEADS).reshape(1, nq)
            gk = jnp.tile(kn_g[j], LANES // HEAD_DIM).reshape(1, LANES)
            qc, qd, kc, vc, kd, vd = _in_odd(h, mv, od_w_in[j].astype(BF16), bd, gq, gk, cos_t, sin_t, ctx_len)
            skip = ctx_len // ATT_TQ if l == depth - 1 else 0
            yc = _global_attention(qc, kc, vc, ctx_len, skip)
            yd = _window_attention(qd, kd, vd, sink[j], ctx_len, skip)
            h = _post_odd(h, mv, yc, yd, od_w_out[j].astype(BF16), *norms, ctx_len, skip)
            if skip:
                rows_ctx = 0
    return h[:, rows_ctx:, :]
```

```python
import functools

import jax
import jax.numpy as jnp
from jax import lax
from jax.experimental import pallas as pl
from jax.experimental.pallas import tpu as pltpu

F32 = jnp.float32
BF16 = jnp.bfloat16

D_MODEL = 1024
DEPTH = 4
GRID_W = 64
RNN_WIDTH = D_MODEL
RNN_HEADS = RNN_WIDTH // 128
CONV_W = 4
LRU_C = 8.0
CMLP_WIDTH = D_MODEL // 2
CMLP_GROUPS = 4
CHUNK = 128
HEAD_DIM = 64
Q_HEADS = 8
KV_HEADS = 2
GROUP = Q_HEADS // KV_HEADS
WINDOW = 128
ROPE_THETA = 10000.0
NEG_INF = -1e30
FFN_HIDDEN = 2816
FFN_CHUNKS = (1024, 1024, 768)
ALPHA = (2.0 * DEPTH) ** 0.25
LOG2E = 1.4426950408889634
EPS = 1e-6

LANES = 128
ROW_TILE = 256
SCAN_BLOCK = 128
SCAN_RUN = 4
ATT_TQ = 256
ATT_TK = 256
ATT_BIG = 1024
VMEM_LIMIT = 56 * 1024 * 1024


def _cparams(sem):
    return pltpu.CompilerParams(dimension_semantics=sem, vmem_limit_bytes=VMEM_LIMIT)


def _const_spec(shape):
    nd = len(shape)
    return pl.BlockSpec(shape, lambda *_: (0,) * nd, pipeline_mode=pl.Buffered(1))


def _gelu(x):
    return 0.5 * x * (1.0 + jnp.tanh(0.7978845608028654 * (x + 0.044715 * (x * x * x))))


def _normalise(x):
    mu = jnp.mean(x, axis=-1, keepdims=True)
    xc = x - mu
    var = jnp.mean(xc * xc, axis=-1, keepdims=True)
    return xc * lax.rsqrt(var + EPS)


def _mod_rows(m_ref, is_ctx, lat_row):
    return jnp.where(is_ctx, m_ref[lat_row + 8:lat_row + 9, :], m_ref[lat_row:lat_row + 1, :])


def _mod_kernel(c_ref, w_ref, b_ref, o_ref):
    c = c_ref[...]
    s = c * jax.nn.sigmoid(c)
    o_ref[...] = jnp.dot(s, w_ref[...], preferred_element_type=F32,
                         precision=lax.Precision.HIGHEST) + b_ref[...]


def _ada_vectors(c_rows, ada_w, ada_b):
    depth, d, n = ada_w.shape
    rows = c_rows.shape[0]
    nb = 1536
    return pl.pallas_call(
        _mod_kernel,
        grid=(depth, n // nb),
        in_specs=[pl.BlockSpec((rows, d), lambda l, j: (0, 0)),
                  pl.BlockSpec((None, d, nb), lambda l, j: (l, 0, j)),
                  pl.BlockSpec((None, 1, nb), lambda l, j: (l, 0, j))],
        out_specs=pl.BlockSpec((None, rows, nb), lambda l, j: (l, 0, j)),
        out_shape=jax.ShapeDtypeStruct((depth, rows, n), F32),
        compiler_params=_cparams(("parallel", "parallel")),
        name="ada_vectors",
    )(c_rows, ada_w, ada_b.reshape(depth, 1, n))


def _in_even_kernel(h_ref, m_ref, w_ref, gg_ref, xr_ref, gu_ref, vn_ref, *, ctx_len):
    tm = h_ref.shape[0]
    w = RNN_WIDTH
    for rows in _sub_tiles(tm):
        is_ctx = pl.program_id(1) * tm + rows.start < ctx_len
        sh = _mod_rows(m_ref, is_ctx, 0)
        sc = _mod_rows(m_ref, is_ctx, 1)
        a = (h_ref[rows, :] * (1.0 + sc) + sh).astype(BF16)
        gate = jnp.dot(a, w_ref[:, 0:w], preferred_element_type=F32)
        gg_ref[rows, :] = _gelu(gate).astype(BF16)
        xr_ref[rows, :] = jnp.dot(a, w_ref[:, w:2 * w], preferred_element_type=F32)
        u = jnp.dot(a, w_ref[:, 2 * w:2 * w + CMLP_WIDTH], preferred_element_type=F32)
        gu_ref[rows, :] = _gelu(u).astype(BF16)
        v = jnp.dot(a, w_ref[:, 2 * w + CMLP_WIDTH:], preferred_element_type=F32)
        vn_ref[rows, :] = _normalise(_gelu(v)).astype(BF16)


def _in_even(h, modv, w_in, ctx_len):
    b, s, d = h.shape
    tm = _post_tile(s, 0)
    n_in = w_in.shape[1]
    row = lambda width: pl.BlockSpec((None, tm, width), lambda bi, i: (bi, i, 0))
    return pl.pallas_call(
        functools.partial(_in_even_kernel, ctx_len=ctx_len),
        grid=(b, s // tm),
        in_specs=[row(d),
                  pl.BlockSpec((None, 16, d), lambda bi, i: (bi, 0, 0)),
                  _const_spec((d, n_in))],
        out_specs=[row(RNN_WIDTH), row(RNN_WIDTH), row(CMLP_WIDTH), row(CMLP_WIDTH)],
        out_shape=[jax.ShapeDtypeStruct((b, s, RNN_WIDTH), BF16),
                   jax.ShapeDtypeStruct((b, s, RNN_WIDTH), F32),
                   jax.ShapeDtypeStruct((b, s, CMLP_WIDTH), BF16),
                   jax.ShapeDtypeStruct((b, s, CMLP_WIDTH), BF16)],
        compiler_params=_cparams(("parallel", "parallel")),
        name="even_in_proj",
    )(h, modv, w_in)


def _block_scan(a_ref, b_ref, o_ref, t0, carry, reverse):
    n = SCAN_RUN
    sub = lax.broadcasted_iota(jnp.int32, (8, a_ref.shape[1]), 0)
    ngroup = a_ref.shape[0] // (8 * n)
    steps = range(n - 1, -1, -1) if reverse else range(n)
    for k in (range(ngroup - 1, -1, -1) if reverse else range(ngroup)):
        base = 8 * n * k
        av = [a_ref[pl.ds(base + i, 8, stride=n), :] for i in range(n)]
        bv = [b_ref[pl.ds(base + i, 8, stride=n), :] for i in range(n)]
        hloc, aloc = [None] * n, [None] * n
        prev = None
        for i in steps:
            hloc[i] = bv[i] if prev is None else av[i] * hloc[prev] + bv[i]
            aloc[i] = av[i] if prev is None else av[i] * aloc[prev]
            prev = i
        atot, htot = aloc[prev], hloc[prev]
        for d in (1, 2, 4):
            shift, ok = (8 - d, sub < 8 - d) if reverse else (d, sub >= d)
            htot = jnp.where(ok, htot + atot * pltpu.roll(htot, shift, 0), htot)
            atot = jnp.where(ok, atot * pltpu.roll(atot, shift, 0), atot)
        shift, ok = (7, sub < 7) if reverse else (1, sub >= 1)
        enter = (jnp.where(ok, pltpu.roll(htot, shift, 0), 0.0)
                 + jnp.where(ok, pltpu.roll(atot, shift, 0), 1.0) * carry)
        for i in range(n):
            o_ref[pl.ds(t0 + base + i, 8, stride=n), :] = hloc[i] + aloc[i] * enter
        last = 0 if reverse else 7
        carry = (jnp.broadcast_to(htot[last:last + 1, :], htot.shape)
                 + jnp.broadcast_to(atot[last:last + 1, :], atot.shape) * carry)
    return carry


def _scan_kernel(xr_ref, gg_ref, cw_ref, cb_ref, gw_ref, gb_ref, lam_ref, out_ref, xc_ref, rf_ref, rr_ref,
                 ab0_ref, ab1_ref, g0_ref, g1_ref, *, ctx_len, tb):
    s = xr_ref.shape[0]
    nblk = s // tb
    ncb = ctx_len // tb
    cw = cw_ref[...]
    cb = cb_ref[...]
    row = lax.broadcasted_iota(jnp.int32, (tb, LANES), 0)

    def conv(blk, _):
        t0 = pl.multiple_of(blk * tb, tb)
        x = xr_ref[pl.ds(t0, tb), :]
        prev = xr_ref[pl.ds(pl.multiple_of(jnp.maximum(t0 - 8, 0), 8), 8), :]
        nxt = xr_ref[pl.ds(pl.multiple_of(jnp.minimum(t0 + tb, s - 8), 8), 8), :]
        pf = jnp.where((blk == 0) | (blk == ncb), 0.0, 1.0)
        nf = jnp.where((blk == ncb - 1) | (blk == nblk - 1), 0.0, 1.0)
        p6 = prev[6:7, :] * pf
        p7 = prev[7:8, :] * pf
        n0 = nxt[0:1, :] * nf
        xm1 = jnp.where(row == 0, p7, pltpu.roll(x, 1, 0))
        xm2 = jnp.where(row == 0, p6, jnp.where(row == 1, p7, pltpu.roll(x, 2, 0)))
        xp1 = jnp.where(row == tb - 1, n0, pltpu.roll(x, tb - 1, 0))
        xc_ref[pl.ds(t0, tb), :] = (xm2 * cw[0:1, :] + xm1 * cw[1:2, :] + x * cw[2:3, :] + xp1 * cw[3:4, :]
                                    + cb)
        return 0

    lax.fori_loop(0, nblk, conv, 0)

    def block_start(step, d):
        blk = step if d == 0 else jnp.where(step < ncb, ncb - 1 - step, nblk - 1 - (step - ncb))
        return pl.multiple_of(blk * tb, tb)

    def gate_matmuls(step, g_ref):
        step = jnp.minimum(step, nblk - 1)
        for d in range(2):
            xc = xc_ref[pl.ds(block_start(step, d), tb), :]
            g_ref[d] = jnp.dot(xc.astype(BF16), gw_ref[d], preferred_element_type=F32)

    def stash_coeffs(step, g_ref, ab_ref):
        step = jnp.minimum(step, nblk - 1)
        for d in range(2):
            xc = xc_ref[pl.ds(block_start(step, d), tb), :]
            g = g_ref[d] + gb_ref[d]
            r = jax.nn.sigmoid(g[:, :LANES])
            ig = jax.nn.sigmoid(g[:, LANES:])
            z = -lam_ref[d:d + 1, :]
            softplus = jnp.maximum(z, 0.0) + jnp.log(1.0 + jnp.exp(-jnp.abs(z)))
            a = jnp.exp2(r * ((-LRU_C * LOG2E) * softplus))
            ab_ref[2 * d] = a
            ab_ref[2 * d + 1] = jnp.sqrt(1.0 - a * a) * (ig * xc)

    def scan(step, ab_ref, cf, cr):
        cf = _block_scan(ab_ref.at[0], ab_ref.at[1], rf_ref, block_start(step, 0), cf, False)
        cr = _block_scan(ab_ref.at[2], ab_ref.at[3], rr_ref, block_start(step, 1), cr, True)
        return cf, cr

    gate_matmuls(0, g0_ref)
    stash_coeffs(0, g0_ref, ab0_ref)
    gate_matmuls(1, g1_ref)
    gate_matmuls(2, g0_ref)

    def two_steps(u, carry):
        cf, cr = carry
        cf, cr = scan(2 * u, ab0_ref, cf, cr)
        stash_coeffs(2 * u + 1, g1_ref, ab1_ref)
        cf, cr = scan(2 * u + 1, ab1_ref, cf, cr)
        stash_coeffs(2 * u + 2, g0_ref, ab0_ref)
        gate_matmuls(2 * u + 3, g1_ref)
        gate_matmuls(2 * u + 4, g0_ref)
        return cf, cr

    zero = jnp.zeros((8, LANES), F32)
    lax.fori_loop(0, nblk // 2, two_steps, (zero, zero))

    def combine(j, _):
        t0 = pl.multiple_of(j * tb, tb)
        rec = rf_ref[pl.ds(t0, tb), :] + rr_ref[pl.ds(t0, tb), :]
        out_ref[pl.ds(t0, tb), :] = (gg_ref[pl.ds(t0, tb), :].astype(F32) * rec).astype(BF16)
        return 0

    lax.fori_loop(0, nblk, combine, 0)


def _rglru(xr, gg, conv_w, conv_b, gate_w, gate_b, lam, ctx_len):
    b, s, w = xr.shape
    nh = w // LANES
    assert (s // SCAN_BLOCK) % 2 == 0 and ctx_len % SCAN_BLOCK == 0
    col = lambda dt: pl.BlockSpec((None, s, LANES), lambda bi, hd: (bi, 0, hd))
    return pl.pallas_call(
        functools.partial(_scan_kernel, ctx_len=ctx_len, tb=SCAN_BLOCK),
        grid=(b, nh),
        in_specs=[col(F32), col(BF16),
                  pl.BlockSpec((CONV_W, LANES), lambda bi, hd: (0, hd)),
                  pl.BlockSpec((1, LANES), lambda bi, hd: (0, hd)),
                  pl.BlockSpec((None, 2, LANES, 2 * LANES), lambda bi, hd: (hd, 0, 0, 0)),
                  pl.BlockSpec((None, 2, 1, 2 * LANES), lambda bi, hd: (hd, 0, 0, 0)),
                  pl.BlockSpec((2, LANES), lambda bi, hd: (0, hd))],
        out_specs=col(BF16),
        out_shape=jax.ShapeDtypeStruct((b, s, w), BF16),
        scratch_shapes=[pltpu.VMEM((s, LANES), F32), pltpu.VMEM((s, LANES), F32), pltpu.VMEM((s, LANES), F32),
                        pltpu.VMEM((4, SCAN_BLOCK, LANES), F32), pltpu.VMEM((4, SCAN_BLOCK, LANES), F32),
                        pltpu.VMEM((2, SCAN_BLOCK, 2 * LANES), F32), pltpu.VMEM((2, SCAN_BLOCK, 2 * LANES), F32)],
        compiler_params=_cparams(("parallel", "parallel")),
        name="rglru_scan",
    )(xr, gg, conv_w, conv_b, gate_w, gate_b, lam)


def _residual_ln(h, y, gate, g, b):
    return _normalise(ALPHA * h + gate * y) * g + b


def _swiglu(a, wi_ref, wo_ref):
    acc = None
    c0 = 0
    for width in FFN_CHUNKS:
        zg = jnp.dot(a, wi_ref[:, c0:c0 + width], preferred_element_type=F32)
        zu = jnp.dot(a, wi_ref[:, FFN_HIDDEN + c0:FFN_HIDDEN + c0 + width], preferred_element_type=F32)
        hm = (zg * jax.nn.sigmoid(zg) * zu).astype(BF16)
        y = jnp.dot(hm, wo_ref[c0:c0 + width, :], preferred_element_type=F32)
        acc = y if acc is None else acc + y
        c0 += width
    return acc


def _residual_pair(h, y_mix, m_ref, is_ctx, l1g_ref, l1b_ref, wi_ref, wo_ref, l2g_ref, l2b_ref):
    h1 = _residual_ln(h, y_mix, _mod_rows(m_ref, is_ctx, 2), l1g_ref[...], l1b_ref[...])
    a = (h1 * (1.0 + _mod_rows(m_ref, is_ctx, 4)) + _mod_rows(m_ref, is_ctx, 3)).astype(BF16)
    return _residual_ln(h1, _swiglu(a, wi_ref, wo_ref), _mod_rows(m_ref, is_ctx, 5), l2g_ref[...], l2b_ref[...])


def _sub_tiles(tm):
    return [slice(r * ROW_TILE, (r + 1) * ROW_TILE) for r in range(tm // ROW_TILE)]


def _post_tile(n_rows, first_tile):
    return 3 * ROW_TILE if first_tile == 0 and n_rows % (3 * ROW_TILE) == 0 else ROW_TILE


def _post_even_kernel(h_ref, m_ref, mr_ref, gu_ref, vn_ref, ws_ref, bs_ref, wm_ref, l1g_ref, l1b_ref,
                      wi_ref, wo_ref, l2g_ref, l2b_ref, o_ref, gm_ref, *, ctx_len):
    tm = h_ref.shape[0]
    gw = CMLP_WIDTH // CMLP_GROUPS
    for rows in _sub_tiles(tm):
        is_ctx = pl.program_id(1) * tm + rows.start < ctx_len
        for c in range(rows.start, rows.stop, CHUNK):
            crow = slice(c, c + CHUNK)
            for g in range(CMLP_GROUPS):
                cols = slice(g * gw, (g + 1) * gw)
                mixed = jnp.dot(ws_ref[g], vn_ref[crow, cols], preferred_element_type=F32) + bs_ref[g]
                gm_ref[crow, cols] = (gu_ref[crow, cols].astype(F32) * mixed).astype(BF16)
        y = jnp.dot(mr_ref[rows, :], wm_ref[0:RNN_WIDTH, :], preferred_element_type=F32)
        y = y + jnp.dot(gm_ref[rows, :], wm_ref[RNN_WIDTH:, :], preferred_element_type=F32)
        o_ref[rows, :] = _residual_pair(h_ref[rows, :], y, m_ref, is_ctx, l1g_ref, l1b_ref,
                                        wi_ref, wo_ref, l2g_ref, l2b_ref)


def _post_even(h, modv, mr, gu, vn, ws, bsb, w_mix, l1g, l1b, w_in, w_out, l2g, l2b, ctx_len):
    b, s, d = h.shape
    tm = _post_tile(s, 0)
    row = lambda width: pl.BlockSpec((None, tm, width), lambda bi, i: (bi, i, 0))
    vec = _const_spec((1, d))
    return pl.pallas_call(
        functools.partial(_post_even_kernel, ctx_len=ctx_len),
        grid=(b, s // tm),
        in_specs=[row(d),
                  pl.BlockSpec((None, 16, d), lambda bi, i: (bi, 0, 0)),
                  row(RNN_WIDTH), row(CMLP_WIDTH), row(CMLP_WIDTH),
                  _const_spec(ws.shape), _const_spec(bsb.shape), _const_spec(w_mix.shape), vec, vec,
                  _const_spec(w_in.shape), _const_spec(w_out.shape), vec, vec],
        out_specs=row(d),
        out_shape=jax.ShapeDtypeStruct((b, s, d), F32),
        scratch_shapes=[pltpu.VMEM((tm, CMLP_WIDTH), BF16)],
        compiler_params=_cparams(("parallel", "parallel")),
        name="even_out_ffn",
    )(h, modv, mr, gu, vn, ws, bsb, w_mix, l1g, l1b, w_in, w_out, l2g, l2b)


def _post_odd_kernel(h_ref, m_ref, yc_ref, yd_ref, wm_ref, l1g_ref, l1b_ref, wi_ref, wo_ref, l2g_ref, l2b_ref,
                     o_ref, *, ctx_len, first_tile):
    tm = h_ref.shape[0]
    half = yc_ref.shape[1]
    for rows in _sub_tiles(tm):
        is_ctx = pl.program_id(1) * tm + first_tile * ROW_TILE + rows.start < ctx_len
        y = jnp.dot(yc_ref[rows, :], wm_ref[0:half, :], preferred_element_type=F32)
        y = y + jnp.dot(yd_ref[rows, :], wm_ref[half:, :], preferred_element_type=F32)
        o_ref[rows, :] = _residual_pair(h_ref[rows, :], y, m_ref, is_ctx, l1g_ref, l1b_ref,
                                        wi_ref, wo_ref, l2g_ref, l2b_ref)


def _post_odd(h, modv, yc, yd, w_mix, l1g, l1b, w_in, w_out, l2g, l2b, ctx_len, first_tile):
    b, s, d = h.shape
    tm = _post_tile(s, first_tile)
    n = (s - first_tile * ROW_TILE) // tm
    assert yc.shape[1] == n * tm and yd.shape[1] == n * tm
    row = lambda width: pl.BlockSpec((None, tm, width), lambda bi, i: (bi, i, 0))
    vec = _const_spec((1, d))
    return pl.pallas_call(
        functools.partial(_post_odd_kernel, ctx_len=ctx_len, first_tile=first_tile),
        grid=(b, n),
        in_specs=[pl.BlockSpec((None, tm, d), lambda bi, i: (bi, i + first_tile, 0)),
                  pl.BlockSpec((None, 16, d), lambda bi, i: (bi, 0, 0)),
                  row(yc.shape[2]), row(yd.shape[2]),
                  _const_spec(w_mix.shape), vec, vec,
                  _const_spec(w_in.shape), _const_spec(w_out.shape), vec, vec],
        out_specs=row(d),
        out_shape=jax.ShapeDtypeStruct((b, n * tm, d), F32),
        compiler_params=_cparams(("parallel", "parallel")),
        name="odd_out_ffn",
    )(h, modv, yc, yd, w_mix, l1g, l1b, w_in, w_out, l2g, l2b)


def _in_odd_kernel(h_ref, m_ref, w_ref, bd_ref, gq_ref, gk_ref, cos_ref, sin_ref,
                   qc_ref, qd_ref, kc_ref, vc_ref, kd_ref, vd_ref, *, ctx_len):
    tm = h_ref.shape[0]
    sub = ROW_TILE
    lane = lax.broadcasted_iota(jnp.int32, (sub, LANES), 1)
    first = (lane & 31) < 16
    rowi = lax.broadcasted_iota(jnp.int32, (LANES, sub), 0)
    pad = jnp.where(rowi == HEAD_DIM, 1.0, 0.0)
    scale = HEAD_DIM ** -0.5
    nq = Q_HEADS * HEAD_DIM

    def rms(x, gain, width):
        ms = jnp.dot((x * x).astype(BF16), bd_ref[0:width, 0:width], preferred_element_type=F32)
        return x * lax.rsqrt(ms + EPS) * gain

    def value_rows(v):
        vt = v.T
        return (jnp.where(rowi < HEAD_DIM, vt, pad).astype(BF16),
                jnp.where(rowi < HEAD_DIM, jnp.concatenate([vt[HEAD_DIM:], vt[:HEAD_DIM]], axis=0), pad).astype(BF16))

    for r, rows in enumerate(_sub_tiles(tm)):
        is_ctx = pl.program_id(1) * tm + rows.start < ctx_len
        sh = _mod_rows(m_ref, is_ctx, 0)
        sc = _mod_rows(m_ref, is_ctx, 1)
        a = (h_ref[rows, :] * (1.0 + sc) + sh).astype(BF16)
        cos = cos_ref[rows, :]
        sin = sin_ref[rows, :]

        def rope(x):
            partner = jnp.where(first, pltpu.roll(x, LANES - 16, 1), pltpu.roll(x, 16, 1))
            return x * cos + partner * sin

        cq = rms(jnp.dot(a, w_ref[:, 0:nq], preferred_element_type=F32), gq_ref[...], nq)
        for m in range(nq // LANES):
            qc_ref[m * LANES:(m + 1) * LANES, rows] = (
                rope(cq[:, m * LANES:(m + 1) * LANES]) * (scale * LOG2E)).T.astype(BF16)
        dq = jnp.dot(a, w_ref[:, nq:2 * nq], preferred_element_type=F32)
        for m in range(nq // LANES):
            qd_ref[m * LANES:(m + 1) * LANES, rows] = (
                rope(dq[:, m * LANES:(m + 1) * LANES]) * (scale * LOG2E)).T.astype(BF16)
        kv = jnp.dot(a, w_ref[:, 2 * nq:], preferred_element_type=F32)
        kc_ref[rows, :] = rope(rms(kv[:, 0:LANES], gk_ref[...], LANES)).astype(BF16)
        kd_ref[rows, :] = rope(kv[:, 2 * LANES:3 * LANES]).astype(BF16)
        vc_ref[0, r], vc_ref[1, r] = value_rows(kv[:, LANES:2 * LANES])
        vd = value_rows(kv[:, 3 * LANES:4 * LANES])
        for jj in range(KV_HEADS):
            for c in range(sub // LANES):
                vd_ref[jj, r * (sub // LANES) + c] = vd[jj][:, c * LANES:(c + 1) * LANES]


def _in_odd(h, modv, w_in, bd, gq, gk, cos_t, sin_t, ctx_len):
    b, s, d = h.shape
    tm = _post_tile(s, 0)
    nq = Q_HEADS * HEAD_DIM
    row = lambda width: pl.BlockSpec((None, tm, width), lambda bi, i: (bi, i, 0))
    tab = pl.BlockSpec((tm, LANES), lambda bi, i: (i, 0))
    qt = pl.BlockSpec((None, nq, tm), lambda bi, i: (bi, 0, i))
    sds = lambda width: jax.ShapeDtypeStruct((b, s, width), BF16)
    return pl.pallas_call(
        functools.partial(_in_odd_kernel, ctx_len=ctx_len),
        grid=(b, s // tm),
        in_specs=[row(d),
                  pl.BlockSpec((None, 16, d), lambda bi, i: (bi, 0, 0)),
                  _const_spec(w_in.shape), _const_spec(bd.shape),
                  _const_spec(gq.shape), _const_spec(gk.shape), tab, tab],
        out_specs=[qt, qt, row(LANES),
                   pl.BlockSpec((None, KV_HEADS, tm // ATT_TK, LANES, ATT_TK), lambda bi, i: (bi, 0, i, 0, 0)),
                   row(LANES),
                   pl.BlockSpec((None, KV_HEADS, tm // LANES, LANES, LANES), lambda bi, i: (bi, 0, i, 0, 0))],
        out_shape=[jax.ShapeDtypeStruct((b, nq, s), BF16), jax.ShapeDtypeStruct((b, nq, s), BF16), sds(LANES),
                   jax.ShapeDtypeStruct((b, KV_HEADS, s // ATT_TK, LANES, ATT_TK), BF16), sds(LANES),
                   jax.ShapeDtypeStruct((b, KV_HEADS, s // LANES, LANES, LANES), BF16)],
        compiler_params=_cparams(("parallel", "parallel")),
        name="odd_in_proj",
    )(h, modv, w_in, bd, gq, gk, cos_t, sin_t)


def _stack_queries(q_ref, qs_ref, j, tq):
    for hh in range(GROUP):
        qh = q_ref[hh * HEAD_DIM:(hh + 1) * HEAD_DIM, :]
        zero = jnp.zeros_like(qh)
        qs_ref[0:HEAD_DIM, hh * tq:(hh + 1) * tq] = jnp.where(j == 0, qh, zero)
        qs_ref[HEAD_DIM:, hh * tq:(hh + 1) * tq] = jnp.where(j == 0, zero, qh)


def _store_heads(o, o_ref, tq):
    for m in range(GROUP // 2):
        pair = jnp.concatenate([o[:, (2 * m) * tq:(2 * m + 1) * tq],
                                o[:, (2 * m + 1) * tq:(2 * m + 2) * tq]], axis=0)
        o_ref[:, m * LANES:(m + 1) * LANES] = pair.T.astype(o_ref.dtype)


def _gattn_kernel(q_ref, k_ref, v_ref, o_ref, *scratch, ctx_len, first_tile):
    tq = q_ref.shape[1]
    gw = GROUP * HEAD_DIM
    _, ntile, _, tv = v_ref.shape
    chains = [scratch[6 * j:6 * j + 6] for j in range(KV_HEADS)]
    big = chains[0][4].shape[0]
    nbig = (ntile * tv - ctx_len) // big
    qi = pl.program_id(1) + first_tile

    def scores(j, start, s_ref):
        if not isinstance(start, int):
            start = pl.multiple_of(start, tv)
        s_ref[...] = jnp.dot(k_ref[pl.ds(start, s_ref.shape[0]), :], chains[j][0][...],
                             preferred_element_type=F32)

    def update(j, start, s_ref):
        m_ref, acc_ref = chains[j][1], chains[j][2]
        sc = s_ref[...]
        m_prev = m_ref[...]
        m_new = jnp.maximum(m_prev, jnp.max(sc, axis=0, keepdims=True))
        alpha = jnp.exp2(m_prev - m_new)
        p = jnp.exp2(sc - m_new).astype(BF16)
        t0 = start // tv
        vt = jnp.concatenate([v_ref[j, t0 + i] for i in range(s_ref.shape[0] // tv)], axis=1)
        acc_ref[...] = alpha * acc_ref[...] + jnp.dot(vt, p, preferred_element_type=F32)
        m_ref[...] = m_new

    def latent(n):
        return ctx_len + n * big

    for j, (qs_ref, m_ref, acc_ref, sc_ref, _, _) in enumerate(chains):
        _stack_queries(q_ref.at[j * gw:(j + 1) * gw, :], qs_ref, j, tq)
        m_ref[...] = jnp.full(m_ref.shape, NEG_INF, F32)
        acc_ref[...] = jnp.zeros(acc_ref.shape, F32)
        scores(j, 0, sc_ref)

    @pl.when(qi * tq >= ctx_len)
    def _():
        for j, (_, _, _, sc_ref, sa_ref, _) in enumerate(chains):
            scores(j, latent(0), sa_ref)
            update(j, 0, sc_ref)

        def pair(u, _):
            for j, (_, _, _, _, sa_ref, sb_ref) in enumerate(chains):
                scores(j, latent(2 * u + 1), sb_ref)
                update(j, latent(2 * u), sa_ref)
            for j, (_, _, _, _, sa_ref, sb_ref) in enumerate(chains):
                scores(j, latent(2 * u + 2), sa_ref)
                update(j, latent(2 * u + 1), sb_ref)
            return 0

        lax.fori_loop(0, (nbig - 1) // 2, pair, 0)
        for j, (_, _, _, _, sa_ref, sb_ref) in enumerate(chains):
            if nbig % 2 == 0:
                scores(j, latent(nbig - 1), sb_ref)
                update(j, latent(nbig - 2), sa_ref)
                update(j, latent(nbig - 1), sb_ref)
            else:
                update(j, latent(nbig - 1), sa_ref)

    @pl.when(qi * tq < ctx_len)
    def _():
        for j in range(KV_HEADS):
            update(j, 0, chains[j][3])

    for j in range(KV_HEADS):
        acc_ref = chains[j][2]
        _store_heads(acc_ref[0:HEAD_DIM, :] / acc_ref[HEAD_DIM:HEAD_DIM + 1, :],
                     o_ref.at[:, j * gw:(j + 1) * gw], tq)


def _global_attention(qt, k, vt, ctx_len, first_tile):
    b, nq, s = qt.shape
    tq = ATT_TQ
    assert vt.shape[4] == ATT_TK and ctx_len % ATT_TK == 0 and (s - ctx_len) % ATT_BIG == 0
    cols = GROUP * tq
    per_head = [pltpu.VMEM((LANES, cols), BF16), pltpu.VMEM((1, cols), F32), pltpu.VMEM((LANES, cols), F32),
                pltpu.VMEM((ctx_len, cols), F32), pltpu.VMEM((ATT_BIG, cols), F32),
                pltpu.VMEM((ATT_BIG, cols), F32)]
    return pl.pallas_call(
        functools.partial(_gattn_kernel, ctx_len=ctx_len, first_tile=first_tile),
        grid=(b, s // tq - first_tile),
        in_specs=[pl.BlockSpec((None, nq, tq), lambda bi, i: (bi, 0, i + first_tile)),
                  pl.BlockSpec((None, s, LANES), lambda bi, i: (bi, 0, 0)),
                  pl.BlockSpec((None,) + vt.shape[1:], lambda bi, i: (bi, 0, 0, 0, 0))],
        out_specs=pl.BlockSpec((None, tq, nq), lambda bi, i: (bi, i, 0)),
        out_shape=jax.ShapeDtypeStruct((b, s - first_tile * tq, nq), BF16),
        scratch_shapes=per_head * KV_HEADS,
        compiler_params=_cparams(("parallel", "arbitrary")),
        name="global_attention",
    )(qt, k, vt)


def _wattn_kernel(q_ref, k_ref, v_ref, sink_ref, bias_ref, o_ref, *scratch, ctx_len, span, first_tile):
    tq = q_ref.shape[1]
    s = k_ref.shape[0]
    gw = GROUP * HEAD_DIM
    qi = pl.program_id(1) + first_tile
    is_ctx = qi * tq < ctx_len

    def scores(j, window_start):
        qs_ref, sc_ref, sw_ref = scratch[3 * j:3 * j + 3]
        _stack_queries(q_ref.at[j * gw:(j + 1) * gw, :], qs_ref, j, tq)
        qs = qs_ref[...]
        sc_ref[...] = jnp.dot(k_ref[0:ctx_len, :], qs, preferred_element_type=F32)
        if window_start is not None:
            sw_ref[...] = (jnp.dot(k_ref[pl.ds(window_start, span), :], qs, preferred_element_type=F32)
                           + bias_ref[...])

    def finish(j, window_start):
        _, sc_ref, sw_ref = scratch[3 * j:3 * j + 3]
        sink = sink_ref[j] * LOG2E
        v_ctx = jnp.concatenate([v_ref[j, t] for t in range(ctx_len // LANES)], axis=1)
        m = jnp.maximum(jnp.max(sc_ref[...], axis=0, keepdims=True), sink)
        if window_start is not None:
            m = jnp.maximum(m, jnp.max(sw_ref[...], axis=0, keepdims=True))
        acc = jnp.dot(v_ctx, jnp.exp2(sc_ref[...] - m).astype(BF16), preferred_element_type=F32)
        if window_start is not None:
            t0 = window_start // LANES
            v_win = jnp.concatenate([v_ref[j, t0 + i] for i in range(span // LANES)], axis=1)
            acc = acc + jnp.dot(v_win, jnp.exp2(sw_ref[...] - m).astype(BF16), preferred_element_type=F32)
        den = acc[HEAD_DIM:HEAD_DIM + 1, :] + jnp.exp2(sink - m)
        _store_heads(acc[0:HEAD_DIM, :] / den, o_ref.at[:, j * gw:(j + 1) * gw], tq)

    def both(window_start):
        scores(0, window_start)
        scores(1, window_start)
        finish(0, window_start)
        finish(1, window_start)

    @pl.when(is_ctx)
    def _():
        both(None)

    @pl.when(jnp.logical_not(is_ctx))
    def _():
        both(pl.multiple_of(jnp.clip(qi * tq - WINDOW, ctx_len, s - span), LANES))


def _window_attention(qt, k, vt, sink, ctx_len, first_tile):
    b, nq, s = qt.shape
    tq = ATT_TQ
    gw = GROUP * HEAD_DIM
    span = tq + 2 * WINDOW
    assert tq >= WINDOW and (s - ctx_len) // tq >= 2
    sink_row = jnp.repeat(sink.reshape(KV_HEADS, 1, GROUP), tq, axis=2)
    r = jnp.arange(span)[:, None]
    qcol = (jnp.arange(GROUP * tq) % tq)[None, :]
    bias = jnp.stack([jnp.where(jnp.abs(off + r - qcol) <= WINDOW, 0.0, NEG_INF)
                      for off in (0, -WINDOW, tq - span)]).astype(F32)

    def placement(i):
        lo = (i + first_tile) * tq - WINDOW
        return jnp.where(lo < ctx_len, 0, jnp.where(lo > s - span, 2, 1))

    return pl.pallas_call(
        functools.partial(_wattn_kernel, ctx_len=ctx_len, span=span, first_tile=first_tile),
        grid=(b, s // tq - first_tile),
        in_specs=[pl.BlockSpec((None, nq, tq), lambda bi, i: (bi, 0, i + first_tile)),
                  pl.BlockSpec((None, s, LANES), lambda bi, i: (bi, 0, 0)),
                  pl.BlockSpec((None,) + vt.shape[1:], lambda bi, i: (bi, 0, 0, 0, 0)),
                  pl.BlockSpec((KV_HEADS, 1, GROUP * tq), lambda bi, i: (0, 0, 0)),
                  pl.BlockSpec((None, span, GROUP * tq), lambda bi, i: (placement(i), 0, 0))],
        out_specs=pl.BlockSpec((None, tq, nq), lambda bi, i: (bi, i, 0)),
        out_shape=jax.ShapeDtypeStruct((b, s - first_tile * tq, nq), BF16),
        scratch_shapes=[pltpu.VMEM((LANES, GROUP * tq), BF16), pltpu.VMEM((ctx_len, GROUP * tq), F32),
                        pltpu.VMEM((span, GROUP * tq), F32)] * KV_HEADS,
        compiler_params=_cparams(("parallel", "parallel")),
        name="window_attention",
    )(qt, k, vt, sink_row, bias)


def _rope_tables(n_lat, ctx_len):
    rows = n_lat // GRID_W
    row = jnp.repeat(jnp.arange(rows, dtype=F32), GRID_W)
    col = jnp.tile(jnp.arange(GRID_W, dtype=F32), rows)
    nf = HEAD_DIM // 4
    inv_freq = ROPE_THETA ** (-jnp.arange(nf, dtype=F32) / nf)
    ang_r = row[:, None] * inv_freq
    ang_c = col[:, None] * inv_freq
    cos = jnp.concatenate([jnp.cos(ang_r)] * 2 + [jnp.cos(ang_c)] * 2, axis=1)
    sin = jnp.concatenate([-jnp.sin(ang_r), jnp.sin(ang_r), -jnp.sin(ang_c), jnp.sin(ang_c)], axis=1)
    cos = jnp.concatenate([jnp.ones((ctx_len, HEAD_DIM), F32), cos], axis=0)
    sin = jnp.concatenate([jnp.zeros((ctx_len, HEAD_DIM), F32), sin], axis=0)
    return jnp.tile(cos, (1, LANES // HEAD_DIM)), jnp.tile(sin, (1, LANES // HEAD_DIM))


def kernel(x, c, ctx, c_ctx, ada_w, ada_b, ln1_g, ln1_b, ln2_g, ln2_b, ffn_w_in, ffn_w_out,
           ev_w_in, ev_w_out, rg_conv_w, rg_conv_b, rg_gate_w, rg_gate_b, rg_lambda, cm_w_s, cm_b_s,
           od_w_in, od_w_out, qn_g, kn_g, sink):
    b, t, d = x.shape
    ctx_len = ctx.shape[1]
    s = ctx_len + t
    depth = ada_w.shape[0]
    assert d == D_MODEL and depth == DEPTH and b + 1 <= 16
    assert ctx_len % ROW_TILE == 0 and t % ROW_TILE == 0 and t % GRID_W == 0 and ATT_TQ == ATT_TK == ROW_TILE

    c_rows = jnp.zeros((16, d), F32).at[:b].set(c).at[b].set(c_ctx)
    mods = _ada_vectors(c_rows, ada_w, ada_b).reshape(depth, 16, 6, d)
    modv = jnp.zeros((depth, b, 16, d), F32)
    modv = modv.at[:, :, 0:6].set(mods[:, :b])
    modv = modv.at[:, :, 8:14].set(jnp.broadcast_to(mods[:, b][:, None], (depth, b, 6, d)))

    cos_t, sin_t = _rope_tables(t, ctx_len)
    nq = Q_HEADS * HEAD_DIM
    bd = jnp.kron(jnp.eye(nq // HEAD_DIM, dtype=F32), jnp.full((HEAD_DIM, HEAD_DIM), 1.0 / HEAD_DIM, F32)).astype(BF16)
    assert sum(FFN_CHUNKS) == FFN_HIDDEN

    h = jnp.concatenate([ctx, x], axis=1)
    rows_ctx = ctx_len
    for l in range(depth):
        j = l // 2
        mv = modv[l]
        norms = (ln1_g[l].reshape(1, d), ln1_b[l].reshape(1, d), ffn_w_in[l].astype(BF16),
                 ffn_w_out[l].astype(BF16), ln2_g[l].reshape(1, d), ln2_b[l].reshape(1, d))
        if l % 2 == 0:
            gg, xr, gu, vn = _in_even(h, mv, ev_w_in[j].astype(BF16), ctx_len)
            gw = jnp.transpose(rg_gate_w[j], (2, 0, 3, 1, 4)).reshape(RNN_HEADS, 2, LANES, 2 * LANES).astype(BF16)
            gb = jnp.transpose(rg_gate_b[j].reshape(2, 2, RNN_HEADS, LANES), (2, 0, 1, 3)).reshape(RNN_HEADS, 2, 1, 2 * LANES)
            mr = _rglru(xr, gg, rg_conv_w[j], rg_conv_b[j].reshape(1, -1), gw, gb, rg_lambda[j], ctx_len)
            bsb = jnp.broadcast_to(cm_b_s[j][:, :, None], (CMLP_GROUPS, CHUNK, CMLP_WIDTH // CMLP_GROUPS))
            h = _post_even(h, mv, mr, gu, vn, cm_w_s[j].astype(BF16), bsb, ev_w_out[j].astype(BF16), *norms, ctx_len)
        else:
            gq = jnp.tile(qn_g[j], Q_HEADS).reshape(1, nq)
            gk = jnp.tile(kn_g[j], LANES // HEAD_DIM).reshape(1, LANES)
            qc, qd, kc, vc, kd, vd = _in_odd(h, mv, od_w_in[j].astype(BF16), bd, gq, gk, cos_t, sin_t, ctx_len)
            skip = ctx_len // ATT_TQ if l == depth - 1 else 0
            yc = _global_attention(qc, kc, vc, ctx_len, skip)
            yd = _window_attention(qd, kd, vd, sink[j], ctx_len, skip)
            h = _post_odd(h, mv, yc, yd, od_w_out[j].astype(BF16), *norms, ctx_len, skip)
            if skip:
                rows_ctx = 0
    return h[:, rows_ctx:, :]
```

```python
import functools

import jax
import jax.numpy as jnp
from jax import lax
from jax.experimental import pallas as pl
from jax.experimental.pallas import tpu as pltpu

F32 = jnp.float32
BF16 = jnp.bfloat16

D_MODEL = 1024
DEPTH = 4
GRID_W = 64
RNN_WIDTH = D_MODEL
RNN_HEADS = RNN_WIDTH // 128
CONV_W = 4
LRU_C = 8.0
CMLP_WIDTH = D_MODEL // 2
CMLP_GROUPS = 4
CHUNK = 128
HEAD_DIM = 64
Q_HEADS = 8
KV_HEADS = 2
GROUP = Q_HEADS // KV_HEADS
WINDOW = 128
ROPE_THETA = 10000.0
NEG_INF = -1e30
FFN_HIDDEN = 2816
FFN_CHUNKS = (1024, 1024, 768)
ALPHA = (2.0 * DEPTH) ** 0.25
LOG2E = 1.4426950408889634
EPS = 1e-6

LANES = 128
ROW_TILE = 256
SCAN_BLOCK = 128
SCAN_RUN = 4
ATT_TQ = 256
ATT_TK = 256
ATT_BIG = 1024
VMEM_LIMIT = 56 * 1024 * 1024


def _cparams(sem):
    return pltpu.CompilerParams(dimension_semantics=sem, vmem_limit_bytes=VMEM_LIMIT)


def _const_spec(shape):
    nd = len(shape)
    return pl.BlockSpec(shape, lambda *_: (0,) * nd, pipeline_mode=pl.Buffered(1))


def _gelu(x):
    return 0.5 * x * (1.0 + jnp.tanh(0.7978845608028654 * (x + 0.044715 * (x * x * x))))


def _normalise(x):
    mu = jnp.mean(x, axis=-1, keepdims=True)
    xc = x - mu
    var = jnp.mean(xc * xc, axis=-1, keepdims=True)
    return xc * lax.rsqrt(var + EPS)


def _mod_rows(m_ref, is_ctx, lat_row):
    return jnp.where(is_ctx, m_ref[lat_row + 8:lat_row + 9, :], m_ref[lat_row:lat_row + 1, :])


def _mod_kernel(c_ref, w_ref, b_ref, o_ref):
    c = c_ref[...]
    s = c * jax.nn.sigmoid(c)
    o_ref[...] = jnp.dot(s, w_ref[...], preferred_element_type=F32,
                         precision=lax.Precision.HIGHEST) + b_ref[...]


def _ada_vectors(c_rows, ada_w, ada_b):
    depth, d, n = ada_w.shape
    rows = c_rows.shape[0]
    nb = 1536
    return pl.pallas_call(
        _mod_kernel,
        grid=(depth, n // nb),
        in_specs=[pl.BlockSpec((rows, d), lambda l, j: (0, 0)),
                  pl.BlockSpec((None, d, nb), lambda l, j: (l, 0, j)),
                  pl.BlockSpec((None, 1, nb), lambda l, j: (l, 0, j))],
        out_specs=pl.BlockSpec((None, rows, nb), lambda l, j: (l, 0, j)),
        out_shape=jax.ShapeDtypeStruct((depth, rows, n), F32),
        compiler_params=_cparams(("parallel", "parallel")),
        name="ada_vectors",
    )(c_rows, ada_w, ada_b.reshape(depth, 1, n))


def _in_even_kernel(h_ref, m_ref, w_ref, gg_ref, xr_ref, gu_ref, vn_ref, *, ctx_len):
    tm = h_ref.shape[0]
    w = RNN_WIDTH
    for rows in _sub_tiles(tm):
        is_ctx = pl.program_id(1) * tm + rows.start < ctx_len
        sh = _mod_rows(m_ref, is_ctx, 0)
        sc = _mod_rows(m_ref, is_ctx, 1)
        a = (h_ref[rows, :] * (1.0 + sc) + sh).astype(BF16)
        gate = jnp.dot(a, w_ref[:, 0:w], preferred_element_type=F32)
        gg_ref[rows, :] = _gelu(gate).astype(BF16)
        xr_ref[rows, :] = jnp.dot(a, w_ref[:, w:2 * w], preferred_element_type=F32)
        u = jnp.dot(a, w_ref[:, 2 * w:2 * w + CMLP_WIDTH], preferred_element_type=F32)
        gu_ref[rows, :] = _gelu(u).astype(BF16)
        v = jnp.dot(a, w_ref[:, 2 * w + CMLP_WIDTH:], preferred_element_type=F32)
        vn_ref[rows, :] = _normalise(_gelu(v)).astype(BF16)


def _in_even(h, modv, w_in, ctx_len):
    b, s, d = h.shape
    tm = _post_tile(s, 0)
    n_in = w_in.shape[1]
    row = lambda width: pl.BlockSpec((None, tm, width), lambda bi, i: (bi, i, 0))
    return pl.pallas_call(
        functools.partial(_in_even_kernel, ctx_len=ctx_len),
        grid=(b, s // tm),
        in_specs=[row(d),
                  pl.BlockSpec((None, 16, d), lambda bi, i: (bi, 0, 0)),
                  _const_spec((d, n_in))],
        out_specs=[row(RNN_WIDTH), row(RNN_WIDTH), row(CMLP_WIDTH), row(CMLP_WIDTH)],
        out_shape=[jax.ShapeDtypeStruct((b, s, RNN_WIDTH), BF16),
                   jax.ShapeDtypeStruct((b, s, RNN_WIDTH), F32),
                   jax.ShapeDtypeStruct((b, s, CMLP_WIDTH), BF16),
                   jax.ShapeDtypeStruct((b, s, CMLP_WIDTH), BF16)],
        compiler_params=_cparams(("parallel", "parallel")),
        name="even_in_proj",
    )(h, modv, w_in)


def _block_scan(a_ref, b_ref, o_ref, t0, carry, reverse):
    n = SCAN_RUN
    sub = lax.broadcasted_iota(jnp.int32, (8, a_ref.shape[1]), 0)
    ngroup = a_ref.shape[0] // (8 * n)
    steps = range(n - 1, -1, -1) if reverse else range(n)
    for k in (range(ngroup - 1, -1, -1) if reverse else range(ngroup)):
        base = 8 * n * k
        av = [a_ref[pl.ds(base + i, 8, stride=n), :] for i in range(n)]
        bv = [b_ref[pl.ds(base + i, 8, stride=n), :] for i in range(n)]
        hloc, aloc = [None] * n, [None] * n
        prev = None
        for i in steps:
            hloc[i] = bv[i] if prev is None else av[i] * hloc[prev] + bv[i]
            aloc[i] = av[i] if prev is None else av[i] * aloc[prev]
            prev = i
        atot, htot = aloc[prev], hloc[prev]
        for d in (1, 2, 4):
            shift, ok = (8 - d, sub < 8 - d) if reverse else (d, sub >= d)
            htot = jnp.where(ok, htot + atot * pltpu.roll(htot, shift, 0), htot)
            atot = jnp.where(ok, atot * pltpu.roll(atot, shift, 0), atot)
        shift, ok = (7, sub < 7) if reverse else (1, sub >= 1)
        enter = (jnp.where(ok, pltpu.roll(htot, shift, 0), 0.0)
                 + jnp.where(ok, pltpu.roll(atot, shift, 0), 1.0) * carry)
        for i in range(n):
            o_ref[pl.ds(t0 + base + i, 8, stride=n), :] = hloc[i] + aloc[i] * enter
        last = 0 if reverse else 7
        carry = (jnp.broadcast_to(htot[last:last + 1, :], htot.shape)
                 + jnp.broadcast_to(atot[last:last + 1, :], atot.shape) * carry)
    return carry


def _scan_kernel(xr_ref, gg_ref, cw_ref, cb_ref, gw_ref, gb_ref, lam_ref, out_ref, xc_ref, rf_ref, rr_ref,
                 ab0_ref, ab1_ref, g0_ref, g1_ref, *, ctx_len, tb):
    s = xr_ref.shape[0]
    nblk = s // tb
    ncb = ctx_len // tb
    cw = cw_ref[...]
    cb = cb_ref[...]
    row = lax.broadcasted_iota(jnp.int32, (tb, LANES), 0)

    def conv(blk, _):
        t0 = pl.multiple_of(blk * tb, tb)
        x = xr_ref[pl.ds(t0, tb), :]
        prev = xr_ref[pl.ds(pl.multiple_of(jnp.maximum(t0 - 8, 0), 8), 8), :]
        nxt = xr_ref[pl.ds(pl.multiple_of(jnp.minimum(t0 + tb, s - 8), 8), 8), :]
        pf = jnp.where((blk == 0) | (blk == ncb), 0.0, 1.0)
        nf = jnp.where((blk == ncb - 1) | (blk == nblk - 1), 0.0, 1.0)
        p6 = prev[6:7, :] * pf
        p7 = prev[7:8, :] * pf
        n0 = nxt[0:1, :] * nf
        xm1 = jnp.where(row == 0, p7, pltpu.roll(x, 1, 0))
        xm2 = jnp.where(row == 0, p6, jnp.where(row == 1, p7, pltpu.roll(x, 2, 0)))
        xp1 = jnp.where(row == tb - 1, n0, pltpu.roll(x, tb - 1, 0))
        xc_ref[pl.ds(t0, tb), :] = (xm2 * cw[0:1, :] + xm1 * cw[1:2, :] + x * cw[2:3, :] + xp1 * cw[3:4, :]
                                    + cb)
        return 0

    lax.fori_loop(0, nblk, conv, 0)

    def block_start(step, d):
        blk = step if d == 0 else jnp.where(step < ncb, ncb - 1 - step, nblk - 1 - (step - ncb))
        return pl.multiple_of(blk * tb, tb)

    def gate_matmuls(step, g_ref):
        step = jnp.minimum(step, nblk - 1)
        for d in range(2):
            xc = xc_ref[pl.ds(block_start(step, d), tb), :]
            g_ref[d] = jnp.dot(xc.astype(BF16), gw_ref[d], preferred_element_type=F32)

    def stash_coeffs(step, g_ref, ab_ref):
        step = jnp.minimum(step, nblk - 1)
        for d in range(2):
            xc = xc_ref[pl.ds(block_start(step, d), tb), :]
            g = g_ref[d] + gb_ref[d]
            r = jax.nn.sigmoid(g[:, :LANES])
            ig = jax.nn.sigmoid(g[:, LANES:])
            z = -lam_ref[d:d + 1, :]
            softplus = jnp.maximum(z, 0.0) + jnp.log(1.0 + jnp.exp(-jnp.abs(z)))
            a = jnp.exp2(r * ((-LRU_C * LOG2E) * softplus))
            ab_ref[2 * d] = a
            ab_ref[2 * d + 1] = jnp.sqrt(1.0 - a * a) * (ig * xc)

    def scan(step, ab_ref, cf, cr):
        cf = _block_scan(ab_ref.at[0], ab_ref.at[1], rf_ref, block_start(step, 0), cf, False)
        cr = _block_scan(ab_ref.at[2], ab_ref.at[3], rr_ref, block_start(step, 1), cr, True)
        return cf, cr

    gate_matmuls(0, g0_ref)
    stash_coeffs(0, g0_ref, ab0_ref)
    gate_matmuls(1, g1_ref)
    gate_matmuls(2, g0_ref)

    def two_steps(u, carry):
        cf, cr = carry
        cf, cr = scan(2 * u, ab0_ref, cf, cr)
        stash_coeffs(2 * u + 1, g1_ref, ab1_ref)
        cf, cr = scan(2 * u + 1, ab1_ref, cf, cr)
        stash_coeffs(2 * u + 2, g0_ref, ab0_ref)
        gate_matmuls(2 * u + 3, g1_ref)
        gate_matmuls(2 * u + 4, g0_ref)
        return cf, cr

    zero = jnp.zeros((8, LANES), F32)
    lax.fori_loop(0, nblk // 2, two_steps, (zero, zero))

    def combine(j, _):
        t0 = pl.multiple_of(j * tb, tb)
        rec = rf_ref[pl.ds(t0, tb), :] + rr_ref[pl.ds(t0, tb), :]
        out_ref[pl.ds(t0, tb), :] = (gg_ref[pl.ds(t0, tb), :].astype(F32) * rec).astype(BF16)
        return 0

    lax.fori_loop(0, nblk, combine, 0)


def _rglru(xr, gg, conv_w, conv_b, gate_w, gate_b, lam, ctx_len):
    b, s, w = xr.shape
    nh = w // LANES
    assert (s // SCAN_BLOCK) % 2 == 0 and ctx_len % SCAN_BLOCK == 0
    col = lambda dt: pl.BlockSpec((None, s, LANES), lambda bi, hd: (bi, 0, hd))
    return pl.pallas_call(
        functools.partial(_scan_kernel, ctx_len=ctx_len, tb=SCAN_BLOCK),
        grid=(b, nh),
        in_specs=[col(F32), col(BF16),
                  pl.BlockSpec((CONV_W, LANES), lambda bi, hd: (0, hd)),
                  pl.BlockSpec((1, LANES), lambda bi, hd: (0, hd)),
                  pl.BlockSpec((None, 2, LANES, 2 * LANES), lambda bi, hd: (hd, 0, 0, 0)),
                  pl.BlockSpec((None, 2, 1, 2 * LANES), lambda bi, hd: (hd, 0, 0, 0)),
                  pl.BlockSpec((2, LANES), lambda bi, hd: (0, hd))],
        out_specs=col(BF16),
        out_shape=jax.ShapeDtypeStruct((b, s, w), BF16),
        scratch_shapes=[pltpu.VMEM((s, LANES), F32), pltpu.VMEM((s, LANES), F32), pltpu.VMEM((s, LANES), F32),
                        pltpu.VMEM((4, SCAN_BLOCK, LANES), F32), pltpu.VMEM((4, SCAN_BLOCK, LANES), F32),
                        pltpu.VMEM((2, SCAN_BLOCK, 2 * LANES), F32), pltpu.VMEM((2, SCAN_BLOCK, 2 * LANES), F32)],
        compiler_params=_cparams(("parallel", "parallel")),
        name="rglru_scan",
    )(xr, gg, conv_w, conv_b, gate_w, gate_b, lam)


def _residual_ln(h, y, gate, g, b):
    return _normalise(ALPHA * h + gate * y) * g + b


def _swiglu(a, wi_ref, wo_ref):
    acc = None
    c0 = 0
    for width in FFN_CHUNKS:
        zg = jnp.dot(a, wi_ref[:, c0:c0 + width], preferred_element_type=F32)
        zu = jnp.dot(a, wi_ref[:, FFN_HIDDEN + c0:FFN_HIDDEN + c0 + width], preferred_element_type=F32)
        hm = (zg * jax.nn.sigmoid(zg) * zu).astype(BF16)
        y = jnp.dot(hm, wo_ref[c0:c0 + width, :], preferred_element_type=F32)
        acc = y if acc is None else acc + y
        c0 += width
    return acc


def _residual_pair(h, y_mix, m_ref, is_ctx, l1g_ref, l1b_ref, wi_ref, wo_ref, l2g_ref, l2b_ref):
    h1 = _residual_ln(h, y_mix, _mod_rows(m_ref, is_ctx, 2), l1g_ref[...], l1b_ref[...])
    a = (h1 * (1.0 + _mod_rows(m_ref, is_ctx, 4)) + _mod_rows(m_ref, is_ctx, 3)).astype(BF16)
    return _residual_ln(h1, _swiglu(a, wi_ref, wo_ref), _mod_rows(m_ref, is_ctx, 5), l2g_ref[...], l2b_ref[...])


def _sub_tiles(tm):
    return [slice(r * ROW_TILE, (r + 1) * ROW_TILE) for r in range(tm // ROW_TILE)]


def _post_tile(n_rows, first_tile):
    return 3 * ROW_TILE if first_tile == 0 and n_rows % (3 * ROW_TILE) == 0 else ROW_TILE


def _post_even_kernel(h_ref, m_ref, mr_ref, gu_ref, vn_ref, ws_ref, bs_ref, wm_ref, l1g_ref, l1b_ref,
                      wi_ref, wo_ref, l2g_ref, l2b_ref, o_ref, gm_ref, *, ctx_len):
    tm = h_ref.shape[0]
    gw = CMLP_WIDTH // CMLP_GROUPS
    for rows in _sub_tiles(tm):
        is_ctx = pl.program_id(1) * tm + rows.start < ctx_len
        for c in range(rows.start, rows.stop, CHUNK):
            crow = slice(c, c + CHUNK)
            for g in range(CMLP_GROUPS):
                cols = slice(g * gw, (g + 1) * gw)
                mixed = jnp.dot(ws_ref[g], vn_ref[crow, cols], preferred_element_type=F32) + bs_ref[g]
                gm_ref[crow, cols] = (gu_ref[crow, cols].astype(F32) * mixed).astype(BF16)
        y = jnp.dot(mr_ref[rows, :], wm_ref[0:RNN_WIDTH, :], preferred_element_type=F32)
        y = y + jnp.dot(gm_ref[rows, :], wm_ref[RNN_WIDTH:, :], preferred_element_type=F32)
        o_ref[rows, :] = _residual_pair(h_ref[rows, :], y, m_ref, is_ctx, l1g_ref, l1b_ref,
                                        wi_ref, wo_ref, l2g_ref, l2b_ref)


def _post_even(h, modv, mr, gu, vn, ws, bsb, w_mix, l1g, l1b, w_in, w_out, l2g, l2b, ctx_len):
    b, s, d = h.shape
    tm = _post_tile(s, 0)
    row = lambda width: pl.BlockSpec((None, tm, width), lambda bi, i: (bi, i, 0))
    vec = _const_spec((1, d))
    return pl.pallas_call(
        functools.partial(_post_even_kernel, ctx_len=ctx_len),
        grid=(b, s // tm),
        in_specs=[row(d),
                  pl.BlockSpec((None, 16, d), lambda bi, i: (bi, 0, 0)),
                  row(RNN_WIDTH), row(CMLP_WIDTH), row(CMLP_WIDTH),
                  _const_spec(ws.shape), _const_spec(bsb.shape), _const_spec(w_mix.shape), vec, vec,
                  _const_spec(w_in.shape), _const_spec(w_out.shape), vec, vec],
        out_specs=row(d),
        out_shape=jax.ShapeDtypeStruct((b, s, d), F32),
        scratch_shapes=[pltpu.VMEM((tm, CMLP_WIDTH), BF16)],
        compiler_params=_cparams(("parallel", "parallel")),
        name="even_out_ffn",
    )(h, modv, mr, gu, vn, ws, bsb, w_mix, l1g, l1b, w_in, w_out, l2g, l2b)


def _post_odd_kernel(h_ref, m_ref, yc_ref, yd_ref, wm_ref, l1g_ref, l1b_ref, wi_ref, wo_ref, l2g_ref, l2b_ref,
                     o_ref, *, ctx_len, first_tile):
    tm = h_ref.shape[0]
    half = yc_ref.shape[1]
    for rows in _sub_tiles(tm):
        is_ctx = pl.program_id(1) * tm + first_tile * ROW_TILE + rows.start < ctx_len
        y = jnp.dot(yc_ref[rows, :], wm_ref[0:half, :], preferred_element_type=F32)
        y = y + jnp.dot(yd_ref[rows, :], wm_ref[half:, :], preferred_element_type=F32)
        o_ref[rows, :] = _residual_pair(h_ref[rows, :], y, m_ref, is_ctx, l1g_ref, l1b_ref,
                                        wi_ref, wo_ref, l2g_ref, l2b_ref)


def _post_odd(h, modv, yc, yd, w_mix, l1g, l1b, w_in, w_out, l2g, l2b, ctx_len, first_tile):
    b, s, d = h.shape
    tm = _post_tile(s, first_tile)
    n = (s - first_tile * ROW_TILE) // tm
    assert yc.shape[1] == n * tm and yd.shape[1] == n * tm
    row = lambda width: pl.BlockSpec((None, tm, width), lambda bi, i: (bi, i, 0))
    vec = _const_spec((1, d))
    return pl.pallas_call(
        functools.partial(_post_odd_kernel, ctx_len=ctx_len, first_tile=first_tile),
        grid=(b, n),
        in_specs=[pl.BlockSpec((None, tm, d), lambda bi, i: (bi, i + first_tile, 0)),
                  pl.BlockSpec((None, 16, d), lambda bi, i: (bi, 0, 0)),
                  row(yc.shape[2]), row(yd.shape[2]),
                  _const_spec(w_mix.shape), vec, vec,
                  _const_spec(w_in.shape), _const_spec(w_out.shape), vec, vec],
        out_specs=row(d),
        out_shape=jax.ShapeDtypeStruct((b, n * tm, d), F32),
        compiler_params=_cparams(("parallel", "parallel")),
        name="odd_out_ffn",
    )(h, modv, yc, yd, w_mix, l1g, l1b, w_in, w_out, l2g, l2b)


def _in_odd_kernel(h_ref, m_ref, w_ref, bd_ref, gq_ref, gk_ref, cos_ref, sin_ref,
                   qc_ref, qd_ref, kc_ref, vc_ref, kd_ref, vd_ref, *, ctx_len):
    tm = h_ref.shape[0]
    sub = ROW_TILE
    lane = lax.broadcasted_iota(jnp.int32, (sub, LANES), 1)
    first = (lane & 31) < 16
    rowi = lax.broadcasted_iota(jnp.int32, (LANES, sub), 0)
    pad = jnp.where(rowi == HEAD_DIM, 1.0, 0.0)
    scale = HEAD_DIM ** -0.5
    nq = Q_HEADS * HEAD_DIM

    def rms(x, gain, width):
        ms = jnp.dot((x * x).astype(BF16), bd_ref[0:width, 0:width], preferred_element_type=F32)
        return x * lax.rsqrt(ms + EPS) * gain

    def value_rows(v):
        vt = v.T
        return (jnp.where(rowi < HEAD_DIM, vt, pad).astype(BF16),
                jnp.where(rowi < HEAD_DIM, jnp.concatenate([vt[HEAD_DIM:], vt[:HEAD_DIM]], axis=0), pad).astype(BF16))

    for r, rows in enumerate(_sub_tiles(tm)):
        is_ctx = pl.program_id(1) * tm + rows.start < ctx_len
        sh = _mod_rows(m_ref, is_ctx, 0)
        sc = _mod_rows(m_ref, is_ctx, 1)
        a = (h_ref[rows, :] * (1.0 + sc) + sh).astype(BF16)
        cos = cos_ref[rows, :]
        sin = sin_ref[rows, :]

        def rope(x):
            partner = jnp.where(first, pltpu.roll(x, LANES - 16, 1), pltpu.roll(x, 16, 1))
            return x * cos + partner * sin

        cq = rms(jnp.dot(a, w_ref[:, 0:nq], preferred_element_type=F32), gq_ref[...], nq)
        for m in range(nq // LANES):
            qc_ref[m * LANES:(m + 1) * LANES, rows] = (
                rope(cq[:, m * LANES:(m + 1) * LANES]) * (scale * LOG2E)).T.astype(BF16)
        dq = jnp.dot(a, w_ref[:, nq:2 * nq], preferred_element_type=F32)
        for m in range(nq // LANES):
            qd_ref[m * LANES:(m + 1) * LANES, rows] = (
                rope(dq[:, m * LANES:(m + 1) * LANES]) * (scale * LOG2E)).T.astype(BF16)
        kv = jnp.dot(a, w_ref[:, 2 * nq:], preferred_element_type=F32)
        kc_ref[rows, :] = rope(rms(kv[:, 0:LANES], gk_ref[...], LANES)).astype(BF16)
        kd_ref[rows, :] = rope(kv[:, 2 * LANES:3 * LANES]).astype(BF16)
        vc_ref[0, r], vc_ref[1, r] = value_rows(kv[:, LANES:2 * LANES])
        vd = value_rows(kv[:, 3 * LANES:4 * LANES])
        for jj in range(KV_HEADS):
            for c in range(sub // LANES):
                vd_ref[jj, r * (sub // LANES) + c] = vd[jj][:, c * LANES:(c + 1) * LANES]


def _in_odd(h, modv, w_in, bd, gq, gk, cos_t, sin_t, ctx_len):
    b, s, d = h.shape
    tm = _post_tile(s, 0)
    nq = Q_HEADS * HEAD_DIM
    row = lambda width: pl.BlockSpec((None, tm, width), lambda bi, i: (bi, i, 0))
    tab = pl.BlockSpec((tm, LANES), lambda bi, i: (i, 0))
    qt = pl.BlockSpec((None, nq, tm), lambda bi, i: (bi, 0, i))
    sds = lambda width: jax.ShapeDtypeStruct((b, s, width), BF16)
    return pl.pallas_call(
        functools.partial(_in_odd_kernel, ctx_len=ctx_len),
        grid=(b, s // tm),
        in_specs=[row(d),
                  pl.BlockSpec((None, 16, d), lambda bi, i: (bi, 0, 0)),
                  _const_spec(w_in.shape), _const_spec(bd.shape),
                  _const_spec(gq.shape), _const_spec(gk.shape), tab, tab],
        out_specs=[qt, qt, row(LANES),
                   pl.BlockSpec((None, KV_HEADS, tm // ATT_TK, LANES, ATT_TK), lambda bi, i: (bi, 0, i, 0, 0)),
                   row(LANES),
                   pl.BlockSpec((None, KV_HEADS, tm // LANES, LANES, LANES), lambda bi, i: (bi, 0, i, 0, 0))],
        out_shape=[jax.ShapeDtypeStruct((b, nq, s), BF16), jax.ShapeDtypeStruct((b, nq, s), BF16), sds(LANES),
                   jax.ShapeDtypeStruct((b, KV_HEADS, s // ATT_TK, LANES, ATT_TK), BF16), sds(LANES),
                   jax.ShapeDtypeStruct((b, KV_HEADS, s // LANES, LANES, LANES), BF16)],
        compiler_params=_cparams(("parallel", "parallel")),
        name="odd_in_proj",
    )(h, modv, w_in, bd, gq, gk, cos_t, sin_t)


def _stack_queries(q_ref, qs_ref, j, tq):
    for hh in range(GROUP):
        qh = q_ref[hh * HEAD_DIM:(hh + 1) * HEAD_DIM, :]
        zero = jnp.zeros_like(qh)
        qs_ref[0:HEAD_DIM, hh * tq:(hh + 1) * tq] = jnp.where(j == 0, qh, zero)
        qs_ref[HEAD_DIM:, hh * tq:(hh + 1) * tq] = jnp.where(j == 0, zero, qh)


def _store_heads(o, o_ref, tq):
    for m in range(GROUP // 2):
        pair = jnp.concatenate([o[:, (2 * m) * tq:(2 * m + 1) * tq],
                                o[:, (2 * m + 1) * tq:(2 * m + 2) * tq]], axis=0)
        o_ref[:, m * LANES:(m + 1) * LANES] = pair.T.astype(o_ref.dtype)


def _gattn_kernel(q_ref, k_ref, v_ref, o_ref, *scratch, ctx_len, first_tile):
    tq = q_ref.shape[1]
    gw = GROUP * HEAD_DIM
    _, ntile, _, tv = v_ref.shape
    chains = [scratch[6 * j:6 * j + 6] for j in range(KV_HEADS)]
    big = chains[0][4].shape[0] - 8
    nbig = (ntile * tv - ctx_len) // big
    qi = pl.program_id(1) + first_tile

    def scores(j, start, s_ref):
        if not isinstance(start, int):
            start = pl.multiple_of(start, tv)
        n = s_ref.shape[0] - 8
        sc = jnp.dot(k_ref[pl.ds(start, n), :], chains[j][0][...], preferred_element_type=F32)
        s_ref[0:n, :] = sc
        s_ref[n:n + 8, :] = jnp.broadcast_to(jnp.max(sc, axis=0, keepdims=True), (8, sc.shape[1]))

    def update(j, start, s_ref):
        m_ref, acc_ref = chains[j][1], chains[j][2]
        n = s_ref.shape[0] - 8
        m_prev = m_ref[...]
        m_new = jnp.maximum(m_prev, s_ref[n:n + 1, :])
        alpha = jnp.exp2(m_prev - m_new)
        p = jnp.exp2(s_ref[0:n, :] - m_new).astype(BF16)
        t0 = start // tv
        vt = jnp.concatenate([v_ref[j, t0 + i] for i in range(n // tv)], axis=1)
        acc_ref[...] = alpha * acc_ref[...] + jnp.dot(vt, p, preferred_element_type=F32)
        m_ref[...] = m_new

    def latent(n):
        return ctx_len + n * big

    for j, (qs_ref, m_ref, acc_ref, sc_ref, _, _) in enumerate(chains):
        _stack_queries(q_ref.at[j * gw:(j + 1) * gw, :], qs_ref, j, tq)
        m_ref[...] = jnp.full(m_ref.shape, NEG_INF, F32)
        acc_ref[...] = jnp.zeros(acc_ref.shape, F32)
        scores(j, 0, sc_ref)

    @pl.when(qi * tq >= ctx_len)
    def _():
        for j, (_, _, _, sc_ref, sa_ref, _) in enumerate(chains):
            scores(j, latent(0), sa_ref)
            update(j, 0, sc_ref)

        def pair(u, _):
            for j, (_, _, _, _, sa_ref, sb_ref) in enumerate(chains):
                scores(j, latent(2 * u + 1), sb_ref)
                update(j, latent(2 * u), sa_ref)
            for j, (_, _, _, _, sa_ref, sb_ref) in enumerate(chains):
                scores(j, latent(2 * u + 2), sa_ref)
                update(j, latent(2 * u + 1), sb_ref)
            return 0

        lax.fori_loop(0, (nbig - 1) // 2, pair, 0)
        for j, (_, _, _, _, sa_ref, sb_ref) in enumerate(chains):
            if nbig % 2 == 0:
                scores(j, latent(nbig - 1), sb_ref)
                update(j, latent(nbig - 2), sa_ref)
                update(j, latent(nbig - 1), sb_ref)
            else:
                update(j, latent(nbig - 1), sa_ref)

    @pl.when(qi * tq < ctx_len)
    def _():
        for j in range(KV_HEADS):
            update(j, 0, chains[j][3])

    for j in range(KV_HEADS):
        acc_ref = chains[j][2]
        _store_heads(acc_ref[0:HEAD_DIM, :] / acc_ref[HEAD_DIM:HEAD_DIM + 1, :],
                     o_ref.at[:, j * gw:(j + 1) * gw], tq)


def _global_attention(qt, k, vt, ctx_len, first_tile):
    b, nq, s = qt.shape
    tq = ATT_TQ
    assert vt.shape[4] == ATT_TK and ctx_len % ATT_TK == 0 and (s - ctx_len) % ATT_BIG == 0
    cols = GROUP * tq
    per_head = [pltpu.VMEM((LANES, cols), BF16), pltpu.VMEM((1, cols), F32), pltpu.VMEM((LANES, cols), F32),
                pltpu.VMEM((ctx_len + 8, cols), F32), pltpu.VMEM((ATT_BIG + 8, cols), F32),
                pltpu.VMEM((ATT_BIG + 8, cols), F32)]
    return pl.pallas_call(
        functools.partial(_gattn_kernel, ctx_len=ctx_len, first_tile=first_tile),
        grid=(b, s // tq - first_tile),
        in_specs=[pl.BlockSpec((None, nq, tq), lambda bi, i: (bi, 0, i + first_tile)),
                  pl.BlockSpec((None, s, LANES), lambda bi, i: (bi, 0, 0)),
                  pl.BlockSpec((None,) + vt.shape[1:], lambda bi, i: (bi, 0, 0, 0, 0))],
        out_specs=pl.BlockSpec((None, tq, nq), lambda bi, i: (bi, i, 0)),
        out_shape=jax.ShapeDtypeStruct((b, s - first_tile * tq, nq), BF16),
        scratch_shapes=per_head * KV_HEADS,
        compiler_params=_cparams(("parallel", "arbitrary")),
        name="global_attention",
    )(qt, k, vt)


def _wattn_kernel(q_ref, k_ref, v_ref, sink_ref, bias_ref, o_ref, *scratch, ctx_len, span, first_tile):
    tq = q_ref.shape[1]
    s = k_ref.shape[0]
    gw = GROUP * HEAD_DIM
    qi = pl.program_id(1) + first_tile
    is_ctx = qi * tq < ctx_len

    def scores(j, window_start):
        qs_ref, sc_ref, sw_ref = scratch[3 * j:3 * j + 3]
        _stack_queries(q_ref.at[j * gw:(j + 1) * gw, :], qs_ref, j, tq)
        qs = qs_ref[...]
        sc_ref[...] = jnp.dot(k_ref[0:ctx_len, :], qs, preferred_element_type=F32)
        if window_start is not None:
            sw_ref[...] = (jnp.dot(k_ref[pl.ds(window_start, span), :], qs, preferred_element_type=F32)
                           + bias_ref[...])

    def finish(j, window_start):
        _, sc_ref, sw_ref = scratch[3 * j:3 * j + 3]
        sink = sink_ref[j] * LOG2E
        v_ctx = jnp.concatenate([v_ref[j, t] for t in range(ctx_len // LANES)], axis=1)
        m = jnp.maximum(jnp.max(sc_ref[...], axis=0, keepdims=True), sink)
        if window_start is not None:
            m = jnp.maximum(m, jnp.max(sw_ref[...], axis=0, keepdims=True))
        acc = jnp.dot(v_ctx, jnp.exp2(sc_ref[...] - m).astype(BF16), preferred_element_type=F32)
        if window_start is not None:
            t0 = window_start // LANES
            v_win = jnp.concatenate([v_ref[j, t0 + i] for i in range(span // LANES)], axis=1)
            acc = acc + jnp.dot(v_win, jnp.exp2(sw_ref[...] - m).astype(BF16), preferred_element_type=F32)
        den = acc[HEAD_DIM:HEAD_DIM + 1, :] + jnp.exp2(sink - m)
        _store_heads(acc[0:HEAD_DIM, :] / den, o_ref.at[:, j * gw:(j + 1) * gw], tq)

    def both(window_start):
        scores(0, window_start)
        scores(1, window_start)
        finish(0, window_start)
        finish(1, window_start)

    @pl.when(is_ctx)
    def _():
        both(None)

    @pl.when(jnp.logical_not(is_ctx))
    def _():
        both(pl.multiple_of(jnp.clip(qi * tq - WINDOW, ctx_len, s - span), LANES))


def _window_attention(qt, k, vt, sink, ctx_len, first_tile):
    b, nq, s = qt.shape
    tq = ATT_TQ
    gw = GROUP * HEAD_DIM
    span = tq + 2 * WINDOW
    assert tq >= WINDOW and (s - ctx_len) // tq >= 2
    sink_row = jnp.repeat(sink.reshape(KV_HEADS, 1, GROUP), tq, axis=2)
    r = jnp.arange(span)[:, None]
    qcol = (jnp.arange(GROUP * tq) % tq)[None, :]
    bias = jnp.stack([jnp.where(jnp.abs(off + r - qcol) <= WINDOW, 0.0, NEG_INF)
                      for off in (0, -WINDOW, tq - span)]).astype(F32)

    def placement(i):
        lo = (i + first_tile) * tq - WINDOW
        return jnp.where(lo < ctx_len, 0, jnp.where(lo > s - span, 2, 1))

    return pl.pallas_call(
        functools.partial(_wattn_kernel, ctx_len=ctx_len, span=span, first_tile=first_tile),
        grid=(b, s // tq - first_tile),
        in_specs=[pl.BlockSpec((None, nq, tq), lambda bi, i: (bi, 0, i + first_tile)),
                  pl.BlockSpec((None, s, LANES), lambda bi, i: (bi, 0, 0)),
                  pl.BlockSpec((None,) + vt.shape[1:], lambda bi, i: (bi, 0, 0, 0, 0)),
                  pl.BlockSpec((KV_HEADS, 1, GROUP * tq), lambda bi, i: (0, 0, 0)),
                  pl.BlockSpec((None, span, GROUP * tq), lambda bi, i: (placement(i), 0, 0))],
        out_specs=pl.BlockSpec((None, tq, nq), lambda bi, i: (bi, i, 0)),
        out_shape=jax.ShapeDtypeStruct((b, s - first_tile * tq, nq), BF16),
        scratch_shapes=[pltpu.VMEM((LANES, GROUP * tq), BF16), pltpu.VMEM((ctx_len, GROUP * tq), F32),
                        pltpu.VMEM((span, GROUP * tq), F32)] * KV_HEADS,
        compiler_params=_cparams(("parallel", "parallel")),
        name="window_attention",
    )(qt, k, vt, sink_row, bias)


def _rope_tables(n_lat, ctx_len):
    rows = n_lat // GRID_W
    row = jnp.repeat(jnp.arange(rows, dtype=F32), GRID_W)
    col = jnp.tile(jnp.arange(GRID_W, dtype=F32), rows)
    nf = HEAD_DIM // 4
    inv_freq = ROPE_THETA ** (-jnp.arange(nf, dtype=F32) / nf)
    ang_r = row[:, None] * inv_freq
    ang_c = col[:, None] * inv_freq
    cos = jnp.concatenate([jnp.cos(ang_r)] * 2 + [jnp.cos(ang_c)] * 2, axis=1)
    sin = jnp.concatenate([-jnp.sin(ang_r), jnp.sin(ang_r), -jnp.sin(ang_c), jnp.sin(ang_c)], axis=1)
    cos = jnp.concatenate([jnp.ones((ctx_len, HEAD_DIM), F32), cos], axis=0)
    sin = jnp.concatenate([jnp.zeros((ctx_len, HEAD_DIM), F32), sin], axis=0)
    return jnp.tile(cos, (1, LANES // HEAD_DIM)), jnp.tile(sin, (1, LANES // HEAD_DIM))


def kernel(x, c, ctx, c_ctx, ada_w, ada_b, ln1_g, ln1_b, ln2_g, ln2_b, ffn_w_in, ffn_w_out,
           ev_w_in, ev_w_out, rg_conv_w, rg_conv_b, rg_gate_w, rg_gate_b, rg_lambda, cm_w_s, cm_b_s,
           od_w_in, od_w_out, qn_g, kn_g, sink):
    b, t, d = x.shape
    ctx_len = ctx.shape[1]
    s = ctx_len + t
    depth = ada_w.shape[0]
    assert d == D_MODEL and depth == DEPTH and b + 1 <= 16
    assert ctx_len % ROW_TILE == 0 and t % ROW_TILE == 0 and t % GRID_W == 0 and ATT_TQ == ATT_TK == ROW_TILE

    c_rows = jnp.zeros((16, d), F32).at[:b].set(c).at[b].set(c_ctx)
    mods = _ada_vectors(c_rows, ada_w, ada_b).reshape(depth, 16, 6, d)
    modv = jnp.zeros((depth, b, 16, d), F32)
    modv = modv.at[:, :, 0:6].set(mods[:, :b])
    modv = modv.at[:, :, 8:14].set(jnp.broadcast_to(mods[:, b][:, None], (depth, b, 6, d)))

    cos_t, sin_t = _rope_tables(t, ctx_len)
    nq = Q_HEADS * HEAD_DIM
    bd = jnp.kron(jnp.eye(nq // HEAD_DIM, dtype=F32), jnp.full((HEAD_DIM, HEAD_DIM), 1.0 / HEAD_DIM, F32)).astype(BF16)
    assert sum(FFN_CHUNKS) == FFN_HIDDEN

    h = jnp.concatenate([ctx, x], axis=1)
    rows_ctx = ctx_len
    for l in range(depth):
        j = l // 2
        mv = modv[l]
        norms = (ln1_g[l].reshape(1, d), ln1_b[l].reshape(1, d), ffn_w_in[l].astype(BF16),
                 ffn_w_out[l].astype(BF16), ln2_g[l].reshape(1, d), ln2_b[l].reshape(1, d))
        if l % 2 == 0:
            gg, xr, gu, vn = _in_even(h, mv, ev_w_in[j].astype(BF16), ctx_len)
            gw = jnp.transpose(rg_gate_w[j], (2, 0, 3, 1, 4)).reshape(RNN_HEADS, 2, LANES, 2 * LANES).astype(BF16)
            gb = jnp.transpose(rg_gate_b[j].reshape(2, 2, RNN_HEADS, LANES), (2, 0, 1, 3)).reshape(RNN_HEADS, 2, 1, 2 * LANES)
            mr = _rglru(xr, gg, rg_conv_w[j], rg_conv_b[j].reshape(1, -1), gw, gb, rg_lambda[j], ctx_len)
            bsb = jnp.broadcast_to(cm_b_s[j][:, :, None], (CMLP_GROUPS, CHUNK, CMLP_WIDTH // CMLP_GROUPS))
            h = _post_even(h, mv, mr, gu, vn, cm_w_s[j].astype(BF16), bsb, ev_w_out[j].astype(BF16), *norms, ctx_len)
        else:
            gq = jnp.tile(qn_g[j], Q_HEADS).reshape(1, nq)
            gk = jnp.tile(kn_g[j], LANES // HEAD_DIM).reshape(1, LANES)
            qc, qd, kc, vc, kd, vd = _in_odd(h, mv, od_w_in[j].astype(BF16), bd, gq, gk, cos_t, sin_t, ctx_len)
            skip = ctx_len // ATT_TQ if l == depth - 1 else 0
            yc = _global_attention(qc, kc, vc, ctx_len, skip)
            yd = _window_attention(qd, kd, vd, sink[j], ctx_len, skip)
            h = _post_odd(h, mv, yc, yd, od_w_out[j].astype(BF16), *norms, ctx_len, skip)
            if skip:
                rows_ctx = 0
    return h[:, rows_ctx:, :]
```

```python
import functools

import jax
import jax.numpy as jnp
from jax import lax
from jax.experimental import pallas as pl
from jax.experimental.pallas import tpu as pltpu

F32 = jnp.float32
BF16 = jnp.bfloat16

D_MODEL = 1024
DEPTH = 4
GRID_W = 64
RNN_WIDTH = D_MODEL
RNN_HEADS = RNN_WIDTH // 128
CONV_W = 4
LRU_C = 8.0
CMLP_WIDTH = D_MODEL // 2
CMLP_GROUPS = 4
CHUNK = 128
HEAD_DIM = 64
Q_HEADS = 8
KV_HEADS = 2
GROUP = Q_HEADS // KV_HEADS
WINDOW = 128
ROPE_THETA = 10000.0
NEG_INF = -1e30
FFN_HIDDEN = 2816
FFN_CHUNKS = (1024, 1024, 768)
ALPHA = (2.0 * DEPTH) ** 0.25
LOG2E = 1.4426950408889634
EPS = 1e-6

LANES = 128
ROW_TILE = 256
SCAN_BLOCK = 128
SCAN_RUN = 4
ATT_TQ = 256
ATT_TK = 256
ATT_BIG = 1024
VMEM_LIMIT = 56 * 1024 * 1024


def _cparams(sem):
    return pltpu.CompilerParams(dimension_semantics=sem, vmem_limit_bytes=VMEM_LIMIT)


def _const_spec(shape):
    nd = len(shape)
    return pl.BlockSpec(shape, lambda *_: (0,) * nd, pipeline_mode=pl.Buffered(1))


def _gelu(x):
    return 0.5 * x * (1.0 + jnp.tanh(0.7978845608028654 * (x + 0.044715 * (x * x * x))))


def _normalise(x):
    mu = jnp.mean(x, axis=-1, keepdims=True)
    xc = x - mu
    var = jnp.mean(xc * xc, axis=-1, keepdims=True)
    return xc * lax.rsqrt(var + EPS)


def _mod_rows(m_ref, is_ctx, lat_row):
    return jnp.where(is_ctx, m_ref[lat_row + 8:lat_row + 9, :], m_ref[lat_row:lat_row + 1, :])


def _mod_kernel(c_ref, w_ref, b_ref, o_ref):
    c = c_ref[...]
    s = c * jax.nn.sigmoid(c)
    o_ref[...] = jnp.dot(s, w_ref[...], preferred_element_type=F32,
                         precision=lax.Precision.HIGHEST) + b_ref[...]


def _ada_vectors(c_rows, ada_w, ada_b):
    depth, d, n = ada_w.shape
    rows = c_rows.shape[0]
    nb = 1536
    return pl.pallas_call(
        _mod_kernel,
        grid=(depth, n // nb),
        in_specs=[pl.BlockSpec((rows, d), lambda l, j: (0, 0)),
                  pl.BlockSpec((None, d, nb), lambda l, j: (l, 0, j)),
                  pl.BlockSpec((None, 1, nb), lambda l, j: (l, 0, j))],
        out_specs=pl.BlockSpec((None, rows, nb), lambda l, j: (l, 0, j)),
        out_shape=jax.ShapeDtypeStruct((depth, rows, n), F32),
        compiler_params=_cparams(("parallel", "parallel")),
        name="ada_vectors",
    )(c_rows, ada_w, ada_b.reshape(depth, 1, n))


def _in_even_kernel(h_ref, m_ref, w_ref, gg_ref, xr_ref, gu_ref, vn_ref, *, ctx_len):
    tm = h_ref.shape[0]
    w = RNN_WIDTH
    for rows in _sub_tiles(tm):
        is_ctx = pl.program_id(1) * tm + rows.start < ctx_len
        sh = _mod_rows(m_ref, is_ctx, 0)
        sc = _mod_rows(m_ref, is_ctx, 1)
        a = (h_ref[rows, :] * (1.0 + sc) + sh).astype(BF16)
        gate = jnp.dot(a, w_ref[:, 0:w], preferred_element_type=F32)
        gg_ref[rows, :] = _gelu(gate).astype(BF16)
        xr_ref[rows, :] = jnp.dot(a, w_ref[:, w:2 * w], preferred_element_type=F32)
        u = jnp.dot(a, w_ref[:, 2 * w:2 * w + CMLP_WIDTH], preferred_element_type=F32)
        gu_ref[rows, :] = _gelu(u).astype(BF16)
        v = jnp.dot(a, w_ref[:, 2 * w + CMLP_WIDTH:], preferred_element_type=F32)
        vn_ref[rows, :] = _normalise(_gelu(v)).astype(BF16)


def _in_even(h, modv, w_in, ctx_len):
    b, s, d = h.shape
    tm = _post_tile(s, 0)
    n_in = w_in.shape[1]
    row = lambda width: pl.BlockSpec((None, tm, width), lambda bi, i: (bi, i, 0))
    return pl.pallas_call(
        functools.partial(_in_even_kernel, ctx_len=ctx_len),
        grid=(b, s // tm),
        in_specs=[row(d),
                  pl.BlockSpec((None, 16, d), lambda bi, i: (bi, 0, 0)),
                  _const_spec((d, n_in))],
        out_specs=[row(RNN_WIDTH), row(RNN_WIDTH), row(CMLP_WIDTH), row(CMLP_WIDTH)],
        out_shape=[jax.ShapeDtypeStruct((b, s, RNN_WIDTH), BF16),
                   jax.ShapeDtypeStruct((b, s, RNN_WIDTH), F32),
                   jax.ShapeDtypeStruct((b, s, CMLP_WIDTH), BF16),
                   jax.ShapeDtypeStruct((b, s, CMLP_WIDTH), BF16)],
        compiler_params=_cparams(("parallel", "parallel")),
        name="even_in_proj",
    )(h, modv, w_in)


def _block_scan(a_ref, b_ref, o_ref, t0, carry, reverse):
    n = SCAN_RUN
    sub = lax.broadcasted_iota(jnp.int32, (8, a_ref.shape[1]), 0)
    ngroup = a_ref.shape[0] // (8 * n)
    steps = range(n - 1, -1, -1) if reverse else range(n)
    for k in (range(ngroup - 1, -1, -1) if reverse else range(ngroup)):
        base = 8 * n * k
        av = [a_ref[pl.ds(base + i, 8, stride=n), :] for i in range(n)]
        bv = [b_ref[pl.ds(base + i, 8, stride=n), :] for i in range(n)]
        hloc, aloc = [None] * n, [None] * n
        prev = None
        for i in steps:
            hloc[i] = bv[i] if prev is None else av[i] * hloc[prev] + bv[i]
            aloc[i] = av[i] if prev is None else av[i] * aloc[prev]
            prev = i
        atot, htot = aloc[prev], hloc[prev]
        for d in (1, 2, 4):
            shift, ok = (8 - d, sub < 8 - d) if reverse else (d, sub >= d)
            htot = jnp.where(ok, htot + atot * pltpu.roll(htot, shift, 0), htot)
            atot = jnp.where(ok, atot * pltpu.roll(atot, shift, 0), atot)
        shift, ok = (7, sub < 7) if reverse else (1, sub >= 1)
        enter = (jnp.where(ok, pltpu.roll(htot, shift, 0), 0.0)
                 + jnp.where(ok, pltpu.roll(atot, shift, 0), 1.0) * carry)
        for i in range(n):
            o_ref[pl.ds(t0 + base + i, 8, stride=n), :] = hloc[i] + aloc[i] * enter
        last = 0 if reverse else 7
        carry = (jnp.broadcast_to(htot[last:last + 1, :], htot.shape)
                 + jnp.broadcast_to(atot[last:last + 1, :], atot.shape) * carry)
    return carry


def _scan_kernel(xr_ref, gg_ref, cw_ref, cb_ref, gw_ref, gb_ref, lam_ref, out_ref, xc_ref, rf_ref, rr_ref,
                 ab0_ref, ab1_ref, g0_ref, g1_ref, *, ctx_len, tb):
    s = xr_ref.shape[0]
    nblk = s // tb
    ncb = ctx_len // tb
    cw = cw_ref[...]
    cb = cb_ref[...]
    row = lax.broadcasted_iota(jnp.int32, (tb, LANES), 0)

    def conv(blk, _):
        t0 = pl.multiple_of(blk * tb, tb)
        x = xr_ref[pl.ds(t0, tb), :]
        prev = xr_ref[pl.ds(pl.multiple_of(jnp.maximum(t0 - 8, 0), 8), 8), :]
        nxt = xr_ref[pl.ds(pl.multiple_of(jnp.minimum(t0 + tb, s - 8), 8), 8), :]
        pf = jnp.where((blk == 0) | (blk == ncb), 0.0, 1.0)
        nf = jnp.where((blk == ncb - 1) | (blk == nblk - 1), 0.0, 1.0)
        p6 = prev[6:7, :] * pf
        p7 = prev[7:8, :] * pf
        n0 = nxt[0:1, :] * nf
        xm1 = jnp.where(row == 0, p7, pltpu.roll(x, 1, 0))
        xm2 = jnp.where(row == 0, p6, jnp.where(row == 1, p7, pltpu.roll(x, 2, 0)))
        xp1 = jnp.where(row == tb - 1, n0, pltpu.roll(x, tb - 1, 0))
        xc_ref[pl.ds(t0, tb), :] = (xm2 * cw[0:1, :] + xm1 * cw[1:2, :] + x * cw[2:3, :] + xp1 * cw[3:4, :]
                                    + cb)
        return 0

    lax.fori_loop(0, nblk, conv, 0)

    def block_start(step, d):
        blk = step if d == 0 else jnp.where(step < ncb, ncb - 1 - step, nblk - 1 - (step - ncb))
        return pl.multiple_of(blk * tb, tb)

    def gate_matmuls(step, g_ref):
        step = jnp.minimum(step, nblk - 1)
        for d in range(2):
            xc = xc_ref[pl.ds(block_start(step, d), tb), :]
            g_ref[d] = jnp.dot(xc.astype(BF16), gw_ref[d], preferred_element_type=F32)

    def stash_coeffs(step, g_ref, ab_ref):
        step = jnp.minimum(step, nblk - 1)
        for d in range(2):
            xc = xc_ref[pl.ds(block_start(step, d), tb), :]
            g = g_ref[d] + gb_ref[d]
            r = jax.nn.sigmoid(g[:, :LANES])
            ig = jax.nn.sigmoid(g[:, LANES:])
            z = -lam_ref[d:d + 1, :]
            softplus = jnp.maximum(z, 0.0) + jnp.log(1.0 + jnp.exp(-jnp.abs(z)))
            a = jnp.exp2(r * ((-LRU_C * LOG2E) * softplus))
            ab_ref[2 * d] = a
            ab_ref[2 * d + 1] = jnp.sqrt(1.0 - a * a) * (ig * xc)

    def scan(step, ab_ref, cf, cr):
        cf = _block_scan(ab_ref.at[0], ab_ref.at[1], rf_ref, block_start(step, 0), cf, False)
        cr = _block_scan(ab_ref.at[2], ab_ref.at[3], rr_ref, block_start(step, 1), cr, True)
        return cf, cr

    gate_matmuls(0, g0_ref)
    stash_coeffs(0, g0_ref, ab0_ref)
    gate_matmuls(1, g1_ref)
    gate_matmuls(2, g0_ref)

    def two_steps(u, carry):
        cf, cr = carry
        cf, cr = scan(2 * u, ab0_ref, cf, cr)
        stash_coeffs(2 * u + 1, g1_ref, ab1_ref)
        cf, cr = scan(2 * u + 1, ab1_ref, cf, cr)
        stash_coeffs(2 * u + 2, g0_ref, ab0_ref)
        gate_matmuls(2 * u + 3, g1_ref)
        gate_matmuls(2 * u + 4, g0_ref)
        return cf, cr

    zero = jnp.zeros((8, LANES), F32)
    lax.fori_loop(0, nblk // 2, two_steps, (zero, zero))

    def combine(j, _):
        t0 = pl.multiple_of(j * tb, tb)
        rec = rf_ref[pl.ds(t0, tb), :] + rr_ref[pl.ds(t0, tb), :]
        out_ref[pl.ds(t0, tb), :] = (gg_ref[pl.ds(t0, tb), :].astype(F32) * rec).astype(BF16)
        return 0

    lax.fori_loop(0, nblk, combine, 0)


def _rglru(xr, gg, conv_w, conv_b, gate_w, gate_b, lam, ctx_len):
    b, s, w = xr.shape
    nh = w // LANES
    assert (s // SCAN_BLOCK) % 2 == 0 and ctx_len % SCAN_BLOCK == 0
    col = lambda dt: pl.BlockSpec((None, s, LANES), lambda bi, hd: (bi, 0, hd))
    return pl.pallas_call(
        functools.partial(_scan_kernel, ctx_len=ctx_len, tb=SCAN_BLOCK),
        grid=(b, nh),
        in_specs=[col(F32), col(BF16),
                  pl.BlockSpec((CONV_W, LANES), lambda bi, hd: (0, hd)),
                  pl.BlockSpec((1, LANES), lambda bi, hd: (0, hd)),
                  pl.BlockSpec((None, 2, LANES, 2 * LANES), lambda bi, hd: (hd, 0, 0, 0)),
                  pl.BlockSpec((None, 2, 1, 2 * LANES), lambda bi, hd: (hd, 0, 0, 0)),
                  pl.BlockSpec((2, LANES), lambda bi, hd: (0, hd))],
        out_specs=col(BF16),
        out_shape=jax.ShapeDtypeStruct((b, s, w), BF16),
        scratch_shapes=[pltpu.VMEM((s, LANES), F32), pltpu.VMEM((s, LANES), F32), pltpu.VMEM((s, LANES), F32),
                        pltpu.VMEM((4, SCAN_BLOCK, LANES), F32), pltpu.VMEM((4, SCAN_BLOCK, LANES), F32),
                        pltpu.VMEM((2, SCAN_BLOCK, 2 * LANES), F32), pltpu.VMEM((2, SCAN_BLOCK, 2 * LANES), F32)],
        compiler_params=_cparams(("parallel", "parallel")),
        name="rglru_scan",
    )(xr, gg, conv_w, conv_b, gate_w, gate_b, lam)


def _residual_ln(h, y, gate, g, b):
    return _normalise(ALPHA * h + gate * y) * g + b


def _swiglu(a, wi_ref, wo_ref):
    acc = None
    c0 = 0
    for width in FFN_CHUNKS:
        zg = jnp.dot(a, wi_ref[:, c0:c0 + width], preferred_element_type=F32)
        zu = jnp.dot(a, wi_ref[:, FFN_HIDDEN + c0:FFN_HIDDEN + c0 + width], preferred_element_type=F32)
        hm = (zg * jax.nn.sigmoid(zg) * zu).astype(BF16)
        y = jnp.dot(hm, wo_ref[c0:c0 + width, :], preferred_element_type=F32)
        acc = y if acc is None else acc + y
        c0 += width
    return acc


def _residual_pair(h, y_mix, m_ref, is_ctx, l1g_ref, l1b_ref, wi_ref, wo_ref, l2g_ref, l2b_ref):
    h1 = _residual_ln(h, y_mix, _mod_rows(m_ref, is_ctx, 2), l1g_ref[...], l1b_ref[...])
    a = (h1 * (1.0 + _mod_rows(m_ref, is_ctx, 4)) + _mod_rows(m_ref, is_ctx, 3)).astype(BF16)
    return _residual_ln(h1, _swiglu(a, wi_ref, wo_ref), _mod_rows(m_ref, is_ctx, 5), l2g_ref[...], l2b_ref[...])


def _sub_tiles(tm):
    return [slice(r * ROW_TILE, (r + 1) * ROW_TILE) for r in range(tm // ROW_TILE)]


def _post_tile(n_rows, first_tile):
    return 3 * ROW_TILE if first_tile == 0 and n_rows % (3 * ROW_TILE) == 0 else ROW_TILE


def _post_even_kernel(h_ref, m_ref, mr_ref, gu_ref, vn_ref, ws_ref, bs_ref, wm_ref, l1g_ref, l1b_ref,
                      wi_ref, wo_ref, l2g_ref, l2b_ref, o_ref, gm_ref, *, ctx_len):
    tm = h_ref.shape[0]
    gw = CMLP_WIDTH // CMLP_GROUPS
    for rows in _sub_tiles(tm):
        is_ctx = pl.program_id(1) * tm + rows.start < ctx_len
        for c in range(rows.start, rows.stop, CHUNK):
            crow = slice(c, c + CHUNK)
            for g in range(CMLP_GROUPS):
                cols = slice(g * gw, (g + 1) * gw)
                mixed = jnp.dot(ws_ref[g], vn_ref[crow, cols], preferred_element_type=F32) + bs_ref[g]
                gm_ref[crow, cols] = (gu_ref[crow, cols].astype(F32) * mixed).astype(BF16)
        y = jnp.dot(mr_ref[rows, :], wm_ref[0:RNN_WIDTH, :], preferred_element_type=F32)
        y = y + jnp.dot(gm_ref[rows, :], wm_ref[RNN_WIDTH:, :], preferred_element_type=F32)
        o_ref[rows, :] = _residual_pair(h_ref[rows, :], y, m_ref, is_ctx, l1g_ref, l1b_ref,
                                        wi_ref, wo_ref, l2g_ref, l2b_ref)


def _post_even(h, modv, mr, gu, vn, ws, bsb, w_mix, l1g, l1b, w_in, w_out, l2g, l2b, ctx_len):
    b, s, d = h.shape
    tm = _post_tile(s, 0)
    row = lambda width: pl.BlockSpec((None, tm, width), lambda bi, i: (bi, i, 0))
    vec = _const_spec((1, d))
    return pl.pallas_call(
        functools.partial(_post_even_kernel, ctx_len=ctx_len),
        grid=(b, s // tm),
        in_specs=[row(d),
                  pl.BlockSpec((None, 16, d), lambda bi, i: (bi, 0, 0)),
                  row(RNN_WIDTH), row(CMLP_WIDTH), row(CMLP_WIDTH),
                  _const_spec(ws.shape), _const_spec(bsb.shape), _const_spec(w_mix.shape), vec, vec,
                  _const_spec(w_in.shape), _const_spec(w_out.shape), vec, vec],
        out_specs=row(d),
        out_shape=jax.ShapeDtypeStruct((b, s, d), F32),
        scratch_shapes=[pltpu.VMEM((tm, CMLP_WIDTH), BF16)],
        compiler_params=_cparams(("parallel", "parallel")),
        name="even_out_ffn",
    )(h, modv, mr, gu, vn, ws, bsb, w_mix, l1g, l1b, w_in, w_out, l2g, l2b)


def _post_odd_kernel(h_ref, m_ref, yc_ref, yd_ref, wm_ref, l1g_ref, l1b_ref, wi_ref, wo_ref, l2g_ref, l2b_ref,
                     o_ref, *, ctx_len, first_tile):
    tm = h_ref.shape[0]
    half = yc_ref.shape[1]
    for rows in _sub_tiles(tm):
        is_ctx = pl.program_id(1) * tm + first_tile * ROW_TILE + rows.start < ctx_len
        y = jnp.dot(yc_ref[rows, :], wm_ref[0:half, :], preferred_element_type=F32)
        y = y + jnp.dot(yd_ref[rows, :], wm_ref[half:, :], preferred_element_type=F32)
        o_ref[rows, :] = _residual_pair(h_ref[rows, :], y, m_ref, is_ctx, l1g_ref, l1b_ref,
                                        wi_ref, wo_ref, l2g_ref, l2b_ref)


def _post_odd(h, modv, yc, yd, w_mix, l1g, l1b, w_in, w_out, l2g, l2b, ctx_len, first_tile):
    b, s, d = h.shape
    tm = _post_tile(s, first_tile)
    n = (s - first_tile * ROW_TILE) // tm
    assert yc.shape[1] == n * tm and yd.shape[1] == n * tm
    row = lambda width: pl.BlockSpec((None, tm, width), lambda bi, i: (bi, i, 0))
    vec = _const_spec((1, d))
    return pl.pallas_call(
        functools.partial(_post_odd_kernel, ctx_len=ctx_len, first_tile=first_tile),
        grid=(b, n),
        in_specs=[pl.BlockSpec((None, tm, d), lambda bi, i: (bi, i + first_tile, 0)),
                  pl.BlockSpec((None, 16, d), lambda bi, i: (bi, 0, 0)),
                  row(yc.shape[2]), row(yd.shape[2]),
                  _const_spec(w_mix.shape), vec, vec,
                  _const_spec(w_in.shape), _const_spec(w_out.shape), vec, vec],
        out_specs=row(d),
        out_shape=jax.ShapeDtypeStruct((b, n * tm, d), F32),
        compiler_params=_cparams(("parallel", "parallel")),
        name="odd_out_ffn",
    )(h, modv, yc, yd, w_mix, l1g, l1b, w_in, w_out, l2g, l2b)


def _in_odd_kernel(h_ref, m_ref, w_ref, bd_ref, gq_ref, gk_ref, cos_ref, sin_ref,
                   qc_ref, qd_ref, kc_ref, vc_ref, kd_ref, vd_ref, *, ctx_len):
    tm = h_ref.shape[0]
    sub = ROW_TILE
    lane = lax.broadcasted_iota(jnp.int32, (sub, LANES), 1)
    first = (lane & 31) < 16
    rowi = lax.broadcasted_iota(jnp.int32, (LANES, sub), 0)
    pad = jnp.where(rowi == HEAD_DIM, 1.0, 0.0)
    scale = HEAD_DIM ** -0.5
    nq = Q_HEADS * HEAD_DIM

    def rms(x, gain, width):
        ms = jnp.dot((x * x).astype(BF16), bd_ref[0:width, 0:width], preferred_element_type=F32)
        return x * lax.rsqrt(ms + EPS) * gain

    def value_rows(v):
        vt = v.T
        return (jnp.where(rowi < HEAD_DIM, vt, pad).astype(BF16),
                jnp.where(rowi < HEAD_DIM, jnp.concatenate([vt[HEAD_DIM:], vt[:HEAD_DIM]], axis=0), pad).astype(BF16))

    for r, rows in enumerate(_sub_tiles(tm)):
        is_ctx = pl.program_id(1) * tm + rows.start < ctx_len
        sh = _mod_rows(m_ref, is_ctx, 0)
        sc = _mod_rows(m_ref, is_ctx, 1)
        a = (h_ref[rows, :] * (1.0 + sc) + sh).astype(BF16)
        cos = cos_ref[rows, :]
        sin = sin_ref[rows, :]

        def rope(x):
            partner = jnp.where(first, pltpu.roll(x, LANES - 16, 1), pltpu.roll(x, 16, 1))
            return x * cos + partner * sin

        cq = rms(jnp.dot(a, w_ref[:, 0:nq], preferred_element_type=F32), gq_ref[...], nq)
        for m in range(nq // LANES):
            qc_ref[m * LANES:(m + 1) * LANES, rows] = (
                rope(cq[:, m * LANES:(m + 1) * LANES]) * (scale * LOG2E)).T.astype(BF16)
        dq = jnp.dot(a, w_ref[:, nq:2 * nq], preferred_element_type=F32)
        for m in range(nq // LANES):
            qd_ref[m * LANES:(m + 1) * LANES, rows] = (
                rope(dq[:, m * LANES:(m + 1) * LANES]) * (scale * LOG2E)).T.astype(BF16)
        kv = jnp.dot(a, w_ref[:, 2 * nq:], preferred_element_type=F32)
        kc_ref[rows, :] = rope(rms(kv[:, 0:LANES], gk_ref[...], LANES)).astype(BF16)
        kd_ref[rows, :] = rope(kv[:, 2 * LANES:3 * LANES]).astype(BF16)
        vc_ref[0, r], vc_ref[1, r] = value_rows(kv[:, LANES:2 * LANES])
        vd = value_rows(kv[:, 3 * LANES:4 * LANES])
        for jj in range(KV_HEADS):
            for c in range(sub // LANES):
                vd_ref[jj, r * (sub // LANES) + c] = vd[jj][:, c * LANES:(c + 1) * LANES]


def _in_odd(h, modv, w_in, bd, gq, gk, cos_t, sin_t, ctx_len):
    b, s, d = h.shape
    tm = _post_tile(s, 0)
    nq = Q_HEADS * HEAD_DIM
    row = lambda width: pl.BlockSpec((None, tm, width), lambda bi, i: (bi, i, 0))
    tab = pl.BlockSpec((tm, LANES), lambda bi, i: (i, 0))
    qt = pl.BlockSpec((None, nq, tm), lambda bi, i: (bi, 0, i))
    sds = lambda width: jax.ShapeDtypeStruct((b, s, width), BF16)
    return pl.pallas_call(
        functools.partial(_in_odd_kernel, ctx_len=ctx_len),
        grid=(b, s // tm),
        in_specs=[row(d),
                  pl.BlockSpec((None, 16, d), lambda bi, i: (bi, 0, 0)),
                  _const_spec(w_in.shape), _const_spec(bd.shape),
                  _const_spec(gq.shape), _const_spec(gk.shape), tab, tab],
        out_specs=[qt, qt, row(LANES),
                   pl.BlockSpec((None, KV_HEADS, tm // ATT_TK, LANES, ATT_TK), lambda bi, i: (bi, 0, i, 0, 0)),
                   row(LANES),
                   pl.BlockSpec((None, KV_HEADS, tm // LANES, LANES, LANES), lambda bi, i: (bi, 0, i, 0, 0))],
        out_shape=[jax.ShapeDtypeStruct((b, nq, s), BF16), jax.ShapeDtypeStruct((b, nq, s), BF16), sds(LANES),
                   jax.ShapeDtypeStruct((b, KV_HEADS, s // ATT_TK, LANES, ATT_TK), BF16), sds(LANES),
                   jax.ShapeDtypeStruct((b, KV_HEADS, s // LANES, LANES, LANES), BF16)],
        compiler_params=_cparams(("parallel", "parallel")),
        name="odd_in_proj",
    )(h, modv, w_in, bd, gq, gk, cos_t, sin_t)


def _stack_queries(q_ref, qs_ref, j, tq):
    for hh in range(GROUP):
        qh = q_ref[hh * HEAD_DIM:(hh + 1) * HEAD_DIM, :]
        zero = jnp.zeros_like(qh)
        qs_ref[0:HEAD_DIM, hh * tq:(hh + 1) * tq] = jnp.where(j == 0, qh, zero)
        qs_ref[HEAD_DIM:, hh * tq:(hh + 1) * tq] = jnp.where(j == 0, zero, qh)


def _store_heads(o, o_ref, tq):
    for m in range(GROUP // 2):
        pair = jnp.concatenate([o[:, (2 * m) * tq:(2 * m + 1) * tq],
                                o[:, (2 * m + 1) * tq:(2 * m + 2) * tq]], axis=0)
        o_ref[:, m * LANES:(m + 1) * LANES] = pair.T.astype(o_ref.dtype)


def _gattn_kernel(q_ref, k_ref, v_ref, o_ref, *scratch, ctx_len, first_tile):
    tq = q_ref.shape[1]
    gw = GROUP * HEAD_DIM
    _, ntile, _, tv = v_ref.shape
    chains = [scratch[6 * j:6 * j + 6] for j in range(KV_HEADS)]
    big = chains[0][4].shape[0] - 8
    nbig = (ntile * tv - ctx_len) // big
    qi = pl.program_id(1) + first_tile

    def scores(j, start, s_ref):
        if not isinstance(start, int):
            start = pl.multiple_of(start, tv)
        n = s_ref.shape[0] - 8
        sc = jnp.dot(k_ref[pl.ds(start, n), :], chains[j][0][...], preferred_element_type=F32)
        s_ref[0:n, :] = sc
        s_ref[n:n + 8, :] = jnp.broadcast_to(jnp.max(sc, axis=0, keepdims=True), (8, sc.shape[1]))

    def update(j, start, s_ref):
        m_ref, acc_ref = chains[j][1], chains[j][2]
        n = s_ref.shape[0] - 8
        m_prev = m_ref[...]
        m_new = jnp.maximum(m_prev, s_ref[n:n + 1, :])
        alpha = jnp.exp2(m_prev - m_new)
        p = jnp.exp2(s_ref[0:n, :] - m_new).astype(BF16)
        t0 = start // tv
        vt = jnp.concatenate([v_ref[j, t0 + i] for i in range(n // tv)], axis=1)
        acc_ref[...] = alpha * acc_ref[...] + jnp.dot(vt, p, preferred_element_type=F32)
        m_ref[...] = m_new

    def latent(n):
        return ctx_len + n * big

    for j, (qs_ref, m_ref, acc_ref, sc_ref, _, _) in enumerate(chains):
        _stack_queries(q_ref.at[j * gw:(j + 1) * gw, :], qs_ref, j, tq)
        m_ref[...] = jnp.full(m_ref.shape, NEG_INF, F32)
        acc_ref[...] = jnp.zeros(acc_ref.shape, F32)
        scores(j, 0, sc_ref)

    @pl.when(qi * tq >= ctx_len)
    def _():
        for j, (_, _, _, sc_ref, sa_ref, _) in enumerate(chains):
            scores(j, latent(0), sa_ref)
            update(j, 0, sc_ref)

        def pair(u, _):
            for j, (_, _, _, _, sa_ref, sb_ref) in enumerate(chains):
                scores(j, latent(2 * u + 1), sb_ref)
                update(j, latent(2 * u), sa_ref)
            for j, (_, _, _, _, sa_ref, sb_ref) in enumerate(chains):
                scores(j, latent(2 * u + 2), sa_ref)
                update(j, latent(2 * u + 1), sb_ref)
            return 0

        lax.fori_loop(0, (nbig - 1) // 2, pair, 0)
        for j, (_, _, _, _, sa_ref, sb_ref) in enumerate(chains):
            if nbig % 2 == 0:
                scores(j, latent(nbig - 1), sb_ref)
                update(j, latent(nbig - 2), sa_ref)
                update(j, latent(nbig - 1), sb_ref)
            else:
                update(j, latent(nbig - 1), sa_ref)

    @pl.when(qi * tq < ctx_len)
    def _():
        for j in range(KV_HEADS):
            update(j, 0, chains[j][3])

    for j in range(KV_HEADS):
        acc_ref = chains[j][2]
        _store_heads(acc_ref[0:HEAD_DIM, :] / acc_ref[HEAD_DIM:HEAD_DIM + 1, :],
                     o_ref.at[:, j * gw:(j + 1) * gw], tq)


def _global_attention(qt, k, vt, ctx_len, first_tile):
    b, nq, s = qt.shape
    tq = ATT_TQ
    assert vt.shape[4] == ATT_TK and ctx_len % ATT_TK == 0 and (s - ctx_len) % ATT_BIG == 0
    cols = GROUP * tq
    per_head = [pltpu.VMEM((LANES, cols), BF16), pltpu.VMEM((1, cols), F32), pltpu.VMEM((LANES, cols), F32),
                pltpu.VMEM((ctx_len + 8, cols), F32), pltpu.VMEM((ATT_BIG + 8, cols), F32),
                pltpu.VMEM((ATT_BIG + 8, cols), F32)]
    return pl.pallas_call(
        functools.partial(_gattn_kernel, ctx_len=ctx_len, first_tile=first_tile),
        grid=(b, s // tq - first_tile),
        in_specs=[pl.BlockSpec((None, nq, tq), lambda bi, i: (bi, 0, i + first_tile)),
                  pl.BlockSpec((None, s, LANES), lambda bi, i: (bi, 0, 0)),
                  pl.BlockSpec((None,) + vt.shape[1:], lambda bi, i: (bi, 0, 0, 0, 0))],
        out_specs=pl.BlockSpec((None, tq, nq), lambda bi, i: (bi, i, 0)),
        out_shape=jax.ShapeDtypeStruct((b, s - first_tile * tq, nq), BF16),
        scratch_shapes=per_head * KV_HEADS,
        compiler_params=_cparams(("parallel", "arbitrary")),
        name="global_attention",
    )(qt, k, vt)


def _wattn_kernel(q_ref, k_ref, v_ref, sink_ref, bias_ref, o_ref, *scratch, ctx_len, span, first_tile):
    tq = q_ref.shape[1]
    s = k_ref.shape[0]
    gw = GROUP * HEAD_DIM
    qi = pl.program_id(1) + first_tile
    is_ctx = qi * tq < ctx_len

    def scores(j, window_start):
        qs_ref, sc_ref, sw_ref = scratch[3 * j:3 * j + 3]
        _stack_queries(q_ref.at[j * gw:(j + 1) * gw, :], qs_ref, j, tq)
        qs = qs_ref[...]

        def put(s_ref, sc):
            n = s_ref.shape[0] - 8
            s_ref[0:n, :] = sc
            s_ref[n:n + 8, :] = jnp.broadcast_to(jnp.max(sc, axis=0, keepdims=True), (8, sc.shape[1]))

        put(sc_ref, jnp.dot(k_ref[0:ctx_len, :], qs, preferred_element_type=F32))
        if window_start is not None:
            put(sw_ref, jnp.dot(k_ref[pl.ds(window_start, span), :], qs, preferred_element_type=F32)
                + bias_ref[...])

    def finish(j, window_start):
        _, sc_ref, sw_ref = scratch[3 * j:3 * j + 3]
        sink = sink_ref[j] * LOG2E
        v_ctx = jnp.concatenate([v_ref[j, t] for t in range(ctx_len // LANES)], axis=1)
        m = jnp.maximum(sc_ref[ctx_len:ctx_len + 1, :], sink)
        if window_start is not None:
            m = jnp.maximum(m, sw_ref[span:span + 1, :])
        acc = jnp.dot(v_ctx, jnp.exp2(sc_ref[0:ctx_len, :] - m).astype(BF16), preferred_element_type=F32)
        if window_start is not None:
            t0 = window_start // LANES
            v_win = jnp.concatenate([v_ref[j, t0 + i] for i in range(span // LANES)], axis=1)
            acc = acc + jnp.dot(v_win, jnp.exp2(sw_ref[0:span, :] - m).astype(BF16),
                                preferred_element_type=F32)
        den = acc[HEAD_DIM:HEAD_DIM + 1, :] + jnp.exp2(sink - m)
        _store_heads(acc[0:HEAD_DIM, :] / den, o_ref.at[:, j * gw:(j + 1) * gw], tq)

    def both(window_start):
        scores(0, window_start)
        scores(1, window_start)
        finish(0, window_start)
        finish(1, window_start)

    @pl.when(is_ctx)
    def _():
        both(None)

    @pl.when(jnp.logical_not(is_ctx))
    def _():
        both(pl.multiple_of(jnp.clip(qi * tq - WINDOW, ctx_len, s - span), LANES))


def _window_attention(qt, k, vt, sink, ctx_len, first_tile):
    b, nq, s = qt.shape
    tq = ATT_TQ
    gw = GROUP * HEAD_DIM
    span = tq + 2 * WINDOW
    assert tq >= WINDOW and (s - ctx_len) // tq >= 2
    sink_row = jnp.repeat(sink.reshape(KV_HEADS, 1, GROUP), tq, axis=2)
    r = jnp.arange(span)[:, None]
    qcol = (jnp.arange(GROUP * tq) % tq)[None, :]
    bias = jnp.stack([jnp.where(jnp.abs(off + r - qcol) <= WINDOW, 0.0, NEG_INF)
                      for off in (0, -WINDOW, tq - span)]).astype(F32)

    def placement(i):
        lo = (i + first_tile) * tq - WINDOW
        return jnp.where(lo < ctx_len, 0, jnp.where(lo > s - span, 2, 1))

    return pl.pallas_call(
        functools.partial(_wattn_kernel, ctx_len=ctx_len, span=span, first_tile=first_tile),
        grid=(b, s // tq - first_tile),
        in_specs=[pl.BlockSpec((None, nq, tq), lambda bi, i: (bi, 0, i + first_tile)),
                  pl.BlockSpec((None, s, LANES), lambda bi, i: (bi, 0, 0)),
                  pl.BlockSpec((None,) + vt.shape[1:], lambda bi, i: (bi, 0, 0, 0, 0)),
                  pl.BlockSpec((KV_HEADS, 1, GROUP * tq), lambda bi, i: (0, 0, 0)),
                  pl.BlockSpec((None, span, GROUP * tq), lambda bi, i: (placement(i), 0, 0))],
        out_specs=pl.BlockSpec((None, tq, nq), lambda bi, i: (bi, i, 0)),
        out_shape=jax.ShapeDtypeStruct((b, s - first_tile * tq, nq), BF16),
        scratch_shapes=[pltpu.VMEM((LANES, GROUP * tq), BF16), pltpu.VMEM((ctx_len + 8, GROUP * tq), F32),
                        pltpu.VMEM((span + 8, GROUP * tq), F32)] * KV_HEADS,
        compiler_params=_cparams(("parallel", "parallel")),
        name="window_attention",
    )(qt, k, vt, sink_row, bias)


def _rope_tables(n_lat, ctx_len):
    rows = n_lat // GRID_W
    row = jnp.repeat(jnp.arange(rows, dtype=F32), GRID_W)
    col = jnp.tile(jnp.arange(GRID_W, dtype=F32), rows)
    nf = HEAD_DIM // 4
    inv_freq = ROPE_THETA ** (-jnp.arange(nf, dtype=F32) / nf)
    ang_r = row[:, None] * inv_freq
    ang_c = col[:, None] * inv_freq
    cos = jnp.concatenate([jnp.cos(ang_r)] * 2 + [jnp.cos(ang_c)] * 2, axis=1)
    sin = jnp.concatenate([-jnp.sin(ang_r), jnp.sin(ang_r), -jnp.sin(ang_c), jnp.sin(ang_c)], axis=1)
    cos = jnp.concatenate([jnp.ones((ctx_len, HEAD_DIM), F32), cos], axis=0)
    sin = jnp.concatenate([jnp.zeros((ctx_len, HEAD_DIM), F32), sin], axis=0)
    return jnp.tile(cos, (1, LANES // HEAD_DIM)), jnp.tile(sin, (1, LANES // HEAD_DIM))


def kernel(x, c, ctx, c_ctx, ada_w, ada_b, ln1_g, ln1_b, ln2_g, ln2_b, ffn_w_in, ffn_w_out,
           ev_w_in, ev_w_out, rg_conv_w, rg_conv_b, rg_gate_w, rg_gate_b, rg_lambda, cm_w_s, cm_b_s,
           od_w_in, od_w_out, qn_g, kn_g, sink):
    b, t, d = x.shape
    ctx_len = ctx.shape[1]
    s = ctx_len + t
    depth = ada_w.shape[0]
    assert d == D_MODEL and depth == DEPTH and b + 1 <= 16
    assert ctx_len % ROW_TILE == 0 and t % ROW_TILE == 0 and t % GRID_W == 0 and ATT_TQ == ATT_TK == ROW_TILE

    c_rows = jnp.zeros((16, d), F32).at[:b].set(c).at[b].set(c_ctx)
    mods = _ada_vectors(c_rows, ada_w, ada_b).reshape(depth, 16, 6, d)
    modv = jnp.zeros((depth, b, 16, d), F32)
    modv = modv.at[:, :, 0:6].set(mods[:, :b])
    modv = modv.at[:, :, 8:14].set(jnp.broadcast_to(mods[:, b][:, None], (depth, b, 6, d)))

    cos_t, sin_t = _rope_tables(t, ctx_len)
    nq = Q_HEADS * HEAD_DIM
    bd = jnp.kron(jnp.eye(nq // HEAD_DIM, dtype=F32), jnp.full((HEAD_DIM, HEAD_DIM), 1.0 / HEAD_DIM, F32)).astype(BF16)
    assert sum(FFN_CHUNKS) == FFN_HIDDEN

    h = jnp.concatenate([ctx, x], axis=1)
    rows_ctx = ctx_len
    for l in range(depth):
        j = l // 2
        mv = modv[l]
        norms = (ln1_g[l].reshape(1, d), ln1_b[l].reshape(1, d), ffn_w_in[l].astype(BF16),
                 ffn_w_out[l].astype(BF16), ln2_g[l].reshape(1, d), ln2_b[l].reshape(1, d))
        if l % 2 == 0:
            gg, xr, gu, vn = _in_even(h, mv, ev_w_in[j].astype(BF16), ctx_len)
            gw = jnp.transpose(rg_gate_w[j], (2, 0, 3, 1, 4)).reshape(RNN_HEADS, 2, LANES, 2 * LANES).astype(BF16)
            gb = jnp.transpose(rg_gate_b[j].reshape(2, 2, RNN_HEADS, LANES), (2, 0, 1, 3)).reshape(RNN_HEADS, 2, 1, 2 * LANES)
            mr = _rglru(xr, gg, rg_conv_w[j], rg_conv_b[j].reshape(1, -1), gw, gb, rg_lambda[j], ctx_len)
            bsb = jnp.broadcast_to(cm_b_s[j][:, :, None], (CMLP_GROUPS, CHUNK, CMLP_WIDTH // CMLP_GROUPS))
            h = _post_even(h, mv, mr, gu, vn, cm_w_s[j].astype(BF16), bsb, ev_w_out[j].astype(BF16), *norms, ctx_len)
        else:
            gq = jnp.tile(qn_g[j], Q_HEADS).reshape(1, nq)
            gk = jnp.tile(kn_g[j], LANES // HEAD_DIM).reshape(1, LANES)
            qc, qd, kc, vc, kd, vd = _in_odd(h, mv, od_w_in[j].astype(BF16), bd, gq, gk, cos_t, sin_t, ctx_len)
            skip = ctx_len // ATT_TQ if l == depth - 1 else 0
            yc = _global_attention(qc, kc, vc, ctx_len, skip)
            yd = _window_attention(qd, kd, vd, sink[j], ctx_len, skip)
            h = _post_odd(h, mv, yc, yd, od_w_out[j].astype(BF16), *norms, ctx_len, skip)
            if skip:
                rows_ctx = 0
    return h[:, rows_ctx:, :]
```
